```python
import math
import jax, jax.numpy as jnp
from jax import lax
import numpy as np

D_MODEL = 1024
BATCH = 4
SEQ = 8192
DEPTH = 1
DEC_BATCH = 32
DEC_SEQ = 4
PAST_LEN = 16384
PAGE_SIZE = 128

MIX_WIDTH = D_MODEL
HEAD_DIM = 64
ATTN_WIDTH = MIX_WIDTH // 2
N_HEADS = ATTN_WIDTH // HEAD_DIM
DILATION_PATTERNS = ((128, 1), (512, 4), (2048, 16))
WINDOW_MAX = 2048
ATTN_BLOCK = 128
POOL_WIDTH = MIX_WIDTH - ATTN_WIDTH
POOL_SIZES = (2, 4, 8, 16)
N_POOL_GROUPS = len(POOL_SIZES)
POOL_GROUP_DIM = POOL_WIDTH // N_POOL_GROUPS
POOL_STATE_LEN = max(POOL_SIZES) - 1
IN_PROJ_WIDTH = 3 * ATTN_WIDTH + POOL_WIDTH
N_EXPERTS = 32
TOP_K = 4
D_FF = D_MODEL
SWIGLU_ALPHA = 1.702
SWIGLU_LIMIT = 7.0
MOE_BLOCK = 128
LN_EPS = 1e-5
DEEPNORM_ALPHA = (2 * DEPTH) ** 0.25
DEEPNORM_BETA = (8 * DEPTH) ** -0.25
NEG_INF = -1e30

kernel_name = "hymba_dilated_attn_pool_moe_step"


def _alibi_slopes():
    return 2.0 ** (-8.0 * jnp.arange(1, N_HEADS + 1, dtype=jnp.float32) / N_HEADS)


def _layer_norm(x, g, b):
    xf = x.astype(jnp.float32)
    mu = jnp.mean(xf, -1, keepdims=True)
    var = jnp.mean(jnp.square(xf - mu), -1, keepdims=True)
    return ((xf - mu) * lax.rsqrt(var + LN_EPS) * g.astype(jnp.float32) + b.astype(jnp.float32)).astype(x.dtype)


def _masked_softmax_lse(s, mask):
    s = jnp.where(mask, s, NEG_INF)
    m = jnp.max(s, axis=-1, keepdims=True)
    p = jnp.exp(s - m)
    den = jnp.sum(p, axis=-1, keepdims=True)
    return p / den, m[..., 0] + jnp.log(den[..., 0])


def _project(x, w_in):
    B, T, _ = x.shape
    h = jnp.einsum("btd,de->bte", x, w_in)
    q = h[..., :ATTN_WIDTH].reshape(B, T, N_HEADS, HEAD_DIM)
    k = h[..., ATTN_WIDTH:2 * ATTN_WIDTH].reshape(B, T, N_HEADS, HEAD_DIM)
    v = h[..., 2 * ATTN_WIDTH:3 * ATTN_WIDTH].reshape(B, T, N_HEADS, HEAD_DIM)
    u = h[..., 3 * ATTN_WIDTH:]
    return q, k, v, u


def _dilated_attn_prompt(q, k, v, slopes, dilation, n_back):
    B, S, H, Dh = q.shape
    L = S // dilation
    nb = -(-L // ATTN_BLOCK)
    Lp = nb * ATTN_BLOCK
    N = B * dilation

    def fold(a):
        a = a.reshape(B, L, dilation, H, Dh).transpose(0, 2, 1, 3, 4).reshape(N, L, H, Dh)
        return jnp.pad(a, ((0, 0), (0, Lp - L), (0, 0), (0, 0)))

    def band(a):
        a = jnp.pad(a, ((0, 0), (ATTN_BLOCK, 0), (0, 0), (0, 0))).reshape(N, nb + 1, ATTN_BLOCK, H, Dh)
        return jnp.concatenate([a[:, :-1], a[:, 1:]], axis=2)

    qb = fold(q).reshape(N, nb, ATTN_BLOCK, H, Dh)
    kb = band(fold(k))
    vb = band(fold(v))
    s = jnp.einsum("nbqhd,nbkhd->nbhqk", qb, kb, preferred_element_type=jnp.float32) * (Dh ** -0.5)
    qi = jnp.arange(ATTN_BLOCK)[:, None]
    kj = jnp.arange(2 * ATTN_BLOCK)[None, :]
    step = qi - kj + ATTN_BLOCK
    key_sub = (jnp.arange(nb) * ATTN_BLOCK - ATTN_BLOCK)[:, None, None] + kj[None]
    mask = (step >= 0) & (step <= n_back) & (key_sub >= 0)
    bias = -slopes[:, None, None] * (step * dilation).astype(jnp.float32)[None]
    p, lse = _masked_softmax_lse(s + bias[None, None], mask[None, :, None])
    o = jnp.einsum("nbhqk,nbkhd->nbqhd", p.astype(v.dtype), vb, preferred_element_type=jnp.float32)
    o = o.reshape(N, Lp, H, Dh)[:, :L].reshape(B, dilation, L, H, Dh).transpose(0, 2, 1, 3, 4).reshape(B, S, H, Dh)
    lse = lse.transpose(0, 1, 3, 2).reshape(N, Lp, H)[:, :L]
    lse = lse.reshape(B, dilation, L, H).transpose(0, 2, 1, 3).reshape(B, S, H)
    return o, lse


def _dilated_attn_sample(q, k_ext, v_ext, slopes, dilation, n_back):
    T = q.shape[1]
    Lb = k_ext.shape[1] - T
    j = jnp.arange(n_back + 1)
    idx = Lb + jnp.arange(T)[:, None] - j[None, :] * dilation
    valid = idx >= 0
    idx = jnp.maximum(idx, 0)
    kg = k_ext[:, idx]
    vg = v_ext[:, idx]
    s = jnp.einsum("bthd,btjhd->bhtj", q, kg, preferred_element_type=jnp.float32) * (q.shape[-1] ** -0.5)
    s = s - slopes[:, None, None] * (j * dilation).astype(jnp.float32)[None, :]
    p, lse = _masked_softmax_lse(s, valid[None, None])
    o = jnp.einsum("bhtj,btjhd->bthd", p.astype(v_ext.dtype), vg, preferred_element_type=jnp.float32)
    return o, lse.transpose(0, 2, 1)


def _merge_dilations(outs, lses):
    w = jax.nn.softmax(jnp.stack(lses, 0), axis=0)
    return jnp.sum(w[..., None] * jnp.stack(outs, 0), axis=0)


def _multiscale_pool(u_ext, pos, pool_w, pool_scale):
    B, Le, _ = u_ext.shape
    T = Le - POOL_STATE_LEN
    uf = u_ext.astype(jnp.float32)
    csum = jnp.pad(jnp.cumsum(uf, axis=1), ((0, 0), (1, 0), (0, 0)))
    means = []
    for g, size in enumerate(POOL_SIZES):
        sl = slice(g * POOL_GROUP_DIM, (g + 1) * POOL_GROUP_DIM)
        lo = POOL_STATE_LEN + 1 - size
        win = csum[:, POOL_STATE_LEN + 1:, sl] - csum[:, lo:lo + T, sl]
        cnt = jnp.minimum(pos + 1, size).astype(jnp.float32)
        means.append(win / cnt[None, :, None])
    d = (jnp.concatenate(means, -1) - uf[:, POOL_STATE_LEN:]).reshape(B, T, N_POOL_GROUPS, POOL_GROUP_DIM)
    y = jnp.einsum("btgc,gce->btge", d, pool_w.astype(jnp.float32)).reshape(B, T, POOL_WIDTH)
    return y * pool_scale.astype(jnp.float32)


def _mixer_prompt(x, w_in, w_out, pool_w, pool_scale, slopes):
    B, S, _ = x.shape
    q, k, v, u = _project(x, w_in)
    outs, lses = [], []
    for window, dil in DILATION_PATTERNS:
        o, l = _dilated_attn_prompt(q, k, v, slopes, dil, window // dil)
        outs.append(o)
        lses.append(l)
    attn = _merge_dilations(outs, lses).reshape(B, S, ATTN_WIDTH)
    u_ext = jnp.pad(u, ((0, 0), (POOL_STATE_LEN, 0), (0, 0)))
    pool = _multiscale_pool(u_ext, jnp.arange(S), pool_w, pool_scale)
    y = jnp.concatenate([attn.astype(x.dtype), pool.astype(x.dtype)], -1) @ w_out
    n_keep = min(WINDOW_MAX, S)
    return y, k[:, S - n_keep:], v[:, S - n_keep:], u_ext[:, -POOL_STATE_LEN:]


def _mixer_sample(x, ck, cv, cpool, w_in, w_out, pool_w, pool_scale, slopes):
    B, T, _ = x.shape
    q, k, v, u = _project(x, w_in)
    k_ext = jnp.concatenate([ck.astype(k.dtype), k], 1)
    v_ext = jnp.concatenate([cv.astype(v.dtype), v], 1)
    outs, lses = [], []
    for window, dil in DILATION_PATTERNS:
        o, l = _dilated_attn_sample(q, k_ext, v_ext, slopes, dil, window // dil)
        outs.append(o)
        lses.append(l)
    attn = _merge_dilations(outs, lses).reshape(B, T, ATTN_WIDTH)
    u_ext = jnp.concatenate([cpool.astype(u.dtype), u], 1)
    pool = _multiscale_pool(u_ext, PAST_LEN + jnp.arange(T), pool_w, pool_scale)
    y = jnp.concatenate([attn.astype(x.dtype), pool.astype(x.dtype)], -1) @ w_out
    n_keep = min(WINDOW_MAX, k_ext.shape[1])
    return y, k_ext[:, -n_keep:], v_ext[:, -n_keep:], u_ext[:, -POOL_STATE_LEN:]


def _moe(x, w_router, b_router, w_up, b_up, w_down, b_down):
    N, D = x.shape
    logits = jnp.dot(x.astype(jnp.float32), w_router.astype(jnp.float32)) + b_router.astype(jnp.float32)
    top_val, top_idx = lax.top_k(logits, TOP_K)
    gate = jax.nn.softmax(top_val, axis=-1)
    NK = N * TOP_K
    flat_e = top_idx.reshape(-1)
    order = jnp.argsort(flat_e, stable=True)
    sorted_e = flat_e[order]
    counts = jnp.bincount(flat_e, length=N_EXPERTS)
    padded = (counts + MOE_BLOCK - 1) // MOE_BLOCK * MOE_BLOCK
    start = jnp.cumsum(counts) - counts
    pend = jnp.cumsum(padded)
    pstart = pend - padded
    dest_sorted = pstart[sorted_e] + jnp.arange(NK) - start[sorted_e]
    dest = jnp.zeros((NK,), jnp.int32).at[order].set(dest_sorted.astype(jnp.int32))
    n_blocks = -(-NK // MOE_BLOCK) + N_EXPERTS
    n_rows = n_blocks * MOE_BLOCK
    row_tok = jnp.full((n_rows,), N, jnp.int32).at[dest].set(jnp.arange(NK, dtype=jnp.int32) // TOP_K)
    block_e = jnp.minimum(jnp.searchsorted(pend, jnp.arange(n_blocks) * MOE_BLOCK, side="right"), N_EXPERTS - 1)
    xs = jnp.concatenate([x, jnp.zeros((1, D), x.dtype)], 0)[row_tok].reshape(n_blocks, MOE_BLOCK, D)

    def expert_block(args):
        xb, e = args
        h = jnp.dot(xb, w_up[e], preferred_element_type=jnp.float32) + b_up[e].astype(jnp.float32)
        g = jnp.minimum(h[:, :D_FF], SWIGLU_LIMIT)
        lin = jnp.clip(h[:, D_FF:], -SWIGLU_LIMIT, SWIGLU_LIMIT)
        act = g * jax.nn.sigmoid(SWIGLU_ALPHA * g) * (lin + 1.0)
        return jnp.dot(act.astype(xb.dtype), w_down[e], preferred_element_type=jnp.float32) + b_down[e].astype(jnp.float32)

    ys = lax.map(expert_block, (xs, block_e)).reshape(n_rows, D)
    y = ys[dest].reshape(N, TOP_K, D)
    return jnp.einsum("nk,nkd->nd", gate, y).astype(x.dtype)


def _moe_seq(h, w_router, b_router, w_up, b_up, w_down, b_down):
    B, T, D = h.shape
    return _moe(h.reshape(B * T, D), w_router, b_router, w_up, b_up, w_down, b_down).reshape(B, T, D)


def setup_inputs(seed: int = 0) -> dict:
    key = jax.random.key(seed)
    ks = jax.random.split(key, 20)
    nrm = jax.random.normal
    f32 = jnp.float32
    win_len = min(WINDOW_MAX, PAST_LEN)
    return {
        "x_prompt": nrm(ks[0], (BATCH, SEQ, D_MODEL), f32),
        "x_sample": nrm(ks[1], (DEC_BATCH, DEC_SEQ, D_MODEL), f32),
        "cache_attn_k": nrm(ks[2], (DEPTH, DEC_BATCH, win_len, N_HEADS, HEAD_DIM), f32),
        "cache_attn_v": nrm(ks[3], (DEPTH, DEC_BATCH, win_len, N_HEADS, HEAD_DIM), f32),
        "state_pool": nrm(ks[4], (DEPTH, DEC_BATCH, POOL_STATE_LEN, POOL_WIDTH), f32),
        "w_in": nrm(ks[5], (DEPTH, D_MODEL, IN_PROJ_WIDTH), f32) * D_MODEL ** -0.5,
        "w_out": nrm(ks[6], (DEPTH, MIX_WIDTH, D_MODEL), f32) * (MIX_WIDTH ** -0.5 * DEEPNORM_BETA),
        "pool_w": nrm(ks[7], (DEPTH, N_POOL_GROUPS, POOL_GROUP_DIM, POOL_GROUP_DIM), f32) * POOL_GROUP_DIM ** -0.5,
        "pool_scale": 1.0 + 0.1 * nrm(ks[8], (DEPTH, POOL_WIDTH), f32),
        "ln1_g": 1.0 + 0.05 * nrm(ks[9], (DEPTH, D_MODEL), f32),
        "ln1_b": 0.02 * nrm(ks[10], (DEPTH, D_MODEL), f32),
        "w_router": nrm(ks[11], (DEPTH, D_MODEL, N_EXPERTS), f32) * D_MODEL ** -0.5,
        "b_router": 0.01 * nrm(ks[12], (DEPTH, N_EXPERTS), f32),
        "w_up": nrm(ks[13], (DEPTH, N_EXPERTS, D_MODEL, 2 * D_FF), f32) * D_MODEL ** -0.5,
        "b_up": 0.02 * nrm(ks[14], (DEPTH, N_EXPERTS, 2 * D_FF), f32),
        "w_down": nrm(ks[15], (DEPTH, N_EXPERTS, D_FF, D_MODEL), f32) * (D_FF ** -0.5 * DEEPNORM_BETA),
        "b_down": 0.02 * nrm(ks[16], (DEPTH, N_EXPERTS, D_MODEL), f32),
        "ln2_g": 1.0 + 0.05 * nrm(ks[17], (DEPTH, D_MODEL), f32),
        "ln2_b": 0.02 * nrm(ks[18], (DEPTH, D_MODEL), f32),
    }


def reference(x_prompt, x_sample, cache_attn_k, cache_attn_v, state_pool, w_in, w_out, pool_w, pool_scale,
              ln1_g, ln1_b, w_router, b_router, w_up, b_up, w_down, b_down, ln2_g, ln2_b):
    slopes = _alibi_slopes()
    hp, hs = x_prompt, x_sample
    kp_l, vp_l, pp_l, ks_l, vs_l, ps_l = [], [], [], [], [], []
    for l in range(DEPTH):
        mix_p, kp, vp, pp = _mixer_prompt(hp, w_in[l], w_out[l], pool_w[l], pool_scale[l], slopes)
        mix_s, kn, vn, pn = _mixer_sample(hs, cache_attn_k[l], cache_attn_v[l], state_pool[l],
                                          w_in[l], w_out[l], pool_w[l], pool_scale[l], slopes)
        kp_l.append(kp); vp_l.append(vp); pp_l.append(pp)
        ks_l.append(kn); vs_l.append(vn); ps_l.append(pn)
        hp = _layer_norm(DEEPNORM_ALPHA * hp + mix_p, ln1_g[l], ln1_b[l])
        hs = _layer_norm(DEEPNORM_ALPHA * hs + mix_s, ln1_g[l], ln1_b[l])
        hp = _layer_norm(DEEPNORM_ALPHA * hp + _moe_seq(hp, w_router[l], b_router[l], w_up[l], b_up[l], w_down[l], b_down[l]), ln2_g[l], ln2_b[l])
        hs = _layer_norm(DEEPNORM_ALPHA * hs + _moe_seq(hs, w_router[l], b_router[l], w_up[l], b_up[l], w_down[l], b_down[l]), ln2_g[l], ln2_b[l])
    return (hp, hs, jnp.stack(kp_l), jnp.stack(vp_l), jnp.stack(pp_l), jnp.stack(ks_l), jnp.stack(vs_l), jnp.stack(ps_l))
```

```python
import functools
import math

import jax
import jax.numpy as jnp
import numpy as np
from jax import lax
from jax.experimental import pallas as pl
from jax.experimental.pallas import tpu as pltpu

HEAD_DIM = 64
N_HEADS = 8
ATTN_WIDTH = N_HEADS * HEAD_DIM
DILATIONS = (1, 4, 16)
WINDOWS = (128, 512, 2048)
N_BACK = 128
WINDOW_MAX = 2048
ATTN_BLOCK = 128
POOL_SIZES = (2, 4, 8, 16)
POOL_GROUP_DIM = 128
POOL_WIDTH = 512
POOL_STATE_LEN = 15
N_EXPERTS = 32
TOP_K = 4
SWIGLU_ALPHA = 1.702
SWIGLU_LIMIT = 7.0
LN_EPS = 1e-5
NEG_INF = -1e30

LANES = 128
HEADS_PER_CHUNK = LANES // HEAD_DIM
N_CHUNKS = ATTN_WIDTH // LANES
VMEM_LIMIT = 56 * 1024 * 1024

ATTN_TILE = 2048
PROJ_TILE = 512
MOE_BLOCK = 512
ROW_TILE = 128

_F32 = jnp.float32
_BF16 = jnp.bfloat16


def _alibi_slopes():
    return 2.0 ** (-8.0 * np.arange(1, N_HEADS + 1, dtype=np.float64) / N_HEADS)


def _params(sem, **kw):
    return pltpu.CompilerParams(dimension_semantics=sem, vmem_limit_bytes=VMEM_LIMIT, **kw)


def _inproj_body(n_keep_tiles, x_ref, w_ref, q_ref, k_ref, v_ref, u_ref, kt_ref, vt_ref):
    i = pl.program_id(1)
    n_tiles = pl.num_programs(1)
    h = jnp.dot(x_ref[...].astype(_BF16), w_ref[...], preferred_element_type=_F32)
    for j in range(N_CHUNKS):
        q_ref[j] = h[:, j * LANES:(j + 1) * LANES]
        k_ref[j] = h[:, ATTN_WIDTH + j * LANES:ATTN_WIDTH + (j + 1) * LANES]
        v_ref[j] = h[:, 2 * ATTN_WIDTH + j * LANES:2 * ATTN_WIDTH + (j + 1) * LANES]
    u_ref[...] = h[:, 3 * ATTN_WIDTH:]

    @pl.when(i >= n_tiles - n_keep_tiles)
    def _():
        kt_ref[...] = h[:, ATTN_WIDTH:2 * ATTN_WIDTH].T
        vt_ref[...] = h[:, 2 * ATTN_WIDTH:3 * ATTN_WIDTH].T


def _inproj_prompt(x, w_in_bf16):
    B, S, D = x.shape
    tm = PROJ_TILE
    n_tiles = S // tm
    n_keep = min(WINDOW_MAX, S)
    n_keep_tiles = n_keep // tm
    first_keep = n_tiles - n_keep_tiles
    chunked = jax.ShapeDtypeStruct((B, N_CHUNKS, S, LANES), _F32)
    chunk_spec = pl.BlockSpec((None, N_CHUNKS, tm, LANES), lambda b, i: (b, 0, i, 0))
    t_spec = pl.BlockSpec((None, ATTN_WIDTH, tm), lambda b, i: (b, 0, jnp.maximum(i - first_keep, 0)))
    return pl.pallas_call(
        functools.partial(_inproj_body, n_keep_tiles),
        grid=(B, n_tiles),
        in_specs=[pl.BlockSpec((None, tm, D), lambda b, i: (b, i, 0)),
                  pl.BlockSpec(w_in_bf16.shape, lambda b, i: (0, 0))],
        out_specs=[chunk_spec, chunk_spec, chunk_spec,
                   pl.BlockSpec((None, tm, POOL_WIDTH), lambda b, i: (b, i, 0)),
                   t_spec, t_spec],
        out_shape=[chunked, chunked, chunked,
                   jax.ShapeDtypeStruct((B, S, POOL_WIDTH), _F32),
                   jax.ShapeDtypeStruct((B, ATTN_WIDTH, n_keep), _F32),
                   jax.ShapeDtypeStruct((B, ATTN_WIDTH, n_keep), _F32)],
        compiler_params=_params(("arbitrary", "arbitrary")),
        name="inproj_prompt",
    )(x, w_in_bf16)


def _attn_bias_tables():
    qi = np.arange(ATTN_BLOCK)[:, None]
    kj = np.arange(2 * ATTN_BLOCK)[None, :]
    step = qi - kj + ATTN_BLOCK
    valid = (step >= 0) & (step <= N_BACK)
    slopes = _alibi_slopes()
    out = np.zeros((N_CHUNKS, len(DILATIONS), HEADS_PER_CHUNK * ATTN_BLOCK, 2 * ATTN_BLOCK), np.float32)
    for c in range(N_CHUNKS):
        for p, dil in enumerate(DILATIONS):
            for hh in range(HEADS_PER_CHUNK):
                bias = -slopes[c * HEADS_PER_CHUNK + hh] * (step * dil).astype(np.float64)
                out[c, p, hh * ATTN_BLOCK:(hh + 1) * ATTN_BLOCK] = np.where(valid, bias, NEG_INF)
    return jnp.asarray(out)


def _attn_body(q_ref, k_ref, v_ref, bias_ref, o_ref,
               qf1, kf1, vf1, qf4, kf4, vf4, qf16, kf16, vf16, acc_n, acc_m, acc_d):
    i = pl.program_id(2)
    T = ATTN_TILE
    folded = ((1, qf1, kf1, vf1), (4, qf4, kf4, vf4), (16, qf16, kf16, vf16))
    scale = HEAD_DIM ** -0.5

    for dil, _, kf, vf in folded:
        L = T // dil

        @pl.when(i == 0)
        def _():
            kf[:, 0:ATTN_BLOCK, :] = jnp.zeros((dil, ATTN_BLOCK, LANES), _BF16)
            vf[:, 0:ATTN_BLOCK, :] = jnp.zeros((dil, ATTN_BLOCK, LANES), _BF16)

        @pl.when(i > 0)
        def _():
            kf[:, 0:ATTN_BLOCK, :] = kf[:, L:L + ATTN_BLOCK, :]
            vf[:, 0:ATTN_BLOCK, :] = vf[:, L:L + ATTN_BLOCK, :]

    for dil, qf, kf, vf in folded:
        L = T // dil
        for r in range(dil):
            rows = pl.ds(r, L, stride=dil) if dil > 1 else pl.ds(0, L)
            qf[r] = (q_ref[rows, :] * scale).astype(_BF16)
            kf[r, ATTN_BLOCK:ATTN_BLOCK + L, :] = k_ref[rows, :].astype(_BF16)
            vf[r, ATTN_BLOCK:ATTN_BLOCK + L, :] = v_ref[rows, :].astype(_BF16)

    lane = lax.broadcasted_iota(jnp.int32, (ATTN_BLOCK, LANES), 1)
    first_head = lane < HEAD_DIM
    col = lax.broadcasted_iota(jnp.int32, (HEADS_PER_CHUNK * ATTN_BLOCK, 2 * ATTN_BLOCK), 1)
    prev_half = col < ATTN_BLOCK

    for p, (dil, qf, kf, vf) in enumerate(folded):
        blocks_per_res = T // dil // ATTN_BLOCK

        def block(blk, carry, p=p, dil=dil, qf=qf, kf=kf, vf=vf, blocks_per_res=blocks_per_res):
            r = blk // blocks_per_res
            c = blk % blocks_per_res
            row0 = pl.multiple_of(c * ATTN_BLOCK, ATTN_BLOCK)
            q = qf[r, pl.ds(row0, ATTN_BLOCK), :]
            kk = kf[r, pl.ds(row0, 2 * ATTN_BLOCK), :]
            vv = vf[r, pl.ds(row0, 2 * ATTN_BLOCK), :]
            zero = jnp.zeros_like(q)
            qm = jnp.concatenate([jnp.where(first_head, q, zero), jnp.where(first_head, zero, q)], axis=0)
            s = lax.dot_general(qm, kk, (((1,), (1,)), ((), ())), preferred_element_type=_F32)
            s = s + bias_ref[p]
            s = jnp.where(jnp.logical_and(prev_half, jnp.logical_and(i == 0, c == 0)), NEG_INF, s)
            m = jnp.max(s, axis=-1, keepdims=True)
            e = jnp.exp(s - m)
            den = jnp.sum(e, axis=-1, keepdims=True)
            pv = jnp.dot(e.astype(_BF16), vv, preferred_element_type=_F32)
            num = jnp.where(first_head, pv[:ATTN_BLOCK], pv[ATTN_BLOCK:])
            m2 = jnp.where(first_head, m[:ATTN_BLOCK], m[ATTN_BLOCK:])
            d2 = jnp.where(first_head, den[:ATTN_BLOCK], den[ATTN_BLOCK:])
            start = r + row0 * dil
            rows = pl.ds(start, ATTN_BLOCK, stride=dil) if dil > 1 else pl.ds(start, ATTN_BLOCK)
            acc_n[p, rows, :] = num
            acc_m[p, rows, :] = m2
            acc_d[p, rows, :] = d2
            return carry

        lax.fori_loop(0, T // ATTN_BLOCK, block, 0)

    chunk_rows = 256

    def merge(t, carry):
        rows = pl.ds(pl.multiple_of(t * chunk_rows, chunk_rows), chunk_rows)
        m0, m1, m2 = acc_m[0, rows, :], acc_m[1, rows, :], acc_m[2, rows, :]
        mx = jnp.maximum(jnp.maximum(m0, m1), m2)
        e0, e1, e2 = jnp.exp(m0 - mx), jnp.exp(m1 - mx), jnp.exp(m2 - mx)
        num = acc_n[0, rows, :] * e0 + acc_n[1, rows, :] * e1 + acc_n[2, rows, :] * e2
        den = acc_d[0, rows, :] * e0 + acc_d[1, rows, :] * e1 + acc_d[2, rows, :] * e2
        o_ref[rows, :] = (num / den).astype(o_ref.dtype)
        return carry

    lax.fori_loop(0, T // chunk_rows, merge, 0)


def _attn_prompt(q, k, v, bias):
    B, _, S, _ = q.shape
    T = ATTN_TILE
    io_spec = pl.BlockSpec((None, None, T, LANES), lambda b, c, i: (b, c, i, 0))
    scratch = []
    for dil in DILATIONS:
        L = T // dil
        scratch += [pltpu.VMEM((dil, L, LANES), _BF16),
                    pltpu.VMEM((dil, ATTN_BLOCK + L, LANES), _BF16),
                    pltpu.VMEM((dil, ATTN_BLOCK + L, LANES), _BF16)]
    scratch += [pltpu.VMEM((len(DILATIONS), T, LANES), _F32)] * 3
    return pl.pallas_call(
        _attn_body,
        grid=(B, N_CHUNKS, S // T),
        in_specs=[io_spec, io_spec, io_spec,
                  pl.BlockSpec((None,) + bias.shape[1:], lambda b, c, i: (c, 0, 0, 0))],
        out_specs=io_spec,
        out_shape=jax.ShapeDtypeStruct((B, N_CHUNKS, S, LANES), _BF16),
        scratch_shapes=scratch,
        compiler_params=_params(("arbitrary", "arbitrary", "arbitrary")),
        name="attn_prompt",
    )(q, k, v, bias)


def _layer_norm(z, g, b):
    mu = jnp.mean(z, axis=-1, keepdims=True)
    zc = z - mu
    var = jnp.mean(zc * zc, axis=-1, keepdims=True)
    return zc * lax.rsqrt(var + LN_EPS) * g + b


def _pack_bf16_pairs(h):
    w = h.shape[1] // 2
    bits = pltpu.bitcast(h.astype(_BF16).astype(_F32), jnp.uint32)
    return (bits[:, w:] & jnp.uint32(0xFFFF0000)) | (bits[:, :w] >> 16)


def _unpack_bf16_pairs(words):
    lo = pltpu.bitcast(words << 16, _F32)
    hi = pltpu.bitcast(words & jnp.uint32(0xFFFF0000), _F32)
    return jnp.concatenate([lo, hi], axis=1).astype(_BF16)


def _route(h1, wr_ref, br_ref, cnt_ref, idx_ref, gate_ref, rank_ref):
    tm = h1.shape[0]
    logits = lax.dot_general(wr_ref[...], h1, (((1,), (1,)), ((), ())),
                             precision=lax.Precision.HIGHEST, preferred_element_type=_F32) + br_ref[...]
    eid = lax.broadcasted_iota(jnp.int32, (N_EXPERTS, tm), 0)
    work = logits
    vals, ids, hots = [], [], []
    for _ in range(TOP_K):
        mx = jnp.max(work, axis=0, keepdims=True)
        sel = jnp.min(jnp.where(work == mx, eid, N_EXPERTS), axis=0, keepdims=True)
        hot = eid == sel
        vals.append(mx)
        ids.append(sel)
        hots.append(hot)
        work = jnp.where(hot, -jnp.inf, work)
    ex = [jnp.exp(v - vals[0]) for v in vals]
    tot = ex[0] + ex[1] + ex[2] + ex[3]
    any_hot = jnp.where(hots[0] | hots[1] | hots[2] | hots[3], 1.0, 0.0)
    earlier = (lax.broadcasted_iota(jnp.int32, (tm, tm), 0) < lax.broadcasted_iota(jnp.int32, (tm, tm), 1))
    prefix = jnp.dot(any_hot.astype(_BF16), jnp.where(earlier, 1.0, 0.0).astype(_BF16),
                     preferred_element_type=_F32) + cnt_ref[...]
    for kk in range(TOP_K):
        idx_ref[kk:kk + 1, :] = ids[kk]
        gate_ref[kk:kk + 1, :] = ex[kk] / tot
        rank_ref[kk:kk + 1, :] = jnp.sum(jnp.where(hots[kk], prefix, 0.0), axis=0, keepdims=True).astype(jnp.int32)
    cnt_ref[...] = cnt_ref[...] + jnp.sum(any_hot, axis=1, keepdims=True)


def _pool_mix(win_fn, u_tile, cnt_fn, pw_ref, ps_ref):
    outs = []
    for g, size in enumerate(POOL_SIZES):
        lanes = slice(g * POOL_GROUP_DIM, (g + 1) * POOL_GROUP_DIM)
        win = win_fn(0, lanes)
        for j in range(1, size):
            win = win + win_fn(j, lanes)
        d = win / cnt_fn(size) - u_tile[:, lanes]
        y = jnp.dot(d.astype(_BF16), pw_ref[g], preferred_element_type=_F32)
        outs.append(y * ps_ref[:, lanes])
    return jnp.concatenate(outs, axis=1)


def _mix_prompt_body(alpha, x_ref, a_ref, u_ref, wo_ref, pw_ref, ps_ref, g_ref, b_ref, wr_ref, br_ref,
                     h1_ref, hp_ref, idx_ref, gate_ref, rank_ref, cnt_out_ref, uext, cnt_ref):
    b = pl.program_id(0)
    i = pl.program_id(1)
    tm = x_ref.shape[0]
    halo = 16

    @pl.when(jnp.logical_and(b == 0, i == 0))
    def _():
        cnt_ref[...] = jnp.zeros_like(cnt_ref)

    @pl.when(i == 0)
    def _():
        uext[0:halo, :] = jnp.zeros((halo, POOL_WIDTH), _F32)

    @pl.when(i > 0)
    def _():
        uext[0:halo, :] = uext[tm:tm + halo, :]

    uext[halo:halo + tm, :] = u_ref[...]
    pos = i * tm + lax.broadcasted_iota(jnp.int32, (tm, 1), 0)
    pool = _pool_mix(lambda j, lanes: uext[halo - j:halo - j + tm, lanes], u_ref,
                     lambda size: jnp.minimum(pos + 1, size).astype(_F32), pw_ref, ps_ref)
    cat = jnp.concatenate([a_ref[j] for j in range(N_CHUNKS)] + [pool.astype(_BF16)], axis=1)
    mix = jnp.dot(cat, wo_ref[...], preferred_element_type=_F32)
    h1 = _layer_norm(alpha * x_ref[...] + mix, g_ref[...], b_ref[...])
    h1_ref[...] = h1
    hp_ref[...] = _pack_bf16_pairs(h1)
    _route(h1, wr_ref, br_ref, cnt_ref, idx_ref, gate_ref, rank_ref)
    cnt_out_ref[...] = cnt_ref[...]


def _mix_prompt(alpha, x, attn, u, w_out_bf16, pool_w_bf16, pool_scale, ln_g, ln_b, wr_t, br):
    B, S, D = x.shape
    n_total = B * S
    tm = PROJ_TILE
    n_tiles = S // tm
    tok = lambda b, i: (b * n_tiles + i, 0)
    tok_t = lambda b, i: (0, b * n_tiles + i)
    const2 = lambda b, i: (0, 0)
    return pl.pallas_call(
        functools.partial(_mix_prompt_body, alpha),
        grid=(B, n_tiles),
        in_specs=[pl.BlockSpec((None, tm, D), lambda b, i: (b, i, 0)),
                  pl.BlockSpec((None, N_CHUNKS, tm, LANES), lambda b, i: (b, 0, i, 0)),
                  pl.BlockSpec((None, tm, POOL_WIDTH), lambda b, i: (b, i, 0)),
                  pl.BlockSpec(w_out_bf16.shape, const2),
                  pl.BlockSpec(pool_w_bf16.shape, lambda b, i: (0, 0, 0)),
                  pl.BlockSpec(pool_scale.shape, const2),
                  pl.BlockSpec(ln_g.shape, const2), pl.BlockSpec(ln_b.shape, const2),
                  pl.BlockSpec(wr_t.shape, const2), pl.BlockSpec(br.shape, const2)],
        out_specs=[pl.BlockSpec((tm, D), tok), pl.BlockSpec((tm, D // 2), tok),
                   pl.BlockSpec((TOP_K, tm), tok_t), pl.BlockSpec((TOP_K, tm), tok_t),
                   pl.BlockSpec((TOP_K, tm), tok_t), pl.BlockSpec((N_EXPERTS, 1), const2)],
        out_shape=[jax.ShapeDtypeStruct((n_total, D), _F32), jax.ShapeDtypeStruct((n_total, D // 2), jnp.uint32),
                   jax.ShapeDtypeStruct((TOP_K, n_total), jnp.int32), jax.ShapeDtypeStruct((TOP_K, n_total), _F32),
                   jax.ShapeDtypeStruct((TOP_K, n_total), jnp.int32), jax.ShapeDtypeStruct((N_EXPERTS, 1), _F32)],
        scratch_shapes=[pltpu.VMEM((16 + tm, POOL_WIDTH), _F32), pltpu.VMEM((N_EXPERTS, 1), _F32)],
        compiler_params=_params(("arbitrary", "arbitrary")),
        name="mix_prompt",
    )(x, attn, u, w_out_bf16, pool_w_bf16, pool_scale, ln_g, ln_b, wr_t, br)


def _inproj_sample_body(x_ref, w_ref, o_ref):
    o_ref[...] = jnp.dot(x_ref[...].astype(_BF16), w_ref[...], preferred_element_type=_F32)


def _inproj_sample(x2d, w_in_bf16):
    return pl.pallas_call(
        _inproj_sample_body,
        out_shape=jax.ShapeDtypeStruct((x2d.shape[0], w_in_bf16.shape[1]), _F32),
        compiler_params=_params(None),
        name="inproj_sample",
    )(x2d, w_in_bf16)


def _decode_tables(T, Lb):
    slopes = _alibi_slopes()

    def mult(delta):
        m = np.zeros(delta.shape, np.float64)
        for win, dil in zip(WINDOWS, DILATIONS):
            m += ((delta % dil == 0) & (delta // dil <= win // dil) & (delta >= 0))
        return m

    t = np.arange(T)[:, None]
    d_cache = Lb + t - np.arange(Lb)[None, :]
    d_new = t - np.arange(T)[None, :]
    m_cache, m_new = mult(d_cache), mult(d_new)
    b_cache = np.where(m_cache > 0, -slopes[:, None, None] * d_cache[None], NEG_INF)
    b_new = np.where(m_new > 0, -slopes[:, None, None] * d_new[None], NEG_INF)
    f = lambda a: jnp.asarray(a.astype(np.float32))
    return f(b_cache), f(m_cache), f(b_new), f(m_new)


def _attn_sample_body(q_ref, kn_ref, vn_ref, knt_ref, vnt_ref, kc_ref, vc_ref, bc_ref, mc_ref, bn_ref, mn_ref,
                      o_ref, ko_ref, vo_ref):
    T = q_ref.shape[0]
    scale = HEAD_DIM ** -0.5
    rnd = lambda a: a.astype(_BF16).astype(_F32)
    new_col = lax.broadcasted_iota(jnp.int32, (T, T), 1)
    outs = []
    for h in range(N_HEADS):
        lanes = slice(h * HEAD_DIM, (h + 1) * HEAD_DIM)
        q = rnd(q_ref[:, lanes] * scale)
        kn = rnd(kn_ref[:, lanes])
        vn = rnd(vn_ref[:, lanes])
        s_c = jnp.dot(q.astype(_BF16), kc_ref[h].astype(_BF16), preferred_element_type=_F32) + bc_ref[h]
        s_n = bn_ref[h]
        for t in range(T):
            s_n = s_n + jnp.where(new_col == t, jnp.sum(q * kn[t:t + 1, :], axis=-1, keepdims=True), 0.0)
        m = jnp.maximum(jnp.max(s_c, axis=-1, keepdims=True), jnp.max(s_n, axis=-1, keepdims=True))
        e_c = mc_ref[...] * jnp.exp(s_c - m)
        e_n = mn_ref[...] * jnp.exp(s_n - m)
        den = jnp.sum(e_c, axis=-1, keepdims=True) + jnp.sum(e_n, axis=-1, keepdims=True)
        num = lax.dot_general(e_c.astype(_BF16), vc_ref[h].astype(_BF16), (((1,), (1,)), ((), ())),
                              preferred_element_type=_F32)
        e_nr = rnd(e_n)
        for t in range(T):
            num = num + e_nr[:, t:t + 1] * vn[t:t + 1, :]
        outs.append(num / den)
    o_ref[...] = jnp.concatenate(outs, axis=1)
    ko_ref[...] = jnp.concatenate([kc_ref[...][:, :, T:], knt_ref[...]], axis=-1)
    vo_ref[...] = jnp.concatenate([vc_ref[...][:, :, T:], vnt_ref[...]], axis=-1)


def _attn_sample(q, kn, vn, knt, vnt, kc, vc, tables):
    Bd, T, _ = q.shape
    Lb = kc.shape[-1]
    tok = pl.BlockSpec((None, T, ATTN_WIDTH), lambda b: (b, 0, 0))
    newt = pl.BlockSpec((None, N_HEADS, HEAD_DIM, T), lambda b: (b, 0, 0, 0))
    cache = pl.BlockSpec((None, N_HEADS, HEAD_DIM, Lb), lambda b: (b, 0, 0, 0))
    tabs = [pl.BlockSpec(t.shape, (lambda b, n=t.ndim: (0,) * n)) for t in tables]
    return pl.pallas_call(
        _attn_sample_body,
        grid=(Bd,),
        in_specs=[tok, tok, tok, newt, newt, cache, cache] + tabs,
        out_specs=[tok, cache, cache],
        out_shape=[jax.ShapeDtypeStruct((Bd, T, ATTN_WIDTH), _F32),
                   jax.ShapeDtypeStruct(kc.shape, _F32), jax.ShapeDtypeStruct(vc.shape, _F32)],
        compiler_params=_params(("arbitrary",)),
        name="attn_sample",
    )(q, kn, vn, knt, vnt, kc, vc, *tables)


def _mix_sample_body(alpha, T, x_ref, a_ref, ue_ref, wo_ref, pw_ref, ps_ref, g_ref, b_ref, wr_ref, br_ref, cnt_in_ref,
                     h1_ref, hp_ref, idx_ref, gate_ref, rank_ref, cnt_out_ref, cnt_ref):
    cnt_ref[...] = cnt_in_ref[...]
    hist = POOL_STATE_LEN
    pools = []
    for t in range(T):
        pools.append(_pool_mix(lambda j, lanes, t=t: ue_ref[hist + t - j, :, lanes], ue_ref[hist + t],
                               lambda size: float(size), pw_ref, ps_ref))
    pool = jnp.concatenate(pools, axis=0)
    cat = jnp.concatenate([a_ref[...].astype(_BF16), pool.astype(_BF16)], axis=1)
    mix = jnp.dot(cat, wo_ref[...], preferred_element_type=_F32)
    h1 = _layer_norm(alpha * x_ref[...] + mix, g_ref[...], b_ref[...])
    h1_ref[...] = h1
    hp_ref[...] = _pack_bf16_pairs(h1)
    _route(h1, wr_ref, br_ref, cnt_ref, idx_ref, gate_ref, rank_ref)
    cnt_out_ref[...] = cnt_ref[...]


def _mix_sample(alpha, T, x_tm, attn_tm, uext_tm, w_out_bf16, pool_w_bf16, pool_scale, ln_g, ln_b, wr_t, br, cnt_in):
    n_s, D = x_tm.shape
    full = lambda shape: pl.BlockSpec(shape, lambda i, n=len(shape): (0,) * n)
    ins = [x_tm, attn_tm, uext_tm, w_out_bf16, pool_w_bf16, pool_scale, ln_g, ln_b, wr_t, br, cnt_in]
    outs = [jax.ShapeDtypeStruct((n_s, D), _F32), jax.ShapeDtypeStruct((n_s, D // 2), jnp.uint32),
            jax.ShapeDtypeStruct((TOP_K, n_s), jnp.int32), jax.ShapeDtypeStruct((TOP_K, n_s), _F32),
            jax.ShapeDtypeStruct((TOP_K, n_s), jnp.int32), jax.ShapeDtypeStruct((N_EXPERTS, 1), _F32)]
    return pl.pallas_call(
        functools.partial(_mix_sample_body, alpha, T),
        grid=(1,),
        in_specs=[full(a.shape) for a in ins],
        out_specs=[full(o.shape) for o in outs],
        out_shape=outs,
        scratch_shapes=[pltpu.VMEM((N_EXPERTS, 1), _F32)],
        compiler_params=_params(("arbitrary",)),
        name="mix_sample",
    )(*ins)


def _dispatch_body(n_p_tiles, dest_hbm, hp_p_ref, hp_s_ref, xs_in, xs_hbm, tile, dest_smem, sem_idx, sem_rows):
    del xs_in
    i = pl.program_id(0)
    tm = tile.shape[0]
    cp = pltpu.make_async_copy(dest_hbm.at[i], dest_smem, sem_idx)
    cp.start()
    tile[...] = jnp.where(i < n_p_tiles, hp_p_ref[...], hp_s_ref[...])
    cp.wait()

    def row_copy(n, kk):
        return pltpu.make_async_copy(tile.at[pl.ds(n, 1)], xs_hbm.at[pl.ds(dest_smem[kk * tm + n], 1)], sem_rows)

    def issue(n, carry):
        for kk in range(TOP_K):
            row_copy(n, kk).start()
        return carry

    lax.fori_loop(0, tm, issue, 0)

    def drain(n, carry):
        for kk in range(TOP_K):
            row_copy(n, kk).wait()
        return carry

    lax.fori_loop(0, tm, drain, 0)


def _dispatch(dest_tiles, hp_p, hp_s, xs_zero):
    w = hp_p.shape[1]
    tm = ROW_TILE
    n_p_tiles, n_s_tiles = hp_p.shape[0] // tm, hp_s.shape[0] // tm
    return pl.pallas_call(
        functools.partial(_dispatch_body, n_p_tiles),
        grid=(n_p_tiles + n_s_tiles,),
        in_specs=[pl.BlockSpec(memory_space=pl.ANY),
                  pl.BlockSpec((tm, w), lambda i: (jnp.minimum(i, n_p_tiles - 1), 0)),
                  pl.BlockSpec((tm, w), lambda i: (jnp.maximum(i - n_p_tiles, 0), 0)),
                  pl.BlockSpec(memory_space=pl.ANY)],
        out_specs=pl.BlockSpec(memory_space=pl.ANY),
        out_shape=jax.ShapeDtypeStruct(xs_zero.shape, xs_zero.dtype),
        input_output_aliases={3: 0},
        scratch_shapes=[pltpu.VMEM((tm, w), jnp.uint32), pltpu.SMEM((TOP_K * tm,), jnp.int32),
                        pltpu.SemaphoreType.DMA(()), pltpu.SemaphoreType.DMA(())],
        compiler_params=_params(("arbitrary",)),
        name="moe_dispatch",
    )(dest_tiles, hp_p, hp_s, xs_zero)


def _experts_body(be_ref, na_ref, xs_ref, wu_ref, bu_ref, wd_ref, bd_ref, ys_ref, wu_bf, wd_bf):
    j = pl.program_id(0)
    d_ff = wd_ref.shape[0]
    changed = jnp.logical_or(j == 0, be_ref[j] != be_ref[jnp.maximum(j - 1, 0)])

    @pl.when(jnp.logical_and(j < na_ref[0], changed))
    def _():
        wu_bf[...] = wu_ref[...].astype(_BF16)
        wd_bf[...] = wd_ref[...].astype(_BF16)

    @pl.when(j < na_ref[0])
    def _():
        x = _unpack_bf16_pairs(xs_ref[...])
        h = jnp.dot(x, wu_bf[...], preferred_element_type=_F32) + bu_ref[...]
        g = jnp.minimum(h[:, :d_ff], SWIGLU_LIMIT)
        lin = jnp.clip(h[:, d_ff:], -SWIGLU_LIMIT, SWIGLU_LIMIT)
        act = g * jax.nn.sigmoid(SWIGLU_ALPHA * g) * (lin + 1.0)
        ys_ref[...] = jnp.dot(act.astype(_BF16), wd_bf[...], preferred_element_type=_F32) + bd_ref[...]

    @pl.when(j >= na_ref[0])
    def _():
        ys_ref[...] = jnp.zeros_like(ys_ref)


def _experts(block_e, n_active, xs, w_up, b_up, w_down, b_down):
    n_rows, w = xs.shape
    blk = MOE_BLOCK
    n_blocks = n_rows // blk
    d_model, d_ff2 = w_up.shape[1:]
    d_ff = w_down.shape[1]
    act_blk = lambda j, be, na: jnp.minimum(j, jnp.maximum(na[0] - 1, 0))
    grid_spec = pltpu.PrefetchScalarGridSpec(
        num_scalar_prefetch=2,
        grid=(n_blocks,),
        in_specs=[pl.BlockSpec((blk, w), lambda j, be, na: (act_blk(j, be, na), 0)),
                  pl.BlockSpec((None, d_model, d_ff2), lambda j, be, na: (be[j], 0, 0)),
                  pl.BlockSpec((None, 1, d_ff2), lambda j, be, na: (be[j], 0, 0)),
                  pl.BlockSpec((None, d_ff, d_model), lambda j, be, na: (be[j], 0, 0)),
                  pl.BlockSpec((None, 1, d_model), lambda j, be, na: (be[j], 0, 0))],
        out_specs=pl.BlockSpec((blk, d_model), lambda j, be, na: (j, 0)),
        scratch_shapes=[pltpu.VMEM((d_model, d_ff2), _BF16), pltpu.VMEM((d_ff, d_model), _BF16)],
    )
    return pl.pallas_call(
        _experts_body,
        grid_spec=grid_spec,
        out_shape=jax.ShapeDtypeStruct((n_rows, d_model), _F32),
        compiler_params=_params(("arbitrary",)),
        name="moe_experts",
    )(block_e, n_active, xs, w_up, b_up, w_down, b_down)


def _combine_body(alpha, n_p_tiles, dest_hbm, ys_hbm, h1_p_ref, h1_s_ref, gate_ref, g_ref, b_ref, o_p_ref, o_s_ref,
                  dest_smem, buf, sem_idx, sem_rows):
    i = pl.program_id(0)
    tm = o_p_ref.shape[0]
    cp = pltpu.make_async_copy(dest_hbm.at[i], dest_smem, sem_idx)
    cp.start()
    cp.wait()

    def row_copy(n, kk):
        return pltpu.make_async_copy(ys_hbm.at[pl.ds(dest_smem[kk * tm + n], 1)], buf.at[kk, pl.ds(n, 1)], sem_rows)

    def issue(n, carry):
        for kk in range(TOP_K):
            row_copy(n, kk).start()
        return carry

    lax.fori_loop(0, tm, issue, 0)

    def drain(n, carry):
        for kk in range(TOP_K):
            row_copy(n, kk).wait()
        return carry

    lax.fori_loop(0, tm, drain, 0)
    y = gate_ref[:, 0:1] * buf[0]
    for kk in range(1, TOP_K):
        y = y + gate_ref[:, kk:kk + 1] * buf[kk]
    @pl.when(i < n_p_tiles)
    def _():
        o_p_ref[...] = _layer_norm(alpha * h1_p_ref[...] + y, g_ref[...], b_ref[...])

    @pl.when(i >= n_p_tiles)
    def _():
        o_s_ref[...] = _layer_norm(alpha * h1_s_ref[...] + y, g_ref[...], b_ref[...])


def _combine(alpha, dest_tiles, ys, h1_p, h1_s, gate_rows, ln_g, ln_b):
    d_model = h1_p.shape[1]
    tm = ROW_TILE
    n_p_tiles, n_s_tiles = h1_p.shape[0] // tm, h1_s.shape[0] // tm
    const2 = lambda i: (0, 0)
    p_tile = lambda i: (jnp.minimum(i, n_p_tiles - 1), 0)
    s_tile = lambda i: (jnp.maximum(i - n_p_tiles, 0), 0)
    return pl.pallas_call(
        functools.partial(_combine_body, alpha, n_p_tiles),
        grid=(n_p_tiles + n_s_tiles,),
        in_specs=[pl.BlockSpec(memory_space=pl.ANY), pl.BlockSpec(memory_space=pl.ANY),
                  pl.BlockSpec((tm, d_model), p_tile), pl.BlockSpec((tm, d_model), s_tile),
                  pl.BlockSpec((tm, TOP_K), lambda i: (i, 0)),
                  pl.BlockSpec(ln_g.shape, const2), pl.BlockSpec(ln_b.shape, const2)],
        out_specs=[pl.BlockSpec((tm, d_model), p_tile), pl.BlockSpec((tm, d_model), s_tile)],
        out_shape=[jax.ShapeDtypeStruct(h1_p.shape, _F32), jax.ShapeDtypeStruct(h1_s.shape, _F32)],
        scratch_shapes=[pltpu.SMEM((TOP_K * tm,), jnp.int32), pltpu.VMEM((TOP_K, tm, d_model), _F32),
                        pltpu.SemaphoreType.DMA(()), pltpu.SemaphoreType.DMA(())],
        compiler_params=_params(("arbitrary",)),
        name="moe_combine",
    )(dest_tiles, ys, h1_p, h1_s, gate_rows, ln_g, ln_b)


def _moe_layout(idx_all, rank_all, counts, n_total):
    blk = MOE_BLOCK
    n_blocks = (TOP_K * n_total + N_EXPERTS * (blk - 1)) // blk + 1
    cnt = counts.reshape(N_EXPERTS).astype(jnp.int32)
    padded = (cnt + blk - 1) // blk * blk
    pend = jnp.cumsum(padded)
    pstart = pend - padded
    dest = pstart[idx_all] + rank_all
    block_e = jnp.minimum(jnp.searchsorted(pend, jnp.arange(n_blocks, dtype=jnp.int32) * blk, side="right"),
                          N_EXPERTS - 1).astype(jnp.int32)
    n_active = (pend[-1] // blk).astype(jnp.int32).reshape(1)
    last_e = block_e[jnp.maximum(n_active[0] - 1, 0)]
    block_e = jnp.where(jnp.arange(n_blocks) < n_active[0], block_e, last_e)
    tm = ROW_TILE
    dest_tiles = dest.reshape(TOP_K, n_total // tm, tm).transpose(1, 0, 2).reshape(n_total // tm, TOP_K * tm)
    return dest_tiles, block_e, n_active, n_blocks * blk


def kernel(x_prompt, x_sample, cache_attn_k, cache_attn_v, state_pool, w_in, w_out, pool_w, pool_scale, ln1_g, ln1_b,
           w_router, b_router, w_up, b_up, w_down, b_down, ln2_g, ln2_b):
    depth = w_in.shape[0]
    assert depth == 1, "single-layer step"
    B, S, D = x_prompt.shape
    Bd, T, _ = x_sample.shape
    Lb = cache_attn_k.shape[2]
    assert S % ATTN_TILE == 0 and Lb == WINDOW_MAX and (B * S) % ROW_TILE == 0 and (Bd * T) % ROW_TILE == 0
    alpha = (2 * depth) ** 0.25
    n_p, n_s = B * S, Bd * T
    n_total = n_p + n_s

    w_in_b = w_in[0].astype(_BF16)
    w_out_b = w_out[0].astype(_BF16)
    pool_w_b = pool_w[0].astype(_BF16)
    wr_t = w_router[0].T
    br = b_router[0].reshape(N_EXPERTS, 1)

    q, k, v, u, kt, vt = _inproj_prompt(x_prompt, w_in_b)
    attn = _attn_prompt(q, k, v, _attn_bias_tables())
    h1_p, hp_p, idx_p, gate_p, rank_p, cnt_p = _mix_prompt(
        alpha, x_prompt, attn, u, w_out_b, pool_w_b, pool_scale, ln1_g, ln1_b, wr_t, br)
    n_keep = kt.shape[-1]
    k_prompt = kt.reshape(1, B, N_HEADS, HEAD_DIM, n_keep).transpose(0, 1, 4, 2, 3)
    v_prompt = vt.reshape(1, B, N_HEADS, HEAD_DIM, n_keep).transpose(0, 1, 4, 2, 3)
    pool_prompt = u[:, S - POOL_STATE_LEN:][None]

    hs = _inproj_sample(x_sample.reshape(n_s, D), w_in_b)
    qs = hs[:, :ATTN_WIDTH].reshape(Bd, T, ATTN_WIDTH)
    ks = hs[:, ATTN_WIDTH:2 * ATTN_WIDTH].reshape(Bd, T, ATTN_WIDTH)
    vs = hs[:, 2 * ATTN_WIDTH:3 * ATTN_WIDTH].reshape(Bd, T, ATTN_WIDTH)
    us = hs[:, 3 * ATTN_WIDTH:].reshape(Bd, T, POOL_WIDTH)
    to_t = lambda a: a.reshape(Bd, T, N_HEADS, HEAD_DIM).transpose(0, 2, 3, 1)
    kc = cache_attn_k[0].transpose(0, 2, 3, 1)
    vc = cache_attn_v[0].transpose(0, 2, 3, 1)
    attn_s, k_new, v_new = _attn_sample(qs, ks, vs, to_t(ks), to_t(vs), kc, vc, _decode_tables(T, Lb))
    k_sample = k_new.transpose(0, 3, 1, 2)[None]
    v_sample = v_new.transpose(0, 3, 1, 2)[None]
    uext_tm = jnp.concatenate([state_pool[0].transpose(1, 0, 2), us.transpose(1, 0, 2)], axis=0)
    pool_sample = uext_tm[T:].transpose(1, 0, 2)[None]
    x_tm = x_sample.transpose(1, 0, 2).reshape(n_s, D)
    attn_tm = attn_s.transpose(1, 0, 2).reshape(n_s, ATTN_WIDTH)
    h1_s, hp_s, idx_s, gate_s, rank_s, counts = _mix_sample(
        alpha, T, x_tm, attn_tm, uext_tm, w_out_b, pool_w_b, pool_scale, ln1_g, ln1_b, wr_t, br, cnt_p)

    idx_all = jnp.concatenate([idx_p, idx_s], axis=1)
    rank_all = jnp.concatenate([rank_p, rank_s], axis=1)
    gate_rows = jnp.concatenate([gate_p, gate_s], axis=1).T
    dest_tiles, block_e, n_active, n_rows = _moe_layout(idx_all, rank_all, counts, n_total)
    xs = _dispatch(dest_tiles, hp_p, hp_s, jnp.zeros((n_rows, D // 2), jnp.uint32))
    ys = _experts(block_e, n_active, xs, w_up[0], b_up[0][:, None, :], w_down[0], b_down[0][:, None, :])
    out_p, out_s = _combine(alpha, dest_tiles, ys, h1_p, h1_s, gate_rows, ln2_g, ln2_b)

    y_prompt = out_p.reshape(B, S, D)
    y_sample = out_s.reshape(T, Bd, D).transpose(1, 0, 2)
    return (y_prompt, y_sample, k_prompt, v_prompt, pool_prompt, k_sample, v_sample, pool_sample)
```

```python
import functools
import math

import jax
import jax.numpy as jnp
import numpy as np
from jax import lax
from jax.experimental import pallas as pl
from jax.experimental.pallas import tpu as pltpu

HEAD_DIM = 64
N_HEADS = 8
ATTN_WIDTH = N_HEADS * HEAD_DIM
DILATIONS = (1, 4, 16)
WINDOWS = (128, 512, 2048)
N_BACK = 128
WINDOW_MAX = 2048
ATTN_BLOCK = 128
POOL_SIZES = (2, 4, 8, 16)
POOL_GROUP_DIM = 128
POOL_WIDTH = 512
POOL_STATE_LEN = 15
N_EXPERTS = 32
TOP_K = 4
SWIGLU_ALPHA = 1.702
SWIGLU_LIMIT = 7.0
LN_EPS = 1e-5
NEG_INF = -1e30

LANES = 128
HEADS_PER_CHUNK = LANES // HEAD_DIM
N_CHUNKS = ATTN_WIDTH // LANES
VMEM_LIMIT = 56 * 1024 * 1024

ATTN_TILE = 2048
BLOCK_UNROLL = 4
PROJ_TILE = 512
MOE_BLOCK = 512
ROW_TILE = 128

_F32 = jnp.float32
_BF16 = jnp.bfloat16


def _alibi_slopes():
    return 2.0 ** (-8.0 * np.arange(1, N_HEADS + 1, dtype=np.float64) / N_HEADS)


def _params(sem, **kw):
    return pltpu.CompilerParams(dimension_semantics=sem, vmem_limit_bytes=VMEM_LIMIT, **kw)


def _inproj_body(n_keep_tiles, x_ref, w_ref, q_ref, k_ref, v_ref, u_ref, kt_ref, vt_ref):
    i = pl.program_id(1)
    n_tiles = pl.num_programs(1)
    h = jnp.dot(x_ref[...].astype(_BF16), w_ref[...], preferred_element_type=_F32)
    for j in range(N_CHUNKS):
        q_ref[j] = h[:, j * LANES:(j + 1) * LANES]
        k_ref[j] = h[:, ATTN_WIDTH + j * LANES:ATTN_WIDTH + (j + 1) * LANES]
        v_ref[j] = h[:, 2 * ATTN_WIDTH + j * LANES:2 * ATTN_WIDTH + (j + 1) * LANES]
    u_ref[...] = h[:, 3 * ATTN_WIDTH:]

    @pl.when(i >= n_tiles - n_keep_tiles)
    def _():
        kt_ref[...] = h[:, ATTN_WIDTH:2 * ATTN_WIDTH].T
        vt_ref[...] = h[:, 2 * ATTN_WIDTH:3 * ATTN_WIDTH].T


def _inproj_prompt(x, w_in_bf16):
    B, S, D = x.shape
    tm = PROJ_TILE
    n_tiles = S // tm
    n_keep = min(WINDOW_MAX, S)
    n_keep_tiles = n_keep // tm
    first_keep = n_tiles - n_keep_tiles
    chunked = jax.ShapeDtypeStruct((B, N_CHUNKS, S, LANES), _F32)
    chunk_spec = pl.BlockSpec((None, N_CHUNKS, tm, LANES), lambda b, i: (b, 0, i, 0))
    t_spec = pl.BlockSpec((None, ATTN_WIDTH, tm), lambda b, i: (b, 0, jnp.maximum(i - first_keep, 0)))
    return pl.pallas_call(
        functools.partial(_inproj_body, n_keep_tiles),
        grid=(B, n_tiles),
        in_specs=[pl.BlockSpec((None, tm, D), lambda b, i: (b, i, 0)),
                  pl.BlockSpec(w_in_bf16.shape, lambda b, i: (0, 0))],
        out_specs=[chunk_spec, chunk_spec, chunk_spec,
                   pl.BlockSpec((None, tm, POOL_WIDTH), lambda b, i: (b, i, 0)),
                   t_spec, t_spec],
        out_shape=[chunked, chunked, chunked,
                   jax.ShapeDtypeStruct((B, S, POOL_WIDTH), _F32),
                   jax.ShapeDtypeStruct((B, ATTN_WIDTH, n_keep), _F32),
                   jax.ShapeDtypeStruct((B, ATTN_WIDTH, n_keep), _F32)],
        compiler_params=_params(("arbitrary", "arbitrary")),
        name="inproj_prompt",
    )(x, w_in_bf16)


def _attn_bias_tables():
    qi = np.arange(ATTN_BLOCK)[:, None]
    kj = np.arange(2 * ATTN_BLOCK)[None, :]
    step = qi - kj + ATTN_BLOCK
    valid = (step >= 0) & (step <= N_BACK)
    slopes = _alibi_slopes()
    out = np.zeros((N_CHUNKS, len(DILATIONS), HEADS_PER_CHUNK * ATTN_BLOCK, 2 * ATTN_BLOCK), np.float32)
    for c in range(N_CHUNKS):
        for p, dil in enumerate(DILATIONS):
            for hh in range(HEADS_PER_CHUNK):
                bias = -slopes[c * HEADS_PER_CHUNK + hh] * (step * dil).astype(np.float64)
                out[c, p, hh * ATTN_BLOCK:(hh + 1) * ATTN_BLOCK] = np.where(valid, bias, NEG_INF)
    return jnp.asarray(out)


def _attn_body(q_ref, k_ref, v_ref, bias_ref, o_ref,
               qf1, kf1, vf1, qf4, kf4, vf4, qf16, kf16, vf16, acc_n, acc_m, acc_d):
    i = pl.program_id(2)
    T = ATTN_TILE
    folded = ((1, qf1, kf1, vf1), (4, qf4, kf4, vf4), (16, qf16, kf16, vf16))
    scale = HEAD_DIM ** -0.5

    for dil, _, kf, vf in folded:
        L = T // dil

        @pl.when(i == 0)
        def _():
            kf[:, 0:ATTN_BLOCK, :] = jnp.zeros((dil, ATTN_BLOCK, LANES), _BF16)
            vf[:, 0:ATTN_BLOCK, :] = jnp.zeros((dil, ATTN_BLOCK, LANES), _BF16)

        @pl.when(i > 0)
        def _():
            kf[:, 0:ATTN_BLOCK, :] = kf[:, L:L + ATTN_BLOCK, :]
            vf[:, 0:ATTN_BLOCK, :] = vf[:, L:L + ATTN_BLOCK, :]

    for dil, qf, kf, vf in folded:
        L = T // dil
        for r in range(dil):
            rows = pl.ds(r, L, stride=dil) if dil > 1 else pl.ds(0, L)
            qf[r] = (q_ref[rows, :] * scale).astype(_BF16)
            kf[r, ATTN_BLOCK:ATTN_BLOCK + L, :] = k_ref[rows, :].astype(_BF16)
            vf[r, ATTN_BLOCK:ATTN_BLOCK + L, :] = v_ref[rows, :].astype(_BF16)

    lane = lax.broadcasted_iota(jnp.int32, (ATTN_BLOCK, LANES), 1)
    first_head = lane < HEAD_DIM
    col = lax.broadcasted_iota(jnp.int32, (HEADS_PER_CHUNK * ATTN_BLOCK, 2 * ATTN_BLOCK), 1)
    prev_half = col < ATTN_BLOCK

    for p, (dil, qf, kf, vf) in enumerate(folded):
        blocks_per_res = T // dil // ATTN_BLOCK

        def block(blk, carry, p=p, dil=dil, qf=qf, kf=kf, vf=vf, blocks_per_res=blocks_per_res):
            r = blk // blocks_per_res
            c = blk % blocks_per_res
            row0 = pl.multiple_of(c * ATTN_BLOCK, ATTN_BLOCK)
            q = qf[r, pl.ds(row0, ATTN_BLOCK), :]
            kk = kf[r, pl.ds(row0, 2 * ATTN_BLOCK), :]
            vv = vf[r, pl.ds(row0, 2 * ATTN_BLOCK), :]
            zero = jnp.zeros_like(q)
            qm = jnp.concatenate([jnp.where(first_head, q, zero), jnp.where(first_head, zero, q)], axis=0)
            s = lax.dot_general(qm, kk, (((1,), (1,)), ((), ())), preferred_element_type=_F32)
            s = s + bias_ref[p]
            s = jnp.where(jnp.logical_and(prev_half, jnp.logical_and(i == 0, c == 0)), NEG_INF, s)
            m = jnp.max(s, axis=-1, keepdims=True)
            e = jnp.exp(s - m)
            den = jnp.sum(e, axis=-1, keepdims=True)
            pv = jnp.dot(e.astype(_BF16), vv, preferred_element_type=_F32)
            num = jnp.where(first_head, pv[:ATTN_BLOCK], pv[ATTN_BLOCK:])
            m2 = jnp.where(first_head, m[:ATTN_BLOCK], m[ATTN_BLOCK:])
            d2 = jnp.where(first_head, den[:ATTN_BLOCK], den[ATTN_BLOCK:])
            start = r + row0 * dil
            rows = pl.ds(start, ATTN_BLOCK, stride=dil) if dil > 1 else pl.ds(start, ATTN_BLOCK)
            acc_n[p, rows, :] = num
            acc_m[p, rows, :] = m2
            acc_d[p, rows, :] = d2
            return carry

        lax.fori_loop(0, T // ATTN_BLOCK, block, 0, unroll=BLOCK_UNROLL)

    chunk_rows = 256

    def merge(t, carry):
        rows = pl.ds(pl.multiple_of(t * chunk_rows, chunk_rows), chunk_rows)
        m0, m1, m2 = acc_m[0, rows, :], acc_m[1, rows, :], acc_m[2, rows, :]
        mx = jnp.maximum(jnp.maximum(m0, m1), m2)
        e0, e1, e2 = jnp.exp(m0 - mx), jnp.exp(m1 - mx), jnp.exp(m2 - mx)
        num = acc_n[0, rows, :] * e0 + acc_n[1, rows, :] * e1 + acc_n[2, rows, :] * e2
        den = acc_d[0, rows, :] * e0 + acc_d[1, rows, :] * e1 + acc_d[2, rows, :] * e2
        o_ref[rows, :] = (num / den).astype(o_ref.dtype)
        return carry

    lax.fori_loop(0, T // chunk_rows, merge, 0)


def _attn_prompt(q, k, v, bias):
    B, _, S, _ = q.shape
    T = ATTN_TILE
    io_spec = pl.BlockSpec((None, None, T, LANES), lambda b, c, i: (b, c, i, 0))
    scratch = []
    for dil in DILATIONS:
        L = T // dil
        scratch += [pltpu.VMEM((dil, L, LANES), _BF16),
                    pltpu.VMEM((dil, ATTN_BLOCK + L, LANES), _BF16),
                    pltpu.VMEM((dil, ATTN_BLOCK + L, LANES), _BF16)]
    scratch += [pltpu.VMEM((len(DILATIONS), T, LANES), _F32)] * 3
    return pl.pallas_call(
        _attn_body,
        grid=(B, N_CHUNKS, S // T),
        in_specs=[io_spec, io_spec, io_spec,
                  pl.BlockSpec((None,) + bias.shape[1:], lambda b, c, i: (c, 0, 0, 0))],
        out_specs=io_spec,
        out_shape=jax.ShapeDtypeStruct((B, N_CHUNKS, S, LANES), _BF16),
        scratch_shapes=scratch,
        compiler_params=_params(("arbitrary", "arbitrary", "arbitrary")),
        name="attn_prompt",
    )(q, k, v, bias)


def _layer_norm(z, g, b):
    mu = jnp.mean(z, axis=-1, keepdims=True)
    zc = z - mu
    var = jnp.mean(zc * zc, axis=-1, keepdims=True)
    return zc * lax.rsqrt(var + LN_EPS) * g + b


def _pack_bf16_pairs(h):
    w = h.shape[1] // 2
    bits = pltpu.bitcast(h.astype(_BF16).astype(_F32), jnp.uint32)
    return (bits[:, w:] & jnp.uint32(0xFFFF0000)) | (bits[:, :w] >> 16)


def _unpack_bf16_pairs(words):
    lo = pltpu.bitcast(words << 16, _F32)
    hi = pltpu.bitcast(words & jnp.uint32(0xFFFF0000), _F32)
    return jnp.concatenate([lo, hi], axis=1).astype(_BF16)


def _route(h1, wr_ref, br_ref, cnt_ref, idx_ref, gate_ref, rank_ref):
    tm = h1.shape[0]
    logits = lax.dot_general(wr_ref[...], h1, (((1,), (1,)), ((), ())),
                             precision=lax.Precision.HIGHEST, preferred_element_type=_F32) + br_ref[...]
    eid = lax.broadcasted_iota(jnp.int32, (N_EXPERTS, tm), 0)
    work = logits
    vals, ids, hots = [], [], []
    for _ in range(TOP_K):
        mx = jnp.max(work, axis=0, keepdims=True)
        sel = jnp.min(jnp.where(work == mx, eid, N_EXPERTS), axis=0, keepdims=True)
        hot = eid == sel
        vals.append(mx)
        ids.append(sel)
        hots.append(hot)
        work = jnp.where(hot, -jnp.inf, work)
    ex = [jnp.exp(v - vals[0]) for v in vals]
    tot = ex[0] + ex[1] + ex[2] + ex[3]
    any_hot = jnp.where(hots[0] | hots[1] | hots[2] | hots[3], 1.0, 0.0)
    earlier = (lax.broadcasted_iota(jnp.int32, (tm, tm), 0) < lax.broadcasted_iota(jnp.int32, (tm, tm), 1))
    prefix = jnp.dot(any_hot.astype(_BF16), jnp.where(earlier, 1.0, 0.0).astype(_BF16),
                     preferred_element_type=_F32) + cnt_ref[...]
    for kk in range(TOP_K):
        idx_ref[kk:kk + 1, :] = ids[kk]
        gate_ref[kk:kk + 1, :] = ex[kk] / tot
        rank_ref[kk:kk + 1, :] = jnp.sum(jnp.where(hots[kk], prefix, 0.0), axis=0, keepdims=True).astype(jnp.int32)
    cnt_ref[...] = cnt_ref[...] + jnp.sum(any_hot, axis=1, keepdims=True)


def _pool_mix(win_fn, u_tile, cnt_fn, pw_ref, ps_ref):
    outs = []
    for g, size in enumerate(POOL_SIZES):
        lanes = slice(g * POOL_GROUP_DIM, (g + 1) * POOL_GROUP_DIM)
        win = win_fn(0, lanes)
        for j in range(1, size):
            win = win + win_fn(j, lanes)
        d = win / cnt_fn(size) - u_tile[:, lanes]
        y = jnp.dot(d.astype(_BF16), pw_ref[g], preferred_element_type=_F32)
        outs.append(y * ps_ref[:, lanes])
    return jnp.concatenate(outs, axis=1)


def _mix_prompt_body(alpha, x_ref, a_ref, u_ref, wo_ref, pw_ref, ps_ref, g_ref, b_ref, wr_ref, br_ref,
                     h1_ref, hp_ref, idx_ref, gate_ref, rank_ref, cnt_out_ref, uext, cnt_ref):
    b = pl.program_id(0)
    i = pl.program_id(1)
    tm = x_ref.shape[0]
    halo = 16

    @pl.when(jnp.logical_and(b == 0, i == 0))
    def _():
        cnt_ref[...] = jnp.zeros_like(cnt_ref)

    @pl.when(i == 0)
    def _():
        uext[0:halo, :] = jnp.zeros((halo, POOL_WIDTH), _F32)

    @pl.when(i > 0)
    def _():
        uext[0:halo, :] = uext[tm:tm + halo, :]

    uext[halo:halo + tm, :] = u_ref[...]
    pos = i * tm + lax.broadcasted_iota(jnp.int32, (tm, 1), 0)
    pool = _pool_mix(lambda j, lanes: uext[halo - j:halo - j + tm, lanes], u_ref,
                     lambda size: jnp.minimum(pos + 1, size).astype(_F32), pw_ref, ps_ref)
    cat = jnp.concatenate([a_ref[j] for j in range(N_CHUNKS)] + [pool.astype(_BF16)], axis=1)
    mix = jnp.dot(cat, wo_ref[...], preferred_element_type=_F32)
    h1 = _layer_norm(alpha * x_ref[...] + mix, g_ref[...], b_ref[...])
    h1_ref[...] = h1
    hp_ref[...] = _pack_bf16_pairs(h1)
    _route(h1, wr_ref, br_ref, cnt_ref, idx_ref, gate_ref, rank_ref)
    cnt_out_ref[...] = cnt_ref[...]


def _mix_prompt(alpha, x, attn, u, w_out_bf16, pool_w_bf16, pool_scale, ln_g, ln_b, wr_t, br):
    B, S, D = x.shape
    n_total = B * S
    tm = PROJ_TILE
    n_tiles = S // tm
    tok = lambda b, i: (b * n_tiles + i, 0)
    tok_t = lambda b, i: (0, b * n_tiles + i)
    const2 = lambda b, i: (0, 0)
    return pl.pallas_call(
        functools.partial(_mix_prompt_body, alpha),
        grid=(B, n_tiles),
        in_specs=[pl.BlockSpec((None, tm, D), lambda b, i: (b, i, 0)),
                  pl.BlockSpec((None, N_CHUNKS, tm, LANES), lambda b, i: (b, 0, i, 0)),
                  pl.BlockSpec((None, tm, POOL_WIDTH), lambda b, i: (b, i, 0)),
                  pl.BlockSpec(w_out_bf16.shape, const2),
                  pl.BlockSpec(pool_w_bf16.shape, lambda b, i: (0, 0, 0)),
                  pl.BlockSpec(pool_scale.shape, const2),
                  pl.BlockSpec(ln_g.shape, const2), pl.BlockSpec(ln_b.shape, const2),
                  pl.BlockSpec(wr_t.shape, const2), pl.BlockSpec(br.shape, const2)],
        out_specs=[pl.BlockSpec((tm, D), tok), pl.BlockSpec((tm, D // 2), tok),
                   pl.BlockSpec((TOP_K, tm), tok_t), pl.BlockSpec((TOP_K, tm), tok_t),
                   pl.BlockSpec((TOP_K, tm), tok_t), pl.BlockSpec((N_EXPERTS, 1), const2)],
        out_shape=[jax.ShapeDtypeStruct((n_total, D), _F32), jax.ShapeDtypeStruct((n_total, D // 2), jnp.uint32),
                   jax.ShapeDtypeStruct((TOP_K, n_total), jnp.int32), jax.ShapeDtypeStruct((TOP_K, n_total), _F32),
                   jax.ShapeDtypeStruct((TOP_K, n_total), jnp.int32), jax.ShapeDtypeStruct((N_EXPERTS, 1), _F32)],
        scratch_shapes=[pltpu.VMEM((16 + tm, POOL_WIDTH), _F32), pltpu.VMEM((N_EXPERTS, 1), _F32)],
        compiler_params=_params(("arbitrary", "arbitrary")),
        name="mix_prompt",
    )(x, attn, u, w_out_bf16, pool_w_bf16, pool_scale, ln_g, ln_b, wr_t, br)


def _inproj_sample_body(x_ref, w_ref, o_ref):
    o_ref[...] = jnp.dot(x_ref[...].astype(_BF16), w_ref[...], preferred_element_type=_F32)


def _inproj_sample(x2d, w_in_bf16):
    return pl.pallas_call(
        _inproj_sample_body,
        out_shape=jax.ShapeDtypeStruct((x2d.shape[0], w_in_bf16.shape[1]), _F32),
        compiler_params=_params(None),
        name="inproj_sample",
    )(x2d, w_in_bf16)


def _decode_tables(T, Lb):
    slopes = _alibi_slopes()

    def mult(delta):
        m = np.zeros(delta.shape, np.float64)
        for win, dil in zip(WINDOWS, DILATIONS):
            m += ((delta % dil == 0) & (delta // dil <= win // dil) & (delta >= 0))
        return m

    t = np.arange(T)[:, None]
    d_cache = Lb + t - np.arange(Lb)[None, :]
    d_new = t - np.arange(T)[None, :]
    m_cache, m_new = mult(d_cache), mult(d_new)
    b_cache = np.where(m_cache > 0, -slopes[:, None, None] * d_cache[None], NEG_INF)
    b_new = np.where(m_new > 0, -slopes[:, None, None] * d_new[None], NEG_INF)
    f = lambda a: jnp.asarray(a.astype(np.float32))
    return f(b_cache), f(m_cache), f(b_new), f(m_new)


def _attn_sample_body(q_ref, kn_ref, vn_ref, knt_ref, vnt_ref, kc_ref, vc_ref, bc_ref, mc_ref, bn_ref, mn_ref,
                      o_ref, ko_ref, vo_ref):
    T = q_ref.shape[0]
    scale = HEAD_DIM ** -0.5
    rnd = lambda a: a.astype(_BF16).astype(_F32)
    new_col = lax.broadcasted_iota(jnp.int32, (T, T), 1)
    outs = []
    for h in range(N_HEADS):
        lanes = slice(h * HEAD_DIM, (h + 1) * HEAD_DIM)
        q = rnd(q_ref[:, lanes] * scale)
        kn = rnd(kn_ref[:, lanes])
        vn = rnd(vn_ref[:, lanes])
        s_c = jnp.dot(q.astype(_BF16), kc_ref[h].astype(_BF16), preferred_element_type=_F32) + bc_ref[h]
        s_n = bn_ref[h]
        for t in range(T):
            s_n = s_n + jnp.where(new_col == t, jnp.sum(q * kn[t:t + 1, :], axis=-1, keepdims=True), 0.0)
        m = jnp.maximum(jnp.max(s_c, axis=-1, keepdims=True), jnp.max(s_n, axis=-1, keepdims=True))
        e_c = mc_ref[...] * jnp.exp(s_c - m)
        e_n = mn_ref[...] * jnp.exp(s_n - m)
        den = jnp.sum(e_c, axis=-1, keepdims=True) + jnp.sum(e_n, axis=-1, keepdims=True)
        num = lax.dot_general(e_c.astype(_BF16), vc_ref[h].astype(_BF16), (((1,), (1,)), ((), ())),
                              preferred_element_type=_F32)
        e_nr = rnd(e_n)
        for t in range(T):
            num = num + e_nr[:, t:t + 1] * vn[t:t + 1, :]
        outs.append(num / den)
    o_ref[...] = jnp.concatenate(outs, axis=1)
    ko_ref[...] = jnp.concatenate([kc_ref[...][:, :, T:], knt_ref[...]], axis=-1)
    vo_ref[...] = jnp.concatenate([vc_ref[...][:, :, T:], vnt_ref[...]], axis=-1)


def _attn_sample(q, kn, vn, knt, vnt, kc, vc, tables):
    Bd, T, _ = q.shape
    Lb = kc.shape[-1]
    tok = pl.BlockSpec((None, T, ATTN_WIDTH), lambda b: (b, 0, 0))
    newt = pl.BlockSpec((None, N_HEADS, HEAD_DIM, T), lambda b: (b, 0, 0, 0))
    cache = pl.BlockSpec((None, N_HEADS, HEAD_DIM, Lb), lambda b: (b, 0, 0, 0))
    tabs = [pl.BlockSpec(t.shape, (lambda b, n=t.ndim: (0,) * n)) for t in tables]
    return pl.pallas_call(
        _attn_sample_body,
        grid=(Bd,),
        in_specs=[tok, tok, tok, newt, newt, cache, cache] + tabs,
        out_specs=[tok, cache, cache],
        out_shape=[jax.ShapeDtypeStruct((Bd, T, ATTN_WIDTH), _F32),
                   jax.ShapeDtypeStruct(kc.shape, _F32), jax.ShapeDtypeStruct(vc.shape, _F32)],
        compiler_params=_params(("arbitrary",)),
        name="attn_sample",
    )(q, kn, vn, knt, vnt, kc, vc, *tables)


def _mix_sample_body(alpha, T, x_ref, a_ref, ue_ref, wo_ref, pw_ref, ps_ref, g_ref, b_ref, wr_ref, br_ref, cnt_in_ref,
                     h1_ref, hp_ref, idx_ref, gate_ref, rank_ref, cnt_out_ref, cnt_ref):
    cnt_ref[...] = cnt_in_ref[...]
    hist = POOL_STATE_LEN
    pools = []
    for t in range(T):
        pools.append(_pool_mix(lambda j, lanes, t=t: ue_ref[hist + t - j, :, lanes], ue_ref[hist + t],
                               lambda size: float(size), pw_ref, ps_ref))
    pool = jnp.concatenate(pools, axis=0)
    cat = jnp.concatenate([a_ref[...].astype(_BF16), pool.astype(_BF16)], axis=1)
    mix = jnp.dot(cat, wo_ref[...], preferred_element_type=_F32)
    h1 = _layer_norm(alpha * x_ref[...] + mix, g_ref[...], b_ref[...])
    h1_ref[...] = h1
    hp_ref[...] = _pack_bf16_pairs(h1)
    _route(h1, wr_ref, br_ref, cnt_ref, idx_ref, gate_ref, rank_ref)
    cnt_out_ref[...] = cnt_ref[...]


def _mix_sample(alpha, T, x_tm, attn_tm, uext_tm, w_out_bf16, pool_w_bf16, pool_scale, ln_g, ln_b, wr_t, br, cnt_in):
    n_s, D = x_tm.shape
    full = lambda shape: pl.BlockSpec(shape, lambda i, n=len(shape): (0,) * n)
    ins = [x_tm, attn_tm, uext_tm, w_out_bf16, pool_w_bf16, pool_scale, ln_g, ln_b, wr_t, br, cnt_in]
    outs = [jax.ShapeDtypeStruct((n_s, D), _F32), jax.ShapeDtypeStruct((n_s, D // 2), jnp.uint32),
            jax.ShapeDtypeStruct((TOP_K, n_s), jnp.int32), jax.ShapeDtypeStruct((TOP_K, n_s), _F32),
            jax.ShapeDtypeStruct((TOP_K, n_s), jnp.int32), jax.ShapeDtypeStruct((N_EXPERTS, 1), _F32)]
    return pl.pallas_call(
        functools.partial(_mix_sample_body, alpha, T),
        grid=(1,),
        in_specs=[full(a.shape) for a in ins],
        out_specs=[full(o.shape) for o in outs],
        out_shape=outs,
        scratch_shapes=[pltpu.VMEM((N_EXPERTS, 1), _F32)],
        compiler_params=_params(("arbitrary",)),
        name="mix_sample",
    )(*ins)


def _dispatch_body(n_p_tiles, dest_hbm, hp_p_ref, hp_s_ref, xs_in, xs_hbm, tile, dest_smem, sem_idx, sem_rows):
    del xs_in
    i = pl.program_id(0)
    n_steps = pl.num_programs(0)
    tm = tile.shape[1]

    def idx_copy(t, s):
        return pltpu.make_async_copy(dest_hbm.at[t], dest_smem.at[pl.ds(s * TOP_K * tm, TOP_K * tm)], sem_idx.at[s])

    def wait_rows(s):
        for _ in range(TOP_K):
            pltpu.make_async_copy(tile.at[s], xs_hbm.at[pl.ds(0, tm)], sem_rows.at[s]).wait()

    @pl.when(i == 0)
    def _():
        idx_copy(i, 0).start()

    staged = jnp.where(i < n_p_tiles, hp_p_ref[...], hp_s_ref[...])

    for s in range(2):
        @pl.when(i % 2 == s)
        def _(s=s):
            @pl.when(i + 1 < n_steps)
            def _():
                idx_copy(i + 1, 1 - s).start()

            tile[s] = staged
            idx_copy(i, s).wait()

            def issue(n, carry):
                for kk in range(TOP_K):
                    pltpu.make_async_copy(tile.at[s, pl.ds(n, 1)], xs_hbm.at[pl.ds(dest_smem[(s * TOP_K + kk) * tm + n], 1)],
                                          sem_rows.at[s]).start()
                return carry

            lax.fori_loop(0, tm, issue, 0, unroll=2)

            @pl.when(i > 0)
            def _():
                wait_rows(1 - s)

            @pl.when(i == n_steps - 1)
            def _():
                wait_rows(s)


def _dispatch(dest_tiles, hp_p, hp_s, xs_zero):
    w = hp_p.shape[1]
    tm = ROW_TILE
    n_p_tiles, n_s_tiles = hp_p.shape[0] // tm, hp_s.shape[0] // tm
    return pl.pallas_call(
        functools.partial(_dispatch_body, n_p_tiles),
        grid=(n_p_tiles + n_s_tiles,),
        in_specs=[pl.BlockSpec(memory_space=pl.ANY),
                  pl.BlockSpec((tm, w), lambda i: (jnp.minimum(i, n_p_tiles - 1), 0)),
                  pl.BlockSpec((tm, w), lambda i: (jnp.maximum(i - n_p_tiles, 0), 0)),
                  pl.BlockSpec(memory_space=pl.ANY)],
        out_specs=pl.BlockSpec(memory_space=pl.ANY),
        out_shape=jax.ShapeDtypeStruct(xs_zero.shape, xs_zero.dtype),
        input_output_aliases={3: 0},
        scratch_shapes=[pltpu.VMEM((2, tm, w), jnp.uint32), pltpu.SMEM((2 * TOP_K * tm,), jnp.int32),
                        pltpu.SemaphoreType.DMA((2,)), pltpu.SemaphoreType.DMA((2,))],
        compiler_params=_params(("arbitrary",)),
        name="moe_dispatch",
    )(dest_tiles, hp_p, hp_s, xs_zero)


def _experts_body(be_ref, na_ref, xs_ref, wu_ref, bu_ref, wd_ref, bd_ref, ys_ref, wu_bf, wd_bf):
    j = pl.program_id(0)
    d_ff = wd_ref.shape[0]
    changed = jnp.logical_or(j == 0, be_ref[j] != be_ref[jnp.maximum(j - 1, 0)])

    @pl.when(jnp.logical_and(j < na_ref[0], changed))
    def _():
        wu_bf[...] = wu_ref[...].astype(_BF16)
        wd_bf[...] = wd_ref[...].astype(_BF16)

    @pl.when(j < na_ref[0])
    def _():
        x = _unpack_bf16_pairs(xs_ref[...])
        h = jnp.dot(x, wu_bf[...], preferred_element_type=_F32) + bu_ref[...]
        g = jnp.minimum(h[:, :d_ff], SWIGLU_LIMIT)
        lin = jnp.clip(h[:, d_ff:], -SWIGLU_LIMIT, SWIGLU_LIMIT)
        act = g * jax.nn.sigmoid(SWIGLU_ALPHA * g) * (lin + 1.0)
        ys_ref[...] = jnp.dot(act.astype(_BF16), wd_bf[...], preferred_element_type=_F32) + bd_ref[...]

    @pl.when(j >= na_ref[0])
    def _():
        ys_ref[...] = jnp.zeros_like(ys_ref)


def _experts(block_e, n_active, xs, w_up, b_up, w_down, b_down):
    n_rows, w = xs.shape
    blk = MOE_BLOCK
    n_blocks = n_rows // blk
    d_model, d_ff2 = w_up.shape[1:]
    d_ff = w_down.shape[1]
    act_blk = lambda j, be, na: jnp.minimum(j, jnp.maximum(na[0] - 1, 0))
    grid_spec = pltpu.PrefetchScalarGridSpec(
        num_scalar_prefetch=2,
        grid=(n_blocks,),
        in_specs=[pl.BlockSpec((blk, w), lambda j, be, na: (act_blk(j, be, na), 0)),
                  pl.BlockSpec((None, d_model, d_ff2), lambda j, be, na: (be[j], 0, 0)),
                  pl.BlockSpec((None, 1, d_ff2), lambda j, be, na: (be[j], 0, 0)),
                  pl.BlockSpec((None, d_ff, d_model), lambda j, be, na: (be[j], 0, 0)),
                  pl.BlockSpec((None, 1, d_model), lambda j, be, na: (be[j], 0, 0))],
        out_specs=pl.BlockSpec((blk, d_model), lambda j, be, na: (j, 0)),
        scratch_shapes=[pltpu.VMEM((d_model, d_ff2), _BF16), pltpu.VMEM((d_ff, d_model), _BF16)],
    )
    return pl.pallas_call(
        _experts_body,
        grid_spec=grid_spec,
        out_shape=jax.ShapeDtypeStruct((n_rows, d_model), _F32),
        compiler_params=_params(("arbitrary",)),
        name="moe_experts",
    )(block_e, n_active, xs, w_up, b_up, w_down, b_down)


def _combine_body(alpha, n_p_tiles, dest_hbm, ys_hbm, h1_p_ref, h1_s_ref, gate_ref, g_ref, b_ref, o_p_ref, o_s_ref,
                  dest_smem, buf, sem_idx, sem_rows):
    i = pl.program_id(0)
    n_steps = pl.num_programs(0)
    tm = o_p_ref.shape[0]

    def idx_copy(t, s):
        return pltpu.make_async_copy(dest_hbm.at[t], dest_smem.at[pl.ds(s * TOP_K * tm, TOP_K * tm)], sem_idx.at[s])

    def issue(s):
        def body(n, carry):
            for kk in range(TOP_K):
                pltpu.make_async_copy(ys_hbm.at[pl.ds(dest_smem[(s * TOP_K + kk) * tm + n], 1)], buf.at[s, kk, pl.ds(n, 1)],
                                      sem_rows.at[s]).start()
            return carry
        lax.fori_loop(0, tm, body, 0, unroll=2)

    def wait_rows(s):
        for kk in range(TOP_K):
            pltpu.make_async_copy(ys_hbm.at[pl.ds(0, tm)], buf.at[s, kk], sem_rows.at[s]).wait()

    @pl.when(i == 0)
    def _():
        idx_copy(i, 0).start()
        idx_copy(i, 0).wait()
        issue(0)

        @pl.when(n_steps > 1)
        def _():
            idx_copy(i + 1, 1).start()

    for s in range(2):
        @pl.when(i % 2 == s)
        def _(s=s):
            @pl.when(i + 1 < n_steps)
            def _():
                idx_copy(i + 1, 1 - s).wait()
                issue(1 - s)

            @pl.when(i + 2 < n_steps)
            def _():
                idx_copy(i + 2, s).start()

            wait_rows(s)
            y = gate_ref[:, 0:1] * buf[s, 0]
            for kk in range(1, TOP_K):
                y = y + gate_ref[:, kk:kk + 1] * buf[s, kk]

            @pl.when(i < n_p_tiles)
            def _():
                o_p_ref[...] = _layer_norm(alpha * h1_p_ref[...] + y, g_ref[...], b_ref[...])

            @pl.when(i >= n_p_tiles)
            def _():
                o_s_ref[...] = _layer_norm(alpha * h1_s_ref[...] + y, g_ref[...], b_ref[...])


def _combine(alpha, dest_tiles, ys, h1_p, h1_s, gate_rows, ln_g, ln_b):
    d_model = h1_p.shape[1]
    tm = ROW_TILE
    n_p_tiles, n_s_tiles = h1_p.shape[0] // tm, h1_s.shape[0] // tm
    const2 = lambda i: (0, 0)
    p_tile = lambda i: (jnp.minimum(i, n_p_tiles - 1), 0)
    s_tile = lambda i: (jnp.maximum(i - n_p_tiles, 0), 0)
    return pl.pallas_call(
        functools.partial(_combine_body, alpha, n_p_tiles),
        grid=(n_p_tiles + n_s_tiles,),
        in_specs=[pl.BlockSpec(memory_space=pl.ANY), pl.BlockSpec(memory_space=pl.ANY),
                  pl.BlockSpec((tm, d_model), p_tile), pl.BlockSpec((tm, d_model), s_tile),
                  pl.BlockSpec((tm, TOP_K), lambda i: (i, 0)),
                  pl.BlockSpec(ln_g.shape, const2), pl.BlockSpec(ln_b.shape, const2)],
        out_specs=[pl.BlockSpec((tm, d_model), p_tile), pl.BlockSpec((tm, d_model), s_tile)],
        out_shape=[jax.ShapeDtypeStruct(h1_p.shape, _F32), jax.ShapeDtypeStruct(h1_s.shape, _F32)],
        scratch_shapes=[pltpu.SMEM((2 * TOP_K * tm,), jnp.int32), pltpu.VMEM((2, TOP_K, tm, d_model), _F32),
                        pltpu.SemaphoreType.DMA((2,)), pltpu.SemaphoreType.DMA((2,))],
        compiler_params=_params(("arbitrary",)),
        name="moe_combine",
    )(dest_tiles, ys, h1_p, h1_s, gate_rows, ln_g, ln_b)


def _moe_layout(idx_all, rank_all, counts, n_total):
    blk = MOE_BLOCK
    n_blocks = (TOP_K * n_total + N_EXPERTS * (blk - 1)) // blk + 1
    cnt = counts.reshape(N_EXPERTS).astype(jnp.int32)
    padded = (cnt + blk - 1) // blk * blk
    pend = jnp.cumsum(padded)
    pstart = pend - padded
    experts = jnp.arange(N_EXPERTS, dtype=jnp.int32)
    dest = rank_all + jnp.sum(jnp.where(idx_all[None] == experts[:, None, None], pstart[:, None, None], 0), axis=0)
    block_row0 = jnp.arange(n_blocks, dtype=jnp.int32) * blk
    block_e = jnp.minimum(jnp.sum((pend[None, :] <= block_row0[:, None]).astype(jnp.int32), axis=1), N_EXPERTS - 1)
    n_active = (pend[-1] // blk).astype(jnp.int32).reshape(1)
    last_e = block_e[jnp.maximum(n_active[0] - 1, 0)]
    block_e = jnp.where(jnp.arange(n_blocks) < n_active[0], block_e, last_e)
    tm = ROW_TILE
    dest_tiles = dest.reshape(TOP_K, n_total // tm, tm).transpose(1, 0, 2).reshape(n_total // tm, TOP_K * tm)
    return dest_tiles, block_e, n_active, n_blocks * blk


def kernel(x_prompt, x_sample, cache_attn_k, cache_attn_v, state_pool, w_in, w_out, pool_w, pool_scale, ln1_g, ln1_b,
           w_router, b_router, w_up, b_up, w_down, b_down, ln2_g, ln2_b):
    depth = w_in.shape[0]
    assert depth == 1, "single-layer step"
    B, S, D = x_prompt.shape
    Bd, T, _ = x_sample.shape
    Lb = cache_attn_k.shape[2]
    assert S % ATTN_TILE == 0 and Lb == WINDOW_MAX and (B * S) % ROW_TILE == 0 and (Bd * T) % ROW_TILE == 0
    alpha = (2 * depth) ** 0.25
    n_p, n_s = B * S, Bd * T
    n_total = n_p + n_s

    w_in_b = w_in[0].astype(_BF16)
    w_out_b = w_out[0].astype(_BF16)
    pool_w_b = pool_w[0].astype(_BF16)
    wr_t = w_router[0].T
    br = b_router[0].reshape(N_EXPERTS, 1)

    q, k, v, u, kt, vt = _inproj_prompt(x_prompt, w_in_b)
    attn = _attn_prompt(q, k, v, _attn_bias_tables())
    h1_p, hp_p, idx_p, gate_p, rank_p, cnt_p = _mix_prompt(
        alpha, x_prompt, attn, u, w_out_b, pool_w_b, pool_scale, ln1_g, ln1_b, wr_t, br)
    n_keep = kt.shape[-1]
    k_prompt = kt.reshape(1, B, N_HEADS, HEAD_DIM, n_keep).transpose(0, 1, 4, 2, 3)
    v_prompt = vt.reshape(1, B, N_HEADS, HEAD_DIM, n_keep).transpose(0, 1, 4, 2, 3)
    pool_prompt = u[:, S - POOL_STATE_LEN:][None]

    hs = _inproj_sample(x_sample.reshape(n_s, D), w_in_b)
    qs = hs[:, :ATTN_WIDTH].reshape(Bd, T, ATTN_WIDTH)
    ks = hs[:, ATTN_WIDTH:2 * ATTN_WIDTH].reshape(Bd, T, ATTN_WIDTH)
    vs = hs[:, 2 * ATTN_WIDTH:3 * ATTN_WIDTH].reshape(Bd, T, ATTN_WIDTH)
    us = hs[:, 3 * ATTN_WIDTH:].reshape(Bd, T, POOL_WIDTH)
    to_t = lambda a: a.reshape(Bd, T, N_HEADS, HEAD_DIM).transpose(0, 2, 3, 1)
    kc = cache_attn_k[0].transpose(0, 2, 3, 1)
    vc = cache_attn_v[0].transpose(0, 2, 3, 1)
    attn_s, k_new, v_new = _attn_sample(qs, ks, vs, to_t(ks), to_t(vs), kc, vc, _decode_tables(T, Lb))
    k_sample = k_new.transpose(0, 3, 1, 2)[None]
    v_sample = v_new.transpose(0, 3, 1, 2)[None]
    uext_tm = jnp.concatenate([state_pool[0].transpose(1, 0, 2), us.transpose(1, 0, 2)], axis=0)
    pool_sample = uext_tm[T:].transpose(1, 0, 2)[None]
    x_tm = x_sample.transpose(1, 0, 2).reshape(n_s, D)
    attn_tm = attn_s.transpose(1, 0, 2).reshape(n_s, ATTN_WIDTH)
    h1_s, hp_s, idx_s, gate_s, rank_s, counts = _mix_sample(
        alpha, T, x_tm, attn_tm, uext_tm, w_out_b, pool_w_b, pool_scale, ln1_g, ln1_b, wr_t, br, cnt_p)

    idx_all = jnp.concatenate([idx_p, idx_s], axis=1)
    rank_all = jnp.concatenate([rank_p, rank_s], axis=1)
    gate_rows = jnp.concatenate([gate_p, gate_s], axis=1).T
    dest_tiles, block_e, n_active, n_rows = _moe_layout(idx_all, rank_all, counts, n_total)
    xs = _dispatch(dest_tiles, hp_p, hp_s, jnp.zeros((n_rows, D // 2), jnp.uint32))
    ys = _experts(block_e, n_active, xs, w_up[0], b_up[0][:, None, :], w_down[0], b_down[0][:, None, :])
    out_p, out_s = _combine(alpha, dest_tiles, ys, h1_p, h1_s, gate_rows, ln2_g, ln2_b)

    y_prompt = out_p.reshape(B, S, D)
    y_sample = out_s.reshape(T, Bd, D).transpose(1, 0, 2)
    return (y_prompt, y_sample, k_prompt, v_prompt, pool_prompt, k_sample, v_sample, pool_sample)
```

```python
import functools
import math

import jax
import jax.numpy as jnp
import numpy as np
from jax import lax
from jax.experimental import pallas as pl
from jax.experimental.pallas import tpu as pltpu

HEAD_DIM = 64
N_HEADS = 8
ATTN_WIDTH = N_HEADS * HEAD_DIM
DILATIONS = (1, 4, 16)
WINDOWS = (128, 512, 2048)
N_BACK = 128
WINDOW_MAX = 2048
ATTN_BLOCK = 128
POOL_SIZES = (2, 4, 8, 16)
POOL_GROUP_DIM = 128
POOL_WIDTH = 512
POOL_STATE_LEN = 15
N_EXPERTS = 32
TOP_K = 4
SWIGLU_ALPHA = 1.702
SWIGLU_LIMIT = 7.0
LN_EPS = 1e-5
NEG_INF = -1e30

LANES = 128
HEADS_PER_CHUNK = LANES // HEAD_DIM
N_CHUNKS = ATTN_WIDTH // LANES
VMEM_LIMIT = 56 * 1024 * 1024

ATTN_TILE = 2048
BLOCK_UNROLL = 16
PROJ_TILE = 512
MOE_BLOCK = 512
ROW_TILE = 128

_F32 = jnp.float32
_BF16 = jnp.bfloat16


def _alibi_slopes():
    return 2.0 ** (-8.0 * np.arange(1, N_HEADS + 1, dtype=np.float64) / N_HEADS)


def _params(sem, **kw):
    return pltpu.CompilerParams(dimension_semantics=sem, vmem_limit_bytes=VMEM_LIMIT, **kw)


def _inproj_body(n_keep_tiles, x_ref, w_ref, q_ref, k_ref, v_ref, u_ref, kt_ref, vt_ref):
    i = pl.program_id(1)
    n_tiles = pl.num_programs(1)
    h = jnp.dot(x_ref[...].astype(_BF16), w_ref[...], preferred_element_type=_F32)
    for j in range(N_CHUNKS):
        q_ref[j] = h[:, j * LANES:(j + 1) * LANES]
        k_ref[j] = h[:, ATTN_WIDTH + j * LANES:ATTN_WIDTH + (j + 1) * LANES]
        v_ref[j] = h[:, 2 * ATTN_WIDTH + j * LANES:2 * ATTN_WIDTH + (j + 1) * LANES]
    u_ref[...] = h[:, 3 * ATTN_WIDTH:]

    @pl.when(i >= n_tiles - n_keep_tiles)
    def _():
        kt_ref[...] = h[:, ATTN_WIDTH:2 * ATTN_WIDTH].T
        vt_ref[...] = h[:, 2 * ATTN_WIDTH:3 * ATTN_WIDTH].T


def _inproj_prompt(x, w_in_bf16):
    B, S, D = x.shape
    tm = PROJ_TILE
    n_tiles = S // tm
    n_keep = min(WINDOW_MAX, S)
    n_keep_tiles = n_keep // tm
    first_keep = n_tiles - n_keep_tiles
    chunked = jax.ShapeDtypeStruct((B, N_CHUNKS, S, LANES), _F32)
    chunk_spec = pl.BlockSpec((None, N_CHUNKS, tm, LANES), lambda b, i: (b, 0, i, 0))
    t_spec = pl.BlockSpec((None, ATTN_WIDTH, tm), lambda b, i: (b, 0, jnp.maximum(i - first_keep, 0)))
    return pl.pallas_call(
        functools.partial(_inproj_body, n_keep_tiles),
        grid=(B, n_tiles),
        in_specs=[pl.BlockSpec((None, tm, D), lambda b, i: (b, i, 0)),
                  pl.BlockSpec(w_in_bf16.shape, lambda b, i: (0, 0))],
        out_specs=[chunk_spec, chunk_spec, chunk_spec,
                   pl.BlockSpec((None, tm, POOL_WIDTH), lambda b, i: (b, i, 0)),
                   t_spec, t_spec],
        out_shape=[chunked, chunked, chunked,
                   jax.ShapeDtypeStruct((B, S, POOL_WIDTH), _F32),
                   jax.ShapeDtypeStruct((B, ATTN_WIDTH, n_keep), _F32),
                   jax.ShapeDtypeStruct((B, ATTN_WIDTH, n_keep), _F32)],
        compiler_params=_params(("arbitrary", "arbitrary")),
        name="inproj_prompt",
    )(x, w_in_bf16)


def _attn_bias_tables():
    qi = np.arange(ATTN_BLOCK)[:, None]
    kj = np.arange(2 * ATTN_BLOCK)[None, :]
    step = qi - kj + ATTN_BLOCK
    valid = (step >= 0) & (step <= N_BACK)
    slopes = _alibi_slopes()
    out = np.zeros((N_CHUNKS, len(DILATIONS), HEADS_PER_CHUNK * ATTN_BLOCK, 2 * ATTN_BLOCK), np.float32)
    for c in range(N_CHUNKS):
        for p, dil in enumerate(DILATIONS):
            for hh in range(HEADS_PER_CHUNK):
                bias = -slopes[c * HEADS_PER_CHUNK + hh] * (step * dil).astype(np.float64)
                out[c, p, hh * ATTN_BLOCK:(hh + 1) * ATTN_BLOCK] = np.where(valid, bias, NEG_INF)
    return jnp.asarray(out)


def _attn_body(q_ref, k_ref, v_ref, bias_ref, o_ref,
               qf1, kf1, vf1, qf4, kf4, vf4, qf16, kf16, vf16, acc_n, acc_m, acc_d):
    i = pl.program_id(2)
    T = ATTN_TILE
    folded = ((1, qf1, kf1, vf1), (4, qf4, kf4, vf4), (16, qf16, kf16, vf16))
    scale = HEAD_DIM ** -0.5

    for dil, _, kf, vf in folded:
        L = T // dil

        @pl.when(i == 0)
        def _():
            kf[:, 0:ATTN_BLOCK, :] = jnp.zeros((dil, ATTN_BLOCK, LANES), _BF16)
            vf[:, 0:ATTN_BLOCK, :] = jnp.zeros((dil, ATTN_BLOCK, LANES), _BF16)

        @pl.when(i > 0)
        def _():
            kf[:, 0:ATTN_BLOCK, :] = kf[:, L:L + ATTN_BLOCK, :]
            vf[:, 0:ATTN_BLOCK, :] = vf[:, L:L + ATTN_BLOCK, :]

    for dil, qf, kf, vf in folded:
        L = T // dil
        for r in range(dil):
            rows = pl.ds(r, L, stride=dil) if dil > 1 else pl.ds(0, L)
            qf[r] = (q_ref[rows, :] * scale).astype(_BF16)
            kf[r, ATTN_BLOCK:ATTN_BLOCK + L, :] = k_ref[rows, :].astype(_BF16)
            vf[r, ATTN_BLOCK:ATTN_BLOCK + L, :] = v_ref[rows, :].astype(_BF16)

    lane = lax.broadcasted_iota(jnp.int32, (ATTN_BLOCK, LANES), 1)
    first_head = lane < HEAD_DIM
    col = lax.broadcasted_iota(jnp.int32, (HEADS_PER_CHUNK * ATTN_BLOCK, 2 * ATTN_BLOCK), 1)
    prev_half = col < ATTN_BLOCK

    for p, (dil, qf, kf, vf) in enumerate(folded):
        blocks_per_res = T // dil // ATTN_BLOCK

        def block(blk, carry, p=p, dil=dil, qf=qf, kf=kf, vf=vf, blocks_per_res=blocks_per_res):
            r = blk // blocks_per_res
            c = blk % blocks_per_res
            row0 = pl.multiple_of(c * ATTN_BLOCK, ATTN_BLOCK)
            q = qf[r, pl.ds(row0, ATTN_BLOCK), :]
            kk = kf[r, pl.ds(row0, 2 * ATTN_BLOCK), :]
            vv = vf[r, pl.ds(row0, 2 * ATTN_BLOCK), :]
            zero = jnp.zeros_like(q)
            qm = jnp.concatenate([jnp.where(first_head, q, zero), jnp.where(first_head, zero, q)], axis=0)
            s = lax.dot_general(qm, kk, (((1,), (1,)), ((), ())), preferred_element_type=_F32)
            s = s + bias_ref[p]
            s = jnp.where(jnp.logical_and(prev_half, jnp.logical_and(i == 0, c == 0)), NEG_INF, s)
            m = jnp.max(s, axis=-1, keepdims=True)
            e = jnp.exp(s - m)
            den = jnp.sum(e, axis=-1, keepdims=True)
            pv = jnp.dot(e.astype(_BF16), vv, preferred_element_type=_F32)
            num = jnp.where(first_head, pv[:ATTN_BLOCK], pv[ATTN_BLOCK:])
            m2 = jnp.where(first_head, m[:ATTN_BLOCK], m[ATTN_BLOCK:])
            d2 = jnp.where(first_head, den[:ATTN_BLOCK], den[ATTN_BLOCK:])
            start = r + row0 * dil
            rows = pl.ds(start, ATTN_BLOCK, stride=dil) if dil > 1 else pl.ds(start, ATTN_BLOCK)
            acc_n[p, rows, :] = num
            acc_m[p, rows, :] = m2
            acc_d[p, rows, :] = d2
            return carry

        lax.fori_loop(0, T // ATTN_BLOCK, block, 0, unroll=BLOCK_UNROLL)

    chunk_rows = 256

    def merge(t, carry):
        rows = pl.ds(pl.multiple_of(t * chunk_rows, chunk_rows), chunk_rows)
        m0, m1, m2 = acc_m[0, rows, :], acc_m[1, rows, :], acc_m[2, rows, :]
        mx = jnp.maximum(jnp.maximum(m0, m1), m2)
        e0, e1, e2 = jnp.exp(m0 - mx), jnp.exp(m1 - mx), jnp.exp(m2 - mx)
        num = acc_n[0, rows, :] * e0 + acc_n[1, rows, :] * e1 + acc_n[2, rows, :] * e2
        den = acc_d[0, rows, :] * e0 + acc_d[1, rows, :] * e1 + acc_d[2, rows, :] * e2
        o_ref[rows, :] = (num / den).astype(o_ref.dtype)
        return carry

    lax.fori_loop(0, T // chunk_rows, merge, 0)


def _attn_prompt(q, k, v, bias):
    B, _, S, _ = q.shape
    T = ATTN_TILE
    io_spec = pl.BlockSpec((None, None, T, LANES), lambda b, c, i: (b, c, i, 0))
    scratch = []
    for dil in DILATIONS:
        L = T // dil
        scratch += [pltpu.VMEM((dil, L, LANES), _BF16),
                    pltpu.VMEM((dil, ATTN_BLOCK + L, LANES), _BF16),
                    pltpu.VMEM((dil, ATTN_BLOCK + L, LANES), _BF16)]
    scratch += [pltpu.VMEM((len(DILATIONS), T, LANES), _F32)] * 3
    return pl.pallas_call(
        _attn_body,
        grid=(B, N_CHUNKS, S // T),
        in_specs=[io_spec, io_spec, io_spec,
                  pl.BlockSpec((None,) + bias.shape[1:], lambda b, c, i: (c, 0, 0, 0))],
        out_specs=io_spec,
        out_shape=jax.ShapeDtypeStruct((B, N_CHUNKS, S, LANES), _BF16),
        scratch_shapes=scratch,
        compiler_params=_params(("arbitrary", "arbitrary", "arbitrary")),
        name="attn_prompt",
    )(q, k, v, bias)


def _layer_norm(z, g, b):
    mu = jnp.mean(z, axis=-1, keepdims=True)
    zc = z - mu
    var = jnp.mean(zc * zc, axis=-1, keepdims=True)
    return zc * lax.rsqrt(var + LN_EPS) * g + b


def _pack_bf16_pairs(h):
    w = h.shape[1] // 2
    bits = pltpu.bitcast(h.astype(_BF16).astype(_F32), jnp.uint32)
    return (bits[:, w:] & jnp.uint32(0xFFFF0000)) | (bits[:, :w] >> 16)


def _unpack_bf16_pairs(words):
    lo = pltpu.bitcast(words << 16, _F32)
    hi = pltpu.bitcast(words & jnp.uint32(0xFFFF0000), _F32)
    return jnp.concatenate([lo, hi], axis=1).astype(_BF16)


def _route(h1, wr_ref, br_ref, cnt_ref, idx_ref, gate_ref, rank_ref):
    tm = h1.shape[0]
    logits = lax.dot_general(wr_ref[...], h1, (((1,), (1,)), ((), ())),
                             precision=lax.Precision.HIGHEST, preferred_element_type=_F32) + br_ref[...]
    eid = lax.broadcasted_iota(jnp.int32, (N_EXPERTS, tm), 0)
    work = logits
    vals, ids, hots = [], [], []
    for _ in range(TOP_K):
        mx = jnp.max(work, axis=0, keepdims=True)
        sel = jnp.min(jnp.where(work == mx, eid, N_EXPERTS), axis=0, keepdims=True)
        hot = eid == sel
        vals.append(mx)
        ids.append(sel)
        hots.append(hot)
        work = jnp.where(hot, -jnp.inf, work)
    ex = [jnp.exp(v - vals[0]) for v in vals]
    tot = ex[0] + ex[1] + ex[2] + ex[3]
    any_hot = jnp.where(hots[0] | hots[1] | hots[2] | hots[3], 1.0, 0.0)
    earlier = (lax.broadcasted_iota(jnp.int32, (tm, tm), 0) < lax.broadcasted_iota(jnp.int32, (tm, tm), 1))
    prefix = jnp.dot(any_hot.astype(_BF16), jnp.where(earlier, 1.0, 0.0).astype(_BF16),
                     preferred_element_type=_F32) + cnt_ref[...]
    for kk in range(TOP_K):
        idx_ref[kk:kk + 1, :] = ids[kk]
        gate_ref[kk:kk + 1, :] = ex[kk] / tot
        rank_ref[kk:kk + 1, :] = jnp.sum(jnp.where(hots[kk], prefix, 0.0), axis=0, keepdims=True).astype(jnp.int32)
    cnt_ref[...] = cnt_ref[...] + jnp.sum(any_hot, axis=1, keepdims=True)


def _pool_mix(win_fn, u_tile, cnt_fn, pw_ref, ps_ref):
    outs = []
    for g, size in enumerate(POOL_SIZES):
        lanes = slice(g * POOL_GROUP_DIM, (g + 1) * POOL_GROUP_DIM)
        win = win_fn(0, lanes)
        for j in range(1, size):
            win = win + win_fn(j, lanes)
        d = win / cnt_fn(size) - u_tile[:, lanes]
        y = jnp.dot(d.astype(_BF16), pw_ref[g], preferred_element_type=_F32)
        outs.append(y * ps_ref[:, lanes])
    return jnp.concatenate(outs, axis=1)


def _mix_prompt_body(alpha, x_ref, a_ref, u_ref, wo_ref, pw_ref, ps_ref, g_ref, b_ref, wr_ref, br_ref,
                     h1_ref, hp_ref, idx_ref, gate_ref, rank_ref, cnt_out_ref, uext, cnt_ref):
    b = pl.program_id(0)
    i = pl.program_id(1)
    tm = x_ref.shape[0]
    halo = 16

    @pl.when(jnp.logical_and(b == 0, i == 0))
    def _():
        cnt_ref[...] = jnp.zeros_like(cnt_ref)

    @pl.when(i == 0)
    def _():
        uext[0:halo, :] = jnp.zeros((halo, POOL_WIDTH), _F32)

    @pl.when(i > 0)
    def _():
        uext[0:halo, :] = uext[tm:tm + halo, :]

    uext[halo:halo + tm, :] = u_ref[...]
    pos = i * tm + lax.broadcasted_iota(jnp.int32, (tm, 1), 0)
    pool = _pool_mix(lambda j, lanes: uext[halo - j:halo - j + tm, lanes], u_ref,
                     lambda size: jnp.minimum(pos + 1, size).astype(_F32), pw_ref, ps_ref)
    cat = jnp.concatenate([a_ref[j] for j in range(N_CHUNKS)] + [pool.astype(_BF16)], axis=1)
    mix = jnp.dot(cat, wo_ref[...], preferred_element_type=_F32)
    h1 = _layer_norm(alpha * x_ref[...] + mix, g_ref[...], b_ref[...])
    h1_ref[...] = h1
    hp_ref[...] = _pack_bf16_pairs(h1)
    _route(h1, wr_ref, br_ref, cnt_ref, idx_ref, gate_ref, rank_ref)
    cnt_out_ref[...] = cnt_ref[...]


def _mix_prompt(alpha, x, attn, u, w_out_bf16, pool_w_bf16, pool_scale, ln_g, ln_b, wr_t, br):
    B, S, D = x.shape
    n_total = B * S
    tm = PROJ_TILE
    n_tiles = S // tm
    tok = lambda b, i: (b * n_tiles + i, 0)
    tok_t = lambda b, i: (0, b * n_tiles + i)
    const2 = lambda b, i: (0, 0)
    return pl.pallas_call(
        functools.partial(_mix_prompt_body, alpha),
        grid=(B, n_tiles),
        in_specs=[pl.BlockSpec((None, tm, D), lambda b, i: (b, i, 0)),
                  pl.BlockSpec((None, N_CHUNKS, tm, LANES), lambda b, i: (b, 0, i, 0)),
                  pl.BlockSpec((None, tm, POOL_WIDTH), lambda b, i: (b, i, 0)),
                  pl.BlockSpec(w_out_bf16.shape, const2),
                  pl.BlockSpec(pool_w_bf16.shape, lambda b, i: (0, 0, 0)),
                  pl.BlockSpec(pool_scale.shape, const2),
                  pl.BlockSpec(ln_g.shape, const2), pl.BlockSpec(ln_b.shape, const2),
                  pl.BlockSpec(wr_t.shape, const2), pl.BlockSpec(br.shape, const2)],
        out_specs=[pl.BlockSpec((tm, D), tok), pl.BlockSpec((tm, D // 2), tok),
                   pl.BlockSpec((TOP_K, tm), tok_t), pl.BlockSpec((TOP_K, tm), tok_t),
                   pl.BlockSpec((TOP_K, tm), tok_t), pl.BlockSpec((N_EXPERTS, 1), const2)],
        out_shape=[jax.ShapeDtypeStruct((n_total, D), _F32), jax.ShapeDtypeStruct((n_total, D // 2), jnp.uint32),
                   jax.ShapeDtypeStruct((TOP_K, n_total), jnp.int32), jax.ShapeDtypeStruct((TOP_K, n_total), _F32),
                   jax.ShapeDtypeStruct((TOP_K, n_total), jnp.int32), jax.ShapeDtypeStruct((N_EXPERTS, 1), _F32)],
        scratch_shapes=[pltpu.VMEM((16 + tm, POOL_WIDTH), _F32), pltpu.VMEM((N_EXPERTS, 1), _F32)],
        compiler_params=_params(("arbitrary", "arbitrary")),
        name="mix_prompt",
    )(x, attn, u, w_out_bf16, pool_w_bf16, pool_scale, ln_g, ln_b, wr_t, br)


def _inproj_sample_body(x_ref, w_ref, o_ref):
    o_ref[...] = jnp.dot(x_ref[...].astype(_BF16), w_ref[...], preferred_element_type=_F32)


def _inproj_sample(x2d, w_in_bf16):
    return pl.pallas_call(
        _inproj_sample_body,
        out_shape=jax.ShapeDtypeStruct((x2d.shape[0], w_in_bf16.shape[1]), _F32),
        compiler_params=_params(None),
        name="inproj_sample",
    )(x2d, w_in_bf16)


def _decode_tables(T, Lb):
    slopes = _alibi_slopes()

    def mult(delta):
        m = np.zeros(delta.shape, np.float64)
        for win, dil in zip(WINDOWS, DILATIONS):
            m += ((delta % dil == 0) & (delta // dil <= win // dil) & (delta >= 0))
        return m

    t = np.arange(T)[:, None]
    d_cache = Lb + t - np.arange(Lb)[None, :]
    d_new = t - np.arange(T)[None, :]
    m_cache, m_new = mult(d_cache), mult(d_new)
    b_cache = np.where(m_cache > 0, -slopes[:, None, None] * d_cache[None], NEG_INF)
    b_new = np.where(m_new > 0, -slopes[:, None, None] * d_new[None], NEG_INF)
    f = lambda a: jnp.asarray(a.astype(np.float32))
    return f(b_cache), f(m_cache), f(b_new), f(m_new)


def _attn_sample_body(q_ref, kn_ref, vn_ref, knt_ref, vnt_ref, kc_ref, vc_ref, bc_ref, mc_ref, bn_ref, mn_ref,
                      o_ref, ko_ref, vo_ref):
    T = q_ref.shape[0]
    scale = HEAD_DIM ** -0.5
    rnd = lambda a: a.astype(_BF16).astype(_F32)
    new_col = lax.broadcasted_iota(jnp.int32, (T, T), 1)
    outs = []
    for h in range(N_HEADS):
        lanes = slice(h * HEAD_DIM, (h + 1) * HEAD_DIM)
        q = rnd(q_ref[:, lanes] * scale)
        kn = rnd(kn_ref[:, lanes])
        vn = rnd(vn_ref[:, lanes])
        s_c = jnp.dot(q.astype(_BF16), kc_ref[h].astype(_BF16), preferred_element_type=_F32) + bc_ref[h]
        s_n = bn_ref[h]
        for t in range(T):
            s_n = s_n + jnp.where(new_col == t, jnp.sum(q * kn[t:t + 1, :], axis=-1, keepdims=True), 0.0)
        m = jnp.maximum(jnp.max(s_c, axis=-1, keepdims=True), jnp.max(s_n, axis=-1, keepdims=True))
        e_c = mc_ref[...] * jnp.exp(s_c - m)
        e_n = mn_ref[...] * jnp.exp(s_n - m)
        den = jnp.sum(e_c, axis=-1, keepdims=True) + jnp.sum(e_n, axis=-1, keepdims=True)
        num = lax.dot_general(e_c.astype(_BF16), vc_ref[h].astype(_BF16), (((1,), (1,)), ((), ())),
                              preferred_element_type=_F32)
        e_nr = rnd(e_n)
        for t in range(T):
            num = num + e_nr[:, t:t + 1] * vn[t:t + 1, :]
        outs.append(num / den)
    o_ref[...] = jnp.concatenate(outs, axis=1)
    ko_ref[...] = jnp.concatenate([kc_ref[...][:, :, T:], knt_ref[...]], axis=-1)
    vo_ref[...] = jnp.concatenate([vc_ref[...][:, :, T:], vnt_ref[...]], axis=-1)


def _attn_sample(q, kn, vn, knt, vnt, kc, vc, tables):
    Bd, T, _ = q.shape
    Lb = kc.shape[-1]
    tok = pl.BlockSpec((None, T, ATTN_WIDTH), lambda b: (b, 0, 0))
    newt = pl.BlockSpec((None, N_HEADS, HEAD_DIM, T), lambda b: (b, 0, 0, 0))
    cache = pl.BlockSpec((None, N_HEADS, HEAD_DIM, Lb), lambda b: (b, 0, 0, 0))
    tabs = [pl.BlockSpec(t.shape, (lambda b, n=t.ndim: (0,) * n)) for t in tables]
    return pl.pallas_call(
        _attn_sample_body,
        grid=(Bd,),
        in_specs=[tok, tok, tok, newt, newt, cache, cache] + tabs,
        out_specs=[tok, cache, cache],
        out_shape=[jax.ShapeDtypeStruct((Bd, T, ATTN_WIDTH), _F32),
                   jax.ShapeDtypeStruct(kc.shape, _F32), jax.ShapeDtypeStruct(vc.shape, _F32)],
        compiler_params=_params(("arbitrary",)),
        name="attn_sample",
    )(q, kn, vn, knt, vnt, kc, vc, *tables)


def _mix_sample_body(alpha, T, x_ref, a_ref, ue_ref, wo_ref, pw_ref, ps_ref, g_ref, b_ref, wr_ref, br_ref, cnt_in_ref,
                     h1_ref, hp_ref, idx_ref, gate_ref, rank_ref, cnt_out_ref, cnt_ref):
    cnt_ref[...] = cnt_in_ref[...]
    hist = POOL_STATE_LEN
    pools = []
    for t in range(T):
        pools.append(_pool_mix(lambda j, lanes, t=t: ue_ref[hist + t - j, :, lanes], ue_ref[hist + t],
                               lambda size: float(size), pw_ref, ps_ref))
    pool = jnp.concatenate(pools, axis=0)
    cat = jnp.concatenate([a_ref[...].astype(_BF16), pool.astype(_BF16)], axis=1)
    mix = jnp.dot(cat, wo_ref[...], preferred_element_type=_F32)
    h1 = _layer_norm(alpha * x_ref[...] + mix, g_ref[...], b_ref[...])
    h1_ref[...] = h1
    hp_ref[...] = _pack_bf16_pairs(h1)
    _route(h1, wr_ref, br_ref, cnt_ref, idx_ref, gate_ref, rank_ref)
    cnt_out_ref[...] = cnt_ref[...]


def _mix_sample(alpha, T, x_tm, attn_tm, uext_tm, w_out_bf16, pool_w_bf16, pool_scale, ln_g, ln_b, wr_t, br, cnt_in):
    n_s, D = x_tm.shape
    full = lambda shape: pl.BlockSpec(shape, lambda i, n=len(shape): (0,) * n)
    ins = [x_tm, attn_tm, uext_tm, w_out_bf16, pool_w_bf16, pool_scale, ln_g, ln_b, wr_t, br, cnt_in]
    outs = [jax.ShapeDtypeStruct((n_s, D), _F32), jax.ShapeDtypeStruct((n_s, D // 2), jnp.uint32),
            jax.ShapeDtypeStruct((TOP_K, n_s), jnp.int32), jax.ShapeDtypeStruct((TOP_K, n_s), _F32),
            jax.ShapeDtypeStruct((TOP_K, n_s), jnp.int32), jax.ShapeDtypeStruct((N_EXPERTS, 1), _F32)]
    return pl.pallas_call(
        functools.partial(_mix_sample_body, alpha, T),
        grid=(1,),
        in_specs=[full(a.shape) for a in ins],
        out_specs=[full(o.shape) for o in outs],
        out_shape=outs,
        scratch_shapes=[pltpu.VMEM((N_EXPERTS, 1), _F32)],
        compiler_params=_params(("arbitrary",)),
        name="mix_sample",
    )(*ins)


def _dispatch_body(n_p_tiles, zflag_ref, dest_hbm, hp_p_ref, hp_s_ref, xs_hbm, tile, zeros, dest_smem,
                   sem_idx, sem_rows, sem_zero):
    i = pl.program_id(0)
    n_steps = pl.num_programs(0)
    tm = tile.shape[1]

    @pl.when(i == 0)
    def _():
        zeros[...] = jnp.zeros_like(zeros)

        blk = zeros.shape[0]

        def zero_copy(j):
            return pltpu.make_async_copy(zeros, xs_hbm.at[pl.ds(pl.multiple_of(j * blk, blk), blk)], sem_zero)

        def start(j, carry):
            @pl.when(zflag_ref[j] != 0)
            def _():
                zero_copy(j).start()
            return carry

        def wait(j, carry):
            @pl.when(zflag_ref[j] != 0)
            def _():
                zero_copy(j).wait()
            return carry

        lax.fori_loop(0, zflag_ref.shape[0], start, 0)
        lax.fori_loop(0, zflag_ref.shape[0], wait, 0)

    def idx_copy(t, s):
        return pltpu.make_async_copy(dest_hbm.at[t], dest_smem.at[pl.ds(s * TOP_K * tm, TOP_K * tm)], sem_idx.at[s])

    def wait_rows(s):
        for _ in range(TOP_K):
            pltpu.make_async_copy(tile.at[s], xs_hbm.at[pl.ds(0, tm)], sem_rows.at[s]).wait()

    @pl.when(i == 0)
    def _():
        idx_copy(i, 0).start()

    staged = jnp.where(i < n_p_tiles, hp_p_ref[...], hp_s_ref[...])

    for s in range(2):
        @pl.when(i % 2 == s)
        def _(s=s):
            @pl.when(i + 1 < n_steps)
            def _():
                idx_copy(i + 1, 1 - s).start()

            tile[s] = staged
            idx_copy(i, s).wait()

            def issue(n, carry):
                for kk in range(TOP_K):
                    pltpu.make_async_copy(tile.at[s, pl.ds(n, 1)], xs_hbm.at[pl.ds(dest_smem[(s * TOP_K + kk) * tm + n], 1)],
                                          sem_rows.at[s]).start()
                return carry

            lax.fori_loop(0, tm, issue, 0, unroll=2)

            @pl.when(i > 0)
            def _():
                wait_rows(1 - s)

            @pl.when(i == n_steps - 1)
            def _():
                wait_rows(s)


def _dispatch(zero_flag, dest_tiles, hp_p, hp_s, n_rows):
    w = hp_p.shape[1]
    tm = ROW_TILE
    n_p_tiles, n_s_tiles = hp_p.shape[0] // tm, hp_s.shape[0] // tm
    grid_spec = pltpu.PrefetchScalarGridSpec(
        num_scalar_prefetch=1,
        grid=(n_p_tiles + n_s_tiles,),
        in_specs=[pl.BlockSpec(memory_space=pl.ANY),
                  pl.BlockSpec((tm, w), lambda i, z: (jnp.minimum(i, n_p_tiles - 1), 0)),
                  pl.BlockSpec((tm, w), lambda i, z: (jnp.maximum(i - n_p_tiles, 0), 0))],
        out_specs=pl.BlockSpec(memory_space=pl.ANY),
        scratch_shapes=[pltpu.VMEM((2, tm, w), jnp.uint32), pltpu.VMEM((MOE_BLOCK, w), jnp.uint32),
                        pltpu.SMEM((2 * TOP_K * tm,), jnp.int32),
                        pltpu.SemaphoreType.DMA((2,)), pltpu.SemaphoreType.DMA((2,)), pltpu.SemaphoreType.DMA(())],
    )
    return pl.pallas_call(
        functools.partial(_dispatch_body, n_p_tiles),
        grid_spec=grid_spec,
        out_shape=jax.ShapeDtypeStruct((n_rows, w), jnp.uint32),
        compiler_params=_params(("arbitrary",)),
        name="moe_dispatch",
    )(zero_flag, dest_tiles, hp_p, hp_s)


def _experts_body(be_ref, na_ref, xs_ref, wu_ref, bu_ref, wd_ref, bd_ref, ys_ref, wu_bf, wd_bf):
    j = pl.program_id(0)
    d_ff = wd_ref.shape[0]
    changed = jnp.logical_or(j == 0, be_ref[j] != be_ref[jnp.maximum(j - 1, 0)])

    @pl.when(jnp.logical_and(j < na_ref[0], changed))
    def _():
        wu_bf[...] = wu_ref[...].astype(_BF16)
        wd_bf[...] = wd_ref[...].astype(_BF16)

    @pl.when(j < na_ref[0])
    def _():
        x = _unpack_bf16_pairs(xs_ref[...])
        h = jnp.dot(x, wu_bf[...], preferred_element_type=_F32) + bu_ref[...]
        g = jnp.minimum(h[:, :d_ff], SWIGLU_LIMIT)
        lin = jnp.clip(h[:, d_ff:], -SWIGLU_LIMIT, SWIGLU_LIMIT)
        act = g * jax.nn.sigmoid(SWIGLU_ALPHA * g) * (lin + 1.0)
        ys_ref[...] = jnp.dot(act.astype(_BF16), wd_bf[...], preferred_element_type=_F32) + bd_ref[...]

    @pl.when(j >= na_ref[0])
    def _():
        ys_ref[...] = jnp.zeros_like(ys_ref)


def _experts(block_e, n_active, xs, w_up, b_up, w_down, b_down):
    n_rows, w = xs.shape
    blk = MOE_BLOCK
    n_blocks = n_rows // blk
    d_model, d_ff2 = w_up.shape[1:]
    d_ff = w_down.shape[1]
    act_blk = lambda j, be, na: jnp.minimum(j, jnp.maximum(na[0] - 1, 0))
    grid_spec = pltpu.PrefetchScalarGridSpec(
        num_scalar_prefetch=2,
        grid=(n_blocks,),
        in_specs=[pl.BlockSpec((blk, w), lambda j, be, na: (act_blk(j, be, na), 0)),
                  pl.BlockSpec((None, d_model, d_ff2), lambda j, be, na: (be[j], 0, 0)),
                  pl.BlockSpec((None, 1, d_ff2), lambda j, be, na: (be[j], 0, 0)),
                  pl.BlockSpec((None, d_ff, d_model), lambda j, be, na: (be[j], 0, 0)),
                  pl.BlockSpec((None, 1, d_model), lambda j, be, na: (be[j], 0, 0))],
        out_specs=pl.BlockSpec((blk, d_model), lambda j, be, na: (j, 0)),
        scratch_shapes=[pltpu.VMEM((d_model, d_ff2), _BF16), pltpu.VMEM((d_ff, d_model), _BF16)],
    )
    return pl.pallas_call(
        _experts_body,
        grid_spec=grid_spec,
        out_shape=jax.ShapeDtypeStruct((n_rows, d_model), _F32),
        compiler_params=_params(("arbitrary",)),
        name="moe_experts",
    )(block_e, n_active, xs, w_up, b_up, w_down, b_down)


def _combine_body(alpha, tile0, dest_hbm, ys_hbm, h1_ref, gate_ref, g_ref, b_ref, o_ref,
                  dest_smem, buf, sem_idx, sem_rows):
    i = pl.program_id(0)
    n_steps = pl.num_programs(0)
    tm = o_ref.shape[0]
    group = 32

    def idx_copy(t, s):
        return pltpu.make_async_copy(dest_hbm.at[tile0 + t], dest_smem.at[pl.ds(s * TOP_K * tm, TOP_K * tm)],
                                     sem_idx.at[s])

    def gather_rows(s, r0):
        for n in range(group):
            for kk in range(TOP_K):
                row = dest_smem[(s * TOP_K + kk) * tm + r0 + n]
                pltpu.make_async_copy(ys_hbm.at[pl.ds(row, 1)], buf.at[s, kk, pl.ds(r0 + n, 1)], sem_rows.at[s]).start()

    def wait_rows(s):
        for kk in range(TOP_K):
            pltpu.make_async_copy(ys_hbm.at[pl.ds(0, tm)], buf.at[s, kk], sem_rows.at[s]).wait()

    def reduce_rows(s, r0):
        rows = pl.ds(r0, group)
        y = gate_ref[rows, 0:1] * buf[s, 0, rows, :]
        for kk in range(1, TOP_K):
            y = y + gate_ref[rows, kk:kk + 1] * buf[s, kk, rows, :]
        o_ref[rows, :] = _layer_norm(alpha * h1_ref[rows, :] + y, g_ref[...], b_ref[...])

    def row_groups(fn):
        def body(gi, carry):
            fn(pl.multiple_of(gi * group, group))
            return carry
        lax.fori_loop(0, tm // group, body, 0)

    @pl.when(i == 0)
    def _():
        idx_copy(i, 0).start()
        idx_copy(i, 0).wait()
        row_groups(lambda r0: gather_rows(0, r0))

        @pl.when(n_steps > 1)
        def _():
            idx_copy(i + 1, 1).start()

    for s in range(2):
        @pl.when(i % 2 == s)
        def _(s=s):
            @pl.when(i + 1 < n_steps)
            def _():
                idx_copy(i + 1, 1 - s).wait()

            @pl.when(i + 2 < n_steps)
            def _():
                idx_copy(i + 2, s).start()

            wait_rows(s)

            @pl.when(i + 1 < n_steps)
            def _():
                def both(r0):
                    gather_rows(1 - s, r0)
                    reduce_rows(s, r0)
                row_groups(both)

            @pl.when(i + 1 >= n_steps)
            def _():
                row_groups(lambda r0: reduce_rows(s, r0))


def _combine(alpha, tile0, dest_tiles, ys, h1, gate_rows, ln_g, ln_b):
    n_tok, d_model = h1.shape
    tm = ROW_TILE
    const2 = lambda i: (0, 0)
    return pl.pallas_call(
        functools.partial(_combine_body, alpha, tile0),
        grid=(n_tok // tm,),
        in_specs=[pl.BlockSpec(memory_space=pl.ANY), pl.BlockSpec(memory_space=pl.ANY),
                  pl.BlockSpec((tm, d_model), lambda i: (i, 0)),
                  pl.BlockSpec((tm, TOP_K), lambda i: (tile0 + i, 0)),
                  pl.BlockSpec(ln_g.shape, const2), pl.BlockSpec(ln_b.shape, const2)],
        out_specs=pl.BlockSpec((tm, d_model), lambda i: (i, 0)),
        out_shape=jax.ShapeDtypeStruct(h1.shape, _F32),
        scratch_shapes=[pltpu.SMEM((2 * TOP_K * tm,), jnp.int32), pltpu.VMEM((2, TOP_K, tm, d_model), _F32),
                        pltpu.SemaphoreType.DMA((2,)), pltpu.SemaphoreType.DMA((2,))],
        compiler_params=_params(("arbitrary",)),
        name="moe_combine_tile%d" % tile0,
    )(dest_tiles, ys, h1, gate_rows, ln_g, ln_b)


def _moe_layout(idx_all, rank_all, counts, n_total):
    blk = MOE_BLOCK
    n_blocks = (TOP_K * n_total + N_EXPERTS * (blk - 1)) // blk + 1
    cnt = counts.reshape(N_EXPERTS).astype(jnp.int32)
    padded = (cnt + blk - 1) // blk * blk
    pend = jnp.cumsum(padded)
    pstart = pend - padded
    experts = jnp.arange(N_EXPERTS, dtype=jnp.int32)
    dest = rank_all + jnp.sum(jnp.where(idx_all[None] == experts[:, None, None], pstart[:, None, None], 0), axis=0)
    block_row0 = jnp.arange(n_blocks, dtype=jnp.int32) * blk
    block_e = jnp.minimum(jnp.sum((pend[None, :] <= block_row0[:, None]).astype(jnp.int32), axis=1), N_EXPERTS - 1)
    n_active = (pend[-1] // blk).astype(jnp.int32).reshape(1)
    last_e = block_e[jnp.maximum(n_active[0] - 1, 0)]
    block_e = jnp.where(jnp.arange(n_blocks) < n_active[0], block_e, last_e)
    tm = ROW_TILE
    dest_tiles = dest.reshape(TOP_K, n_total // tm, tm).transpose(1, 0, 2).reshape(n_total // tm, TOP_K * tm)
    blocks = jnp.arange(n_blocks, dtype=jnp.int32)
    is_last = jnp.any((pend[None, :] == block_row0[:, None] + blk) & (cnt[None, :] > 0), axis=1)
    zero_flag = (is_last | (blocks >= n_active[0])).astype(jnp.int32)
    return dest_tiles, block_e, n_active, zero_flag, n_blocks * blk


def kernel(x_prompt, x_sample, cache_attn_k, cache_attn_v, state_pool, w_in, w_out, pool_w, pool_scale, ln1_g, ln1_b,
           w_router, b_router, w_up, b_up, w_down, b_down, ln2_g, ln2_b):
    depth = w_in.shape[0]
    assert depth == 1, "single-layer step"
    B, S, D = x_prompt.shape
    Bd, T, _ = x_sample.shape
    Lb = cache_attn_k.shape[2]
    assert S % ATTN_TILE == 0 and Lb == WINDOW_MAX and (B * S) % ROW_TILE == 0 and (Bd * T) % ROW_TILE == 0
    alpha = (2 * depth) ** 0.25
    n_p, n_s = B * S, Bd * T
    n_total = n_p + n_s

    w_in_b = w_in[0].astype(_BF16)
    w_out_b = w_out[0].astype(_BF16)
    pool_w_b = pool_w[0].astype(_BF16)
    wr_t = w_router[0].T
    br = b_router[0].reshape(N_EXPERTS, 1)

    q, k, v, u, kt, vt = _inproj_prompt(x_prompt, w_in_b)
    attn = _attn_prompt(q, k, v, _attn_bias_tables())
    h1_p, hp_p, idx_p, gate_p, rank_p, cnt_p = _mix_prompt(
        alpha, x_prompt, attn, u, w_out_b, pool_w_b, pool_scale, ln1_g, ln1_b, wr_t, br)
    n_keep = kt.shape[-1]
    k_prompt = kt.reshape(1, B, N_HEADS, HEAD_DIM, n_keep).transpose(0, 1, 4, 2, 3)
    v_prompt = vt.reshape(1, B, N_HEADS, HEAD_DIM, n_keep).transpose(0, 1, 4, 2, 3)
    pool_prompt = u[:, S - POOL_STATE_LEN:][None]

    hs = _inproj_sample(x_sample.reshape(n_s, D), w_in_b)
    qs = hs[:, :ATTN_WIDTH].reshape(Bd, T, ATTN_WIDTH)
    ks = hs[:, ATTN_WIDTH:2 * ATTN_WIDTH].reshape(Bd, T, ATTN_WIDTH)
    vs = hs[:, 2 * ATTN_WIDTH:3 * ATTN_WIDTH].reshape(Bd, T, ATTN_WIDTH)
    us = hs[:, 3 * ATTN_WIDTH:].reshape(Bd, T, POOL_WIDTH)
    to_t = lambda a: a.reshape(Bd, T, N_HEADS, HEAD_DIM).transpose(0, 2, 3, 1)
    kc = cache_attn_k[0].transpose(0, 2, 3, 1)
    vc = cache_attn_v[0].transpose(0, 2, 3, 1)
    attn_s, k_new, v_new = _attn_sample(qs, ks, vs, to_t(ks), to_t(vs), kc, vc, _decode_tables(T, Lb))
    k_sample = k_new.transpose(0, 3, 1, 2)[None]
    v_sample = v_new.transpose(0, 3, 1, 2)[None]
    uext_tm = jnp.concatenate([state_pool[0].transpose(1, 0, 2), us.transpose(1, 0, 2)], axis=0)
    pool_sample = uext_tm[T:].transpose(1, 0, 2)[None]
    x_tm = x_sample.transpose(1, 0, 2).reshape(n_s, D)
    attn_tm = attn_s.transpose(1, 0, 2).reshape(n_s, ATTN_WIDTH)
    h1_s, hp_s, idx_s, gate_s, rank_s, counts = _mix_sample(
        alpha, T, x_tm, attn_tm, uext_tm, w_out_b, pool_w_b, pool_scale, ln1_g, ln1_b, wr_t, br, cnt_p)

    idx_all = jnp.concatenate([idx_p, idx_s], axis=1)
    rank_all = jnp.concatenate([rank_p, rank_s], axis=1)
    gate_rows = jnp.concatenate([gate_p, gate_s], axis=1).T
    dest_tiles, block_e, n_active, zero_flag, n_rows = _moe_layout(idx_all, rank_all, counts, n_total)
    xs = _dispatch(zero_flag, dest_tiles, hp_p, hp_s, n_rows)
    ys = _experts(block_e, n_active, xs, w_up[0], b_up[0][:, None, :], w_down[0], b_down[0][:, None, :])
    out_p = _combine(alpha, 0, dest_tiles, ys, h1_p, gate_rows, ln2_g, ln2_b)
    out_s = _combine(alpha, n_p // ROW_TILE, dest_tiles, ys, h1_s, gate_rows, ln2_g, ln2_b)

    y_prompt = out_p.reshape(B, S, D)
    y_sample = out_s.reshape(T, Bd, D).transpose(1, 0, 2)
    return (y_prompt, y_sample, k_prompt, v_prompt, pool_prompt, k_sample, v_sample, pool_sample)
```

```python
import functools
import math

import jax
import jax.numpy as jnp
import numpy as np
from jax import lax
from jax.experimental import pallas as pl
from jax.experimental.pallas import tpu as pltpu

HEAD_DIM = 64
N_HEADS = 8
ATTN_WIDTH = N_HEADS * HEAD_DIM
DILATIONS = (1, 4, 16)
WINDOWS = (128, 512, 2048)
N_BACK = 128
WINDOW_MAX = 2048
ATTN_BLOCK = 128
POOL_SIZES = (2, 4, 8, 16)
POOL_GROUP_DIM = 128
POOL_WIDTH = 512
POOL_STATE_LEN = 15
N_EXPERTS = 32
TOP_K = 4
SWIGLU_ALPHA = 1.702
SWIGLU_LIMIT = 7.0
LN_EPS = 1e-5
NEG_INF = -1e30

LANES = 128
HEADS_PER_CHUNK = LANES // HEAD_DIM
N_CHUNKS = ATTN_WIDTH // LANES
VMEM_LIMIT = 56 * 1024 * 1024

ATTN_TILE = 2048
BLOCK_UNROLL = 16
PROJ_TILE = 512
MOE_BLOCK = 512
ROW_TILE = 128

_F32 = jnp.float32
_BF16 = jnp.bfloat16


def _alibi_slopes():
    return 2.0 ** (-8.0 * np.arange(1, N_HEADS + 1, dtype=np.float64) / N_HEADS)


def _params(sem, **kw):
    return pltpu.CompilerParams(dimension_semantics=sem, vmem_limit_bytes=VMEM_LIMIT, **kw)


def _inproj_body(n_keep_tiles, x_ref, w_ref, q_ref, k_ref, v_ref, u_ref, kt_ref, vt_ref):
    i = pl.program_id(1)
    n_tiles = pl.num_programs(1)
    h = jnp.dot(x_ref[...].astype(_BF16), w_ref[...], preferred_element_type=_F32)
    for j in range(N_CHUNKS):
        q_ref[j] = h[:, j * LANES:(j + 1) * LANES]
        k_ref[j] = h[:, ATTN_WIDTH + j * LANES:ATTN_WIDTH + (j + 1) * LANES]
        v_ref[j] = h[:, 2 * ATTN_WIDTH + j * LANES:2 * ATTN_WIDTH + (j + 1) * LANES]
    u_ref[...] = h[:, 3 * ATTN_WIDTH:]

    @pl.when(i >= n_tiles - n_keep_tiles)
    def _():
        kt_ref[...] = h[:, ATTN_WIDTH:2 * ATTN_WIDTH].T
        vt_ref[...] = h[:, 2 * ATTN_WIDTH:3 * ATTN_WIDTH].T


def _inproj_prompt(x, w_in_bf16):
    B, S, D = x.shape
    tm = PROJ_TILE
    n_tiles = S // tm
    n_keep = min(WINDOW_MAX, S)
    n_keep_tiles = n_keep // tm
    first_keep = n_tiles - n_keep_tiles
    chunked = jax.ShapeDtypeStruct((B, N_CHUNKS, S, LANES), _F32)
    chunk_spec = pl.BlockSpec((None, N_CHUNKS, tm, LANES), lambda b, i: (b, 0, i, 0))
    t_spec = pl.BlockSpec((None, ATTN_WIDTH, tm), lambda b, i: (b, 0, jnp.maximum(i - first_keep, 0)))
    return pl.pallas_call(
        functools.partial(_inproj_body, n_keep_tiles),
        grid=(B, n_tiles),
        in_specs=[pl.BlockSpec((None, tm, D), lambda b, i: (b, i, 0)),
                  pl.BlockSpec(w_in_bf16.shape, lambda b, i: (0, 0))],
        out_specs=[chunk_spec, chunk_spec, chunk_spec,
                   pl.BlockSpec((None, tm, POOL_WIDTH), lambda b, i: (b, i, 0)),
                   t_spec, t_spec],
        out_shape=[chunked, chunked, chunked,
                   jax.ShapeDtypeStruct((B, S, POOL_WIDTH), _F32),
                   jax.ShapeDtypeStruct((B, ATTN_WIDTH, n_keep), _F32),
                   jax.ShapeDtypeStruct((B, ATTN_WIDTH, n_keep), _F32)],
        compiler_params=_params(("arbitrary", "arbitrary")),
        name="inproj_prompt",
    )(x, w_in_bf16)


def _attn_bias_tables():
    qi = np.arange(ATTN_BLOCK)[:, None]
    kj = np.arange(2 * ATTN_BLOCK)[None, :]
    step = qi - kj + ATTN_BLOCK
    valid = (step >= 0) & (step <= N_BACK)
    slopes = _alibi_slopes()
    out = np.zeros((N_CHUNKS, len(DILATIONS), HEADS_PER_CHUNK * ATTN_BLOCK, 2 * ATTN_BLOCK), np.float32)
    for c in range(N_CHUNKS):
        for p, dil in enumerate(DILATIONS):
            for hh in range(HEADS_PER_CHUNK):
                bias = -slopes[c * HEADS_PER_CHUNK + hh] * (step * dil).astype(np.float64)
                out[c, p, hh * ATTN_BLOCK:(hh + 1) * ATTN_BLOCK] = np.where(valid, bias, NEG_INF)
    return jnp.asarray(out)


def _attn_body(q_ref, k_ref, v_ref, bias_ref, o_ref,
               qf1, kf1, vf1, qf4, kf4, vf4, qf16, kf16, vf16, acc_n, acc_m, acc_d):
    i = pl.program_id(2)
    T = ATTN_TILE
    folded = ((1, qf1, kf1, vf1), (4, qf4, kf4, vf4), (16, qf16, kf16, vf16))
    scale = HEAD_DIM ** -0.5

    for dil, _, kf, vf in folded:
        L = T // dil

        @pl.when(i == 0)
        def _():
            kf[:, 0:ATTN_BLOCK, :] = jnp.zeros((dil, ATTN_BLOCK, LANES), _BF16)
            vf[:, 0:ATTN_BLOCK, :] = jnp.zeros((dil, ATTN_BLOCK, LANES), _BF16)

        @pl.when(i > 0)
        def _():
            kf[:, 0:ATTN_BLOCK, :] = kf[:, L:L + ATTN_BLOCK, :]
            vf[:, 0:ATTN_BLOCK, :] = vf[:, L:L + ATTN_BLOCK, :]

    for dil, qf, kf, vf in folded:
        L = T // dil
        for r in range(dil):
            rows = pl.ds(r, L, stride=dil) if dil > 1 else pl.ds(0, L)
            qf[r] = (q_ref[rows, :] * scale).astype(_BF16)
            kf[r, ATTN_BLOCK:ATTN_BLOCK + L, :] = k_ref[rows, :].astype(_BF16)
            vf[r, ATTN_BLOCK:ATTN_BLOCK + L, :] = v_ref[rows, :].astype(_BF16)

    lane = lax.broadcasted_iota(jnp.int32, (ATTN_BLOCK, LANES), 1)
    first_head = lane < HEAD_DIM
    col = lax.broadcasted_iota(jnp.int32, (HEADS_PER_CHUNK * ATTN_BLOCK, 2 * ATTN_BLOCK), 1)
    prev_half = col < ATTN_BLOCK

    for p, (dil, qf, kf, vf) in enumerate(folded):
        blocks_per_res = T // dil // ATTN_BLOCK

        def block(blk, carry, p=p, dil=dil, qf=qf, kf=kf, vf=vf, blocks_per_res=blocks_per_res):
            r = blk // blocks_per_res
            c = blk % blocks_per_res
            row0 = pl.multiple_of(c * ATTN_BLOCK, ATTN_BLOCK)
            q = qf[r, pl.ds(row0, ATTN_BLOCK), :]
            kk = kf[r, pl.ds(row0, 2 * ATTN_BLOCK), :]
            vv = vf[r, pl.ds(row0, 2 * ATTN_BLOCK), :]
            zero = jnp.zeros_like(q)
            qm = jnp.concatenate([jnp.where(first_head, q, zero), jnp.where(first_head, zero, q)], axis=0)
            s = lax.dot_general(qm, kk, (((1,), (1,)), ((), ())), preferred_element_type=_F32)
            s = s + bias_ref[p]
            s = jnp.where(jnp.logical_and(prev_half, jnp.logical_and(i == 0, c == 0)), NEG_INF, s)
            m = jnp.max(s, axis=-1, keepdims=True)
            e = jnp.exp(s - m)
            den = jnp.sum(e, axis=-1, keepdims=True)
            pv = jnp.dot(e.astype(_BF16), vv, preferred_element_type=_F32)
            num = jnp.where(first_head, pv[:ATTN_BLOCK], pv[ATTN_BLOCK:])
            m2 = jnp.where(first_head, m[:ATTN_BLOCK], m[ATTN_BLOCK:])
            d2 = jnp.where(first_head, den[:ATTN_BLOCK], den[ATTN_BLOCK:])
            start = r + row0 * dil
            rows = pl.ds(start, ATTN_BLOCK, stride=dil) if dil > 1 else pl.ds(start, ATTN_BLOCK)
            acc_n[p, rows, :] = num
            acc_m[p, rows, :] = m2
            acc_d[p, rows, :] = d2
            return carry

        lax.fori_loop(0, T // ATTN_BLOCK, block, 0, unroll=BLOCK_UNROLL)

    chunk_rows = 256

    def merge(t, carry):
        rows = pl.ds(pl.multiple_of(t * chunk_rows, chunk_rows), chunk_rows)
        m0, m1, m2 = acc_m[0, rows, :], acc_m[1, rows, :], acc_m[2, rows, :]
        mx = jnp.maximum(jnp.maximum(m0, m1), m2)
        e0, e1, e2 = jnp.exp(m0 - mx), jnp.exp(m1 - mx), jnp.exp(m2 - mx)
        num = acc_n[0, rows, :] * e0 + acc_n[1, rows, :] * e1 + acc_n[2, rows, :] * e2
        den = acc_d[0, rows, :] * e0 + acc_d[1, rows, :] * e1 + acc_d[2, rows, :] * e2
        o_ref[rows, :] = (num / den).astype(o_ref.dtype)
        return carry

    lax.fori_loop(0, T // chunk_rows, merge, 0)


def _attn_prompt(q, k, v, bias):
    B, _, S, _ = q.shape
    T = ATTN_TILE
    io_spec = pl.BlockSpec((None, None, T, LANES), lambda b, c, i: (b, c, i, 0))
    scratch = []
    for dil in DILATIONS:
        L = T // dil
        scratch += [pltpu.VMEM((dil, L, LANES), _BF16),
                    pltpu.VMEM((dil, ATTN_BLOCK + L, LANES), _BF16),
                    pltpu.VMEM((dil, ATTN_BLOCK + L, LANES), _BF16)]
    scratch += [pltpu.VMEM((len(DILATIONS), T, LANES), _F32)] * 3
    return pl.pallas_call(
        _attn_body,
        grid=(B, N_CHUNKS, S // T),
        in_specs=[io_spec, io_spec, io_spec,
                  pl.BlockSpec((None,) + bias.shape[1:], lambda b, c, i: (c, 0, 0, 0))],
        out_specs=io_spec,
        out_shape=jax.ShapeDtypeStruct((B, N_CHUNKS, S, LANES), _BF16),
        scratch_shapes=scratch,
        compiler_params=_params(("arbitrary", "arbitrary", "arbitrary")),
        name="attn_prompt",
    )(q, k, v, bias)


def _layer_norm(z, g, b):
    mu = jnp.mean(z, axis=-1, keepdims=True)
    zc = z - mu
    var = jnp.mean(zc * zc, axis=-1, keepdims=True)
    return zc * lax.rsqrt(var + LN_EPS) * g + b


def _pack_bf16_pairs(h):
    w = h.shape[1] // 2
    bits = pltpu.bitcast(h.astype(_BF16).astype(_F32), jnp.uint32)
    return (bits[:, w:] & jnp.uint32(0xFFFF0000)) | (bits[:, :w] >> 16)


def _unpack_bf16_pairs(words):
    lo = pltpu.bitcast(words << 16, _F32)
    hi = pltpu.bitcast(words & jnp.uint32(0xFFFF0000), _F32)
    return jnp.concatenate([lo, hi], axis=1).astype(_BF16)


def _route(h1, wr_ref, br_ref, cnt_ref, idx_ref, gate_ref, rank_ref):
    tm = h1.shape[0]
    logits = lax.dot_general(wr_ref[...], h1, (((1,), (1,)), ((), ())),
                             precision=lax.Precision.HIGHEST, preferred_element_type=_F32) + br_ref[...]
    eid = lax.broadcasted_iota(jnp.int32, (N_EXPERTS, tm), 0)
    work = logits
    vals, ids, hots = [], [], []
    for _ in range(TOP_K):
        mx = jnp.max(work, axis=0, keepdims=True)
        sel = jnp.min(jnp.where(work == mx, eid, N_EXPERTS), axis=0, keepdims=True)
        hot = eid == sel
        vals.append(mx)
        ids.append(sel)
        hots.append(hot)
        work = jnp.where(hot, -jnp.inf, work)
    ex = [jnp.exp(v - vals[0]) for v in vals]
    tot = ex[0] + ex[1] + ex[2] + ex[3]
    any_hot = jnp.where(hots[0] | hots[1] | hots[2] | hots[3], 1.0, 0.0)
    earlier = (lax.broadcasted_iota(jnp.int32, (tm, tm), 0) < lax.broadcasted_iota(jnp.int32, (tm, tm), 1))
    prefix = jnp.dot(any_hot.astype(_BF16), jnp.where(earlier, 1.0, 0.0).astype(_BF16),
                     preferred_element_type=_F32) + cnt_ref[...]
    for kk in range(TOP_K):
        idx_ref[kk:kk + 1, :] = ids[kk]
        gate_ref[kk:kk + 1, :] = ex[kk] / tot
        rank_ref[kk:kk + 1, :] = jnp.sum(jnp.where(hots[kk], prefix, 0.0), axis=0, keepdims=True).astype(jnp.int32)
    cnt_ref[...] = cnt_ref[...] + jnp.sum(any_hot, axis=1, keepdims=True)


def _pool_mix(win_fn, u_tile, cnt_fn, pw_ref, ps_ref):
    outs = []
    for g, size in enumerate(POOL_SIZES):
        lanes = slice(g * POOL_GROUP_DIM, (g + 1) * POOL_GROUP_DIM)
        win = win_fn(0, lanes)
        for j in range(1, size):
            win = win + win_fn(j, lanes)
        d = win / cnt_fn(size) - u_tile[:, lanes]
        y = jnp.dot(d.astype(_BF16), pw_ref[g], preferred_element_type=_F32)
        outs.append(y * ps_ref[:, lanes])
    return jnp.concatenate(outs, axis=1)


def _mix_prompt_body(alpha, x_ref, a_ref, u_ref, wo_ref, pw_ref, ps_ref, g_ref, b_ref, wr_ref, br_ref,
                     h1_ref, hp_ref, idx_ref, gate_ref, rank_ref, cnt_out_ref, uext, cnt_ref):
    b = pl.program_id(0)
    i = pl.program_id(1)
    tm = x_ref.shape[0]
    halo = 16

    @pl.when(jnp.logical_and(b == 0, i == 0))
    def _():
        cnt_ref[...] = jnp.zeros_like(cnt_ref)

    @pl.when(i == 0)
    def _():
        uext[0:halo, :] = jnp.zeros((halo, POOL_WIDTH), _F32)

    @pl.when(i > 0)
    def _():
        uext[0:halo, :] = uext[tm:tm + halo, :]

    uext[halo:halo + tm, :] = u_ref[...]
    pos = i * tm + lax.broadcasted_iota(jnp.int32, (tm, 1), 0)
    pool = _pool_mix(lambda j, lanes: uext[halo - j:halo - j + tm, lanes], u_ref,
                     lambda size: jnp.minimum(pos + 1, size).astype(_F32), pw_ref, ps_ref)
    cat = jnp.concatenate([a_ref[j] for j in range(N_CHUNKS)] + [pool.astype(_BF16)], axis=1)
    mix = jnp.dot(cat, wo_ref[...], preferred_element_type=_F32)
    h1 = _layer_norm(alpha * x_ref[...] + mix, g_ref[...], b_ref[...])
    h1_ref[...] = h1
    hp_ref[...] = _pack_bf16_pairs(h1)
    _route(h1, wr_ref, br_ref, cnt_ref, idx_ref, gate_ref, rank_ref)
    cnt_out_ref[...] = cnt_ref[...]


def _mix_prompt(alpha, x, attn, u, w_out_bf16, pool_w_bf16, pool_scale, ln_g, ln_b, wr_t, br):
    B, S, D = x.shape
    n_total = B * S
    tm = PROJ_TILE
    n_tiles = S // tm
    tok = lambda b, i: (b * n_tiles + i, 0)
    tok_t = lambda b, i: (0, b * n_tiles + i)
    const2 = lambda b, i: (0, 0)
    return pl.pallas_call(
        functools.partial(_mix_prompt_body, alpha),
        grid=(B, n_tiles),
        in_specs=[pl.BlockSpec((None, tm, D), lambda b, i: (b, i, 0)),
                  pl.BlockSpec((None, N_CHUNKS, tm, LANES), lambda b, i: (b, 0, i, 0)),
                  pl.BlockSpec((None, tm, POOL_WIDTH), lambda b, i: (b, i, 0)),
                  pl.BlockSpec(w_out_bf16.shape, const2),
                  pl.BlockSpec(pool_w_bf16.shape, lambda b, i: (0, 0, 0)),
                  pl.BlockSpec(pool_scale.shape, const2),
                  pl.BlockSpec(ln_g.shape, const2), pl.BlockSpec(ln_b.shape, const2),
                  pl.BlockSpec(wr_t.shape, const2), pl.BlockSpec(br.shape, const2)],
        out_specs=[pl.BlockSpec((tm, D), tok), pl.BlockSpec((tm, D // 2), tok),
                   pl.BlockSpec((TOP_K, tm), tok_t), pl.BlockSpec((TOP_K, tm), tok_t),
                   pl.BlockSpec((TOP_K, tm), tok_t), pl.BlockSpec((N_EXPERTS, 1), const2)],
        out_shape=[jax.ShapeDtypeStruct((n_total, D), _F32), jax.ShapeDtypeStruct((n_total, D // 2), jnp.uint32),
                   jax.ShapeDtypeStruct((TOP_K, n_total), jnp.int32), jax.ShapeDtypeStruct((TOP_K, n_total), _F32),
                   jax.ShapeDtypeStruct((TOP_K, n_total), jnp.int32), jax.ShapeDtypeStruct((N_EXPERTS, 1), _F32)],
        scratch_shapes=[pltpu.VMEM((16 + tm, POOL_WIDTH), _F32), pltpu.VMEM((N_EXPERTS, 1), _F32)],
        compiler_params=_params(("arbitrary", "arbitrary")),
        name="mix_prompt",
    )(x, attn, u, w_out_bf16, pool_w_bf16, pool_scale, ln_g, ln_b, wr_t, br)


def _inproj_sample_body(x_ref, w_ref, o_ref):
    o_ref[...] = jnp.dot(x_ref[...].astype(_BF16), w_ref[...], preferred_element_type=_F32)


def _inproj_sample(x2d, w_in_bf16):
    return pl.pallas_call(
        _inproj_sample_body,
        out_shape=jax.ShapeDtypeStruct((x2d.shape[0], w_in_bf16.shape[1]), _F32),
        compiler_params=_params(None),
        name="inproj_sample",
    )(x2d, w_in_bf16)


def _decode_tables(T, Lb):
    slopes = _alibi_slopes()

    def mult(delta):
        m = np.zeros(delta.shape, np.float64)
        for win, dil in zip(WINDOWS, DILATIONS):
            m += ((delta % dil == 0) & (delta // dil <= win // dil) & (delta >= 0))
        return m

    t = np.arange(T)[:, None]
    d_cache = Lb + t - np.arange(Lb)[None, :]
    d_new = t - np.arange(T)[None, :]
    m_cache, m_new = mult(d_cache), mult(d_new)
    b_cache = np.where(m_cache > 0, -slopes[:, None, None] * d_cache[None], NEG_INF)
    b_new = np.where(m_new > 0, -slopes[:, None, None] * d_new[None], NEG_INF)
    f = lambda a: jnp.asarray(a.astype(np.float32))
    return f(b_cache), f(m_cache), f(b_new), f(m_new)


def _attn_sample_body(q_ref, kn_ref, vn_ref, knt_ref, vnt_ref, kc_ref, vc_ref, bc_ref, mc_ref, bn_ref, mn_ref,
                      o_ref, ko_ref, vo_ref):
    T = q_ref.shape[0]
    scale = HEAD_DIM ** -0.5
    rnd = lambda a: a.astype(_BF16).astype(_F32)
    new_col = lax.broadcasted_iota(jnp.int32, (T, T), 1)
    outs = []
    for h in range(N_HEADS):
        lanes = slice(h * HEAD_DIM, (h + 1) * HEAD_DIM)
        q = rnd(q_ref[:, lanes] * scale)
        kn = rnd(kn_ref[:, lanes])
        vn = rnd(vn_ref[:, lanes])
        s_c = jnp.dot(q.astype(_BF16), kc_ref[h].astype(_BF16), preferred_element_type=_F32) + bc_ref[h]
        s_n = bn_ref[h]
        for t in range(T):
            s_n = s_n + jnp.where(new_col == t, jnp.sum(q * kn[t:t + 1, :], axis=-1, keepdims=True), 0.0)
        m = jnp.maximum(jnp.max(s_c, axis=-1, keepdims=True), jnp.max(s_n, axis=-1, keepdims=True))
        e_c = mc_ref[...] * jnp.exp(s_c - m)
        e_n = mn_ref[...] * jnp.exp(s_n - m)
        den = jnp.sum(e_c, axis=-1, keepdims=True) + jnp.sum(e_n, axis=-1, keepdims=True)
        num = lax.dot_general(e_c.astype(_BF16), vc_ref[h].astype(_BF16), (((1,), (1,)), ((), ())),
                              preferred_element_type=_F32)
        e_nr = rnd(e_n)
        for t in range(T):
            num = num + e_nr[:, t:t + 1] * vn[t:t + 1, :]
        outs.append(num / den)
    o_ref[...] = jnp.concatenate(outs, axis=1)
    ko_ref[...] = jnp.concatenate([kc_ref[...][:, :, T:], knt_ref[...]], axis=-1)
    vo_ref[...] = jnp.concatenate([vc_ref[...][:, :, T:], vnt_ref[...]], axis=-1)


def _attn_sample(q, kn, vn, knt, vnt, kc, vc, tables):
    Bd, T, _ = q.shape
    Lb = kc.shape[-1]
    tok = pl.BlockSpec((None, T, ATTN_WIDTH), lambda b: (b, 0, 0))
    newt = pl.BlockSpec((None, N_HEADS, HEAD_DIM, T), lambda b: (b, 0, 0, 0))
    cache = pl.BlockSpec((None, N_HEADS, HEAD_DIM, Lb), lambda b: (b, 0, 0, 0))
    tabs = [pl.BlockSpec(t.shape, (lambda b, n=t.ndim: (0,) * n)) for t in tables]
    return pl.pallas_call(
        _attn_sample_body,
        grid=(Bd,),
        in_specs=[tok, tok, tok, newt, newt, cache, cache] + tabs,
        out_specs=[tok, cache, cache],
        out_shape=[jax.ShapeDtypeStruct((Bd, T, ATTN_WIDTH), _F32),
                   jax.ShapeDtypeStruct(kc.shape, _F32), jax.ShapeDtypeStruct(vc.shape, _F32)],
        compiler_params=_params(("arbitrary",)),
        name="attn_sample",
    )(q, kn, vn, knt, vnt, kc, vc, *tables)


def _mix_sample_body(alpha, T, x_ref, a_ref, ue_ref, wo_ref, pw_ref, ps_ref, g_ref, b_ref, wr_ref, br_ref, cnt_in_ref,
                     h1_ref, hp_ref, idx_ref, gate_ref, rank_ref, cnt_out_ref, cnt_ref):
    cnt_ref[...] = cnt_in_ref[...]
    hist = POOL_STATE_LEN
    pools = []
    for t in range(T):
        pools.append(_pool_mix(lambda j, lanes, t=t: ue_ref[hist + t - j, :, lanes], ue_ref[hist + t],
                               lambda size: float(size), pw_ref, ps_ref))
    pool = jnp.concatenate(pools, axis=0)
    cat = jnp.concatenate([a_ref[...].astype(_BF16), pool.astype(_BF16)], axis=1)
    mix = jnp.dot(cat, wo_ref[...], preferred_element_type=_F32)
    h1 = _layer_norm(alpha * x_ref[...] + mix, g_ref[...], b_ref[...])
    h1_ref[...] = h1
    hp_ref[...] = _pack_bf16_pairs(h1)
    _route(h1, wr_ref, br_ref, cnt_ref, idx_ref, gate_ref, rank_ref)
    cnt_out_ref[...] = cnt_ref[...]


def _mix_sample(alpha, T, x_tm, attn_tm, uext_tm, w_out_bf16, pool_w_bf16, pool_scale, ln_g, ln_b, wr_t, br, cnt_in):
    n_s, D = x_tm.shape
    full = lambda shape: pl.BlockSpec(shape, lambda i, n=len(shape): (0,) * n)
    ins = [x_tm, attn_tm, uext_tm, w_out_bf16, pool_w_bf16, pool_scale, ln_g, ln_b, wr_t, br, cnt_in]
    outs = [jax.ShapeDtypeStruct((n_s, D), _F32), jax.ShapeDtypeStruct((n_s, D // 2), jnp.uint32),
            jax.ShapeDtypeStruct((TOP_K, n_s), jnp.int32), jax.ShapeDtypeStruct((TOP_K, n_s), _F32),
            jax.ShapeDtypeStruct((TOP_K, n_s), jnp.int32), jax.ShapeDtypeStruct((N_EXPERTS, 1), _F32)]
    return pl.pallas_call(
        functools.partial(_mix_sample_body, alpha, T),
        grid=(1,),
        in_specs=[full(a.shape) for a in ins],
        out_specs=[full(o.shape) for o in outs],
        out_shape=outs,
        scratch_shapes=[pltpu.VMEM((N_EXPERTS, 1), _F32)],
        compiler_params=_params(("arbitrary",)),
        name="mix_sample",
    )(*ins)


def _dispatch_body(n_p_tiles, zflag_ref, dest_hbm, hp_p_ref, hp_s_ref, xs_hbm, tile, zeros, dest_smem,
                   sem_idx, sem_rows, sem_zero):
    i = pl.program_id(0)
    n_steps = pl.num_programs(0)
    tm = tile.shape[1]

    @pl.when(i == 0)
    def _():
        zeros[...] = jnp.zeros_like(zeros)

        blk = zeros.shape[0]

        def zero_copy(j):
            return pltpu.make_async_copy(zeros, xs_hbm.at[pl.ds(pl.multiple_of(j * blk, blk), blk)], sem_zero)

        def start(j, carry):
            @pl.when(zflag_ref[j] != 0)
            def _():
                zero_copy(j).start()
            return carry

        def wait(j, carry):
            @pl.when(zflag_ref[j] != 0)
            def _():
                zero_copy(j).wait()
            return carry

        lax.fori_loop(0, zflag_ref.shape[0], start, 0)
        lax.fori_loop(0, zflag_ref.shape[0], wait, 0)

    def idx_copy(t, s):
        return pltpu.make_async_copy(dest_hbm.at[t], dest_smem.at[pl.ds(s * TOP_K * tm, TOP_K * tm)], sem_idx.at[s])

    def wait_rows(s):
        for _ in range(TOP_K):
            pltpu.make_async_copy(tile.at[s], xs_hbm.at[pl.ds(0, tm)], sem_rows.at[s]).wait()

    @pl.when(i == 0)
    def _():
        idx_copy(i, 0).start()

    staged = jnp.where(i < n_p_tiles, hp_p_ref[...], hp_s_ref[...])

    for s in range(2):
        @pl.when(i % 2 == s)
        def _(s=s):
            @pl.when(i + 1 < n_steps)
            def _():
                idx_copy(i + 1, 1 - s).start()

            tile[s] = staged
            idx_copy(i, s).wait()

            def issue(n, carry):
                for kk in range(TOP_K):
                    pltpu.make_async_copy(tile.at[s, pl.ds(n, 1)], xs_hbm.at[pl.ds(dest_smem[(s * TOP_K + kk) * tm + n], 1)],
                                          sem_rows.at[s]).start()
                return carry

            lax.fori_loop(0, tm, issue, 0, unroll=2)

            @pl.when(i > 0)
            def _():
                wait_rows(1 - s)

            @pl.when(i == n_steps - 1)
            def _():
                wait_rows(s)


def _dispatch(zero_flag, dest_tiles, hp_p, hp_s, n_rows):
    w = hp_p.shape[1]
    tm = ROW_TILE
    n_p_tiles, n_s_tiles = hp_p.shape[0] // tm, hp_s.shape[0] // tm
    grid_spec = pltpu.PrefetchScalarGridSpec(
        num_scalar_prefetch=1,
        grid=(n_p_tiles + n_s_tiles,),
        in_specs=[pl.BlockSpec(memory_space=pl.ANY),
                  pl.BlockSpec((tm, w), lambda i, z: (jnp.minimum(i, n_p_tiles - 1), 0)),
                  pl.BlockSpec((tm, w), lambda i, z: (jnp.maximum(i - n_p_tiles, 0), 0))],
        out_specs=pl.BlockSpec(memory_space=pl.ANY),
        scratch_shapes=[pltpu.VMEM((2, tm, w), jnp.uint32), pltpu.VMEM((MOE_BLOCK, w), jnp.uint32),
                        pltpu.SMEM((2 * TOP_K * tm,), jnp.int32),
                        pltpu.SemaphoreType.DMA((2,)), pltpu.SemaphoreType.DMA((2,)), pltpu.SemaphoreType.DMA(())],
    )
    return pl.pallas_call(
        functools.partial(_dispatch_body, n_p_tiles),
        grid_spec=grid_spec,
        out_shape=jax.ShapeDtypeStruct((n_rows, w), jnp.uint32),
        compiler_params=_params(("arbitrary",)),
        name="moe_dispatch",
    )(zero_flag, dest_tiles, hp_p, hp_s)


def _experts_body(be_ref, na_ref, xs_ref, wu_ref, bu_ref, wd_ref, bd_ref, ys_ref, wu_bf, wd_bf):
    j = pl.program_id(0)
    d_ff = wd_ref.shape[0]
    changed = jnp.logical_or(j == 0, be_ref[j] != be_ref[jnp.maximum(j - 1, 0)])

    @pl.when(jnp.logical_and(j < na_ref[0], changed))
    def _():
        wu_bf[...] = wu_ref[...].astype(_BF16)
        wd_bf[...] = wd_ref[...].astype(_BF16)

    @pl.when(j < na_ref[0])
    def _():
        x = _unpack_bf16_pairs(xs_ref[...])
        h = jnp.dot(x, wu_bf[...], preferred_element_type=_F32) + bu_ref[...]
        g = jnp.minimum(h[:, :d_ff], SWIGLU_LIMIT)
        lin = jnp.clip(h[:, d_ff:], -SWIGLU_LIMIT, SWIGLU_LIMIT)
        act = g * jax.nn.sigmoid(SWIGLU_ALPHA * g) * (lin + 1.0)
        y = jnp.dot(act.astype(_BF16), wd_bf[...], preferred_element_type=_F32) + bd_ref[...]
        ys_ref[...] = _pack_bf16_pairs(y)

    @pl.when(j >= na_ref[0])
    def _():
        ys_ref[...] = jnp.zeros_like(ys_ref)


def _experts(block_e, n_active, xs, w_up, b_up, w_down, b_down):
    n_rows, w = xs.shape
    blk = MOE_BLOCK
    n_blocks = n_rows // blk
    d_model, d_ff2 = w_up.shape[1:]
    d_ff = w_down.shape[1]
    act_blk = lambda j, be, na: jnp.minimum(j, jnp.maximum(na[0] - 1, 0))
    grid_spec = pltpu.PrefetchScalarGridSpec(
        num_scalar_prefetch=2,
        grid=(n_blocks,),
        in_specs=[pl.BlockSpec((blk, w), lambda j, be, na: (act_blk(j, be, na), 0)),
                  pl.BlockSpec((None, d_model, d_ff2), lambda j, be, na: (be[j], 0, 0)),
                  pl.BlockSpec((None, 1, d_ff2), lambda j, be, na: (be[j], 0, 0)),
                  pl.BlockSpec((None, d_ff, d_model), lambda j, be, na: (be[j], 0, 0)),
                  pl.BlockSpec((None, 1, d_model), lambda j, be, na: (be[j], 0, 0))],
        out_specs=pl.BlockSpec((blk, d_model // 2), lambda j, be, na: (j, 0)),
        scratch_shapes=[pltpu.VMEM((d_model, d_ff2), _BF16), pltpu.VMEM((d_ff, d_model), _BF16)],
    )
    return pl.pallas_call(
        _experts_body,
        grid_spec=grid_spec,
        out_shape=jax.ShapeDtypeStruct((n_rows, d_model // 2), jnp.uint32),
        compiler_params=_params(("arbitrary",)),
        name="moe_experts",
    )(block_e, n_active, xs, w_up, b_up, w_down, b_down)


def _combine_body(alpha, tile0, dest_hbm, ys_hbm, h1_ref, gate_ref, g_ref, b_ref, o_ref,
                  dest_smem, buf, sem_idx, sem_rows):
    i = pl.program_id(0)
    n_steps = pl.num_programs(0)
    tm = o_ref.shape[0]
    group = 32

    def idx_copy(t, s):
        return pltpu.make_async_copy(dest_hbm.at[tile0 + t], dest_smem.at[pl.ds(s * TOP_K * tm, TOP_K * tm)],
                                     sem_idx.at[s])

    def gather_rows(s, r0):
        for n in range(group):
            for kk in range(TOP_K):
                row = dest_smem[(s * TOP_K + kk) * tm + r0 + n]
                pltpu.make_async_copy(ys_hbm.at[pl.ds(row, 1)], buf.at[s, kk, pl.ds(r0 + n, 1)], sem_rows.at[s]).start()

    def wait_rows(s):
        for kk in range(TOP_K):
            pltpu.make_async_copy(ys_hbm.at[pl.ds(0, tm)], buf.at[s, kk], sem_rows.at[s]).wait()

    def reduce_rows(s, r0):
        rows = pl.ds(r0, group)
        lo = hi = None
        for kk in range(TOP_K):
            words = buf[s, kk, rows, :]
            gate = gate_ref[rows, kk:kk + 1]
            t_lo = gate * pltpu.bitcast(words << 16, _F32)
            t_hi = gate * pltpu.bitcast(words & jnp.uint32(0xFFFF0000), _F32)
            lo, hi = (t_lo, t_hi) if kk == 0 else (lo + t_lo, hi + t_hi)
        o_ref[rows, :] = _layer_norm(alpha * h1_ref[rows, :] + jnp.concatenate([lo, hi], axis=1), g_ref[...], b_ref[...])

    def row_groups(fn):
        def body(gi, carry):
            fn(pl.multiple_of(gi * group, group))
            return carry
        lax.fori_loop(0, tm // group, body, 0)

    @pl.when(i == 0)
    def _():
        idx_copy(i, 0).start()
        idx_copy(i, 0).wait()
        row_groups(lambda r0: gather_rows(0, r0))

        @pl.when(n_steps > 1)
        def _():
            idx_copy(i + 1, 1).start()

    for s in range(2):
        @pl.when(i % 2 == s)
        def _(s=s):
            @pl.when(i + 1 < n_steps)
            def _():
                idx_copy(i + 1, 1 - s).wait()
                row_groups(lambda r0: gather_rows(1 - s, r0))

            @pl.when(i + 2 < n_steps)
            def _():
                idx_copy(i + 2, s).start()

            wait_rows(s)
            row_groups(lambda r0: reduce_rows(s, r0))


def _combine(alpha, tile0, dest_tiles, ys, h1, gate_rows, ln_g, ln_b):
    n_tok, d_model = h1.shape
    tm = ROW_TILE
    const2 = lambda i: (0, 0)
    return pl.pallas_call(
        functools.partial(_combine_body, alpha, tile0),
        grid=(n_tok // tm,),
        in_specs=[pl.BlockSpec(memory_space=pl.ANY), pl.BlockSpec(memory_space=pl.ANY),
                  pl.BlockSpec((tm, d_model), lambda i: (i, 0)),
                  pl.BlockSpec((tm, TOP_K), lambda i: (tile0 + i, 0)),
                  pl.BlockSpec(ln_g.shape, const2), pl.BlockSpec(ln_b.shape, const2)],
        out_specs=pl.BlockSpec((tm, d_model), lambda i: (i, 0)),
        out_shape=jax.ShapeDtypeStruct(h1.shape, _F32),
        scratch_shapes=[pltpu.SMEM((2 * TOP_K * tm,), jnp.int32), pltpu.VMEM((2, TOP_K, tm, ys.shape[1]), ys.dtype),
                        pltpu.SemaphoreType.DMA((2,)), pltpu.SemaphoreType.DMA((2,))],
        compiler_params=_params(("arbitrary",)),
        name="moe_combine_tile%d" % tile0,
    )(dest_tiles, ys, h1, gate_rows, ln_g, ln_b)


def _moe_layout(idx_all, rank_all, counts, n_total):
    blk = MOE_BLOCK
    n_blocks = (TOP_K * n_total + N_EXPERTS * (blk - 1)) // blk + 1
    cnt = counts.reshape(N_EXPERTS).astype(jnp.int32)
    padded = (cnt + blk - 1) // blk * blk
    pend = jnp.cumsum(padded)
    pstart = pend - padded
    experts = jnp.arange(N_EXPERTS, dtype=jnp.int32)
    dest = rank_all + jnp.sum(jnp.where(idx_all[None] == experts[:, None, None], pstart[:, None, None], 0), axis=0)
    block_row0 = jnp.arange(n_blocks, dtype=jnp.int32) * blk
    block_e = jnp.minimum(jnp.sum((pend[None, :] <= block_row0[:, None]).astype(jnp.int32), axis=1), N_EXPERTS - 1)
    n_active = (pend[-1] // blk).astype(jnp.int32).reshape(1)
    last_e = block_e[jnp.maximum(n_active[0] - 1, 0)]
    block_e = jnp.where(jnp.arange(n_blocks) < n_active[0], block_e, last_e)
    tm = ROW_TILE
    dest_tiles = dest.reshape(TOP_K, n_total // tm, tm).transpose(1, 0, 2).reshape(n_total // tm, TOP_K * tm)
    blocks = jnp.arange(n_blocks, dtype=jnp.int32)
    is_last = jnp.any((pend[None, :] == block_row0[:, None] + blk) & (cnt[None, :] > 0), axis=1)
    zero_flag = (is_last | (blocks >= n_active[0])).astype(jnp.int32)
    return dest_tiles, block_e, n_active, zero_flag, n_blocks * blk


def kernel(x_prompt, x_sample, cache_attn_k, cache_attn_v, state_pool, w_in, w_out, pool_w, pool_scale, ln1_g, ln1_b,
           w_router, b_router, w_up, b_up, w_down, b_down, ln2_g, ln2_b):
    depth = w_in.shape[0]
    assert depth == 1, "single-layer step"
    B, S, D = x_prompt.shape
    Bd, T, _ = x_sample.shape
    Lb = cache_attn_k.shape[2]
    assert S % ATTN_TILE == 0 and Lb == WINDOW_MAX and (B * S) % ROW_TILE == 0 and (Bd * T) % ROW_TILE == 0
    alpha = (2 * depth) ** 0.25
    n_p, n_s = B * S, Bd * T
    n_total = n_p + n_s

    w_in_b = w_in[0].astype(_BF16)
    w_out_b = w_out[0].astype(_BF16)
    pool_w_b = pool_w[0].astype(_BF16)
    wr_t = w_router[0].T
    br = b_router[0].reshape(N_EXPERTS, 1)

    q, k, v, u, kt, vt = _inproj_prompt(x_prompt, w_in_b)
    attn = _attn_prompt(q, k, v, _attn_bias_tables())
    h1_p, hp_p, idx_p, gate_p, rank_p, cnt_p = _mix_prompt(
        alpha, x_prompt, attn, u, w_out_b, pool_w_b, pool_scale, ln1_g, ln1_b, wr_t, br)
    n_keep = kt.shape[-1]
    k_prompt = kt.reshape(1, B, N_HEADS, HEAD_DIM, n_keep).transpose(0, 1, 4, 2, 3)
    v_prompt = vt.reshape(1, B, N_HEADS, HEAD_DIM, n_keep).transpose(0, 1, 4, 2, 3)
    pool_prompt = u[:, S - POOL_STATE_LEN:][None]

    hs = _inproj_sample(x_sample.reshape(n_s, D), w_in_b)
    qs = hs[:, :ATTN_WIDTH].reshape(Bd, T, ATTN_WIDTH)
    ks = hs[:, ATTN_WIDTH:2 * ATTN_WIDTH].reshape(Bd, T, ATTN_WIDTH)
    vs = hs[:, 2 * ATTN_WIDTH:3 * ATTN_WIDTH].reshape(Bd, T, ATTN_WIDTH)
    us = hs[:, 3 * ATTN_WIDTH:].reshape(Bd, T, POOL_WIDTH)
    to_t = lambda a: a.reshape(Bd, T, N_HEADS, HEAD_DIM).transpose(0, 2, 3, 1)
    kc = cache_attn_k[0].transpose(0, 2, 3, 1)
    vc = cache_attn_v[0].transpose(0, 2, 3, 1)
    attn_s, k_new, v_new = _attn_sample(qs, ks, vs, to_t(ks), to_t(vs), kc, vc, _decode_tables(T, Lb))
    k_sample = k_new.transpose(0, 3, 1, 2)[None]
    v_sample = v_new.transpose(0, 3, 1, 2)[None]
    uext_tm = jnp.concatenate([state_pool[0].transpose(1, 0, 2), us.transpose(1, 0, 2)], axis=0)
    pool_sample = uext_tm[T:].transpose(1, 0, 2)[None]
    x_tm = x_sample.transpose(1, 0, 2).reshape(n_s, D)
    attn_tm = attn_s.transpose(1, 0, 2).reshape(n_s, ATTN_WIDTH)
    h1_s, hp_s, idx_s, gate_s, rank_s, counts = _mix_sample(
        alpha, T, x_tm, attn_tm, uext_tm, w_out_b, pool_w_b, pool_scale, ln1_g, ln1_b, wr_t, br, cnt_p)

    idx_all = jnp.concatenate([idx_p, idx_s], axis=1)
    rank_all = jnp.concatenate([rank_p, rank_s], axis=1)
    gate_rows = jnp.concatenate([gate_p, gate_s], axis=1).T
    dest_tiles, block_e, n_active, zero_flag, n_rows = _moe_layout(idx_all, rank_all, counts, n_total)
    xs = _dispatch(zero_flag, dest_tiles, hp_p, hp_s, n_rows)
    ys = _experts(block_e, n_active, xs, w_up[0], b_up[0][:, None, :], w_down[0], b_down[0][:, None, :])
    out_p = _combine(alpha, 0, dest_tiles, ys, h1_p, gate_rows, ln2_g, ln2_b)
    out_s = _combine(alpha, n_p // ROW_TILE, dest_tiles, ys, h1_s, gate_rows, ln2_g, ln2_b)

    y_prompt = out_p.reshape(B, S, D)
    y_sample = out_s.reshape(T, Bd, D).transpose(1, 0, 2)
    return (y_prompt, y_sample, k_prompt, v_prompt, pool_prompt, k_sample, v_sample, pool_sample)
```

```python
import functools

import jax
import jax.numpy as jnp
import numpy as np
from jax import lax
from jax.experimental import pallas as pl
from jax.experimental.pallas import tpu as pltpu

HEAD_DIM = 64
N_HEADS = 8
ATTN_WIDTH = N_HEADS * HEAD_DIM
DILATIONS = (1, 4, 16)
WINDOWS = (128, 512, 2048)
N_BACK = 128
WINDOW_MAX = 2048
ATTN_BLOCK = 128
POOL_SIZES = (2, 4, 8, 16)
POOL_GROUP_DIM = 128
POOL_WIDTH = 512
POOL_STATE_LEN = 15
N_EXPERTS = 32
TOP_K = 4
SWIGLU_ALPHA = 1.702
SWIGLU_LIMIT = 7.0
LN_EPS = 1e-5
NEG_INF = -1e30

LANES = 128
HEADS_PER_CHUNK = LANES // HEAD_DIM
N_CHUNKS = ATTN_WIDTH // LANES
VMEM_LIMIT = 56 * 1024 * 1024

ATTN_TILE = 2048
BLOCK_UNROLL = 16
PROJ_TILE = 512
MOE_BLOCK = 512
ROW_TILE = 128

_F32 = jnp.float32
_BF16 = jnp.bfloat16


def _alibi_slopes():
    return 2.0 ** (-8.0 * np.arange(1, N_HEADS + 1, dtype=np.float64) / N_HEADS)


def _params(sem, **kw):
    return pltpu.CompilerParams(dimension_semantics=sem, vmem_limit_bytes=VMEM_LIMIT, **kw)


def _inproj_body(n_keep_tiles, x_ref, w_ref, q_ref, k_ref, v_ref, u_ref, kt_ref, vt_ref):
    i = pl.program_id(1)
    n_tiles = pl.num_programs(1)
    h = jnp.dot(x_ref[...].astype(_BF16), w_ref[...], preferred_element_type=_F32)
    for j in range(N_CHUNKS):
        q_ref[j] = h[:, j * LANES:(j + 1) * LANES]
        k_ref[j] = h[:, ATTN_WIDTH + j * LANES:ATTN_WIDTH + (j + 1) * LANES]
        v_ref[j] = h[:, 2 * ATTN_WIDTH + j * LANES:2 * ATTN_WIDTH + (j + 1) * LANES]
    u_ref[...] = h[:, 3 * ATTN_WIDTH:]

    @pl.when(i >= n_tiles - n_keep_tiles)
    def _():
        kt_ref[...] = h[:, ATTN_WIDTH:2 * ATTN_WIDTH].T
        vt_ref[...] = h[:, 2 * ATTN_WIDTH:3 * ATTN_WIDTH].T


def _inproj_prompt(x, w_in_bf16):
    B, S, D = x.shape
    tm = PROJ_TILE
    n_tiles = S // tm
    n_keep = min(WINDOW_MAX, S)
    n_keep_tiles = n_keep // tm
    first_keep = n_tiles - n_keep_tiles
    chunked = jax.ShapeDtypeStruct((B, N_CHUNKS, S, LANES), _F32)
    chunk_spec = pl.BlockSpec((None, N_CHUNKS, tm, LANES), lambda b, i: (b, 0, i, 0))
    t_spec = pl.BlockSpec((None, ATTN_WIDTH, tm), lambda b, i: (b, 0, jnp.maximum(i - first_keep, 0)))
    return pl.pallas_call(
        functools.partial(_inproj_body, n_keep_tiles),
        grid=(B, n_tiles),
        in_specs=[pl.BlockSpec((None, tm, D), lambda b, i: (b, i, 0)),
                  pl.BlockSpec(w_in_bf16.shape, lambda b, i: (0, 0))],
        out_specs=[chunk_spec, chunk_spec, chunk_spec,
                   pl.BlockSpec((None, tm, POOL_WIDTH), lambda b, i: (b, i, 0)),
                   t_spec, t_spec],
        out_shape=[chunked, chunked, chunked,
                   jax.ShapeDtypeStruct((B, S, POOL_WIDTH), _F32),
                   jax.ShapeDtypeStruct((B, ATTN_WIDTH, n_keep), _F32),
                   jax.ShapeDtypeStruct((B, ATTN_WIDTH, n_keep), _F32)],
        compiler_params=_params(("arbitrary", "arbitrary")),
        name="inproj_prompt",
    )(x, w_in_bf16)


def _attn_bias_tables():
    qi = np.arange(ATTN_BLOCK)[:, None]
    kj = np.arange(2 * ATTN_BLOCK)[None, :]
    step = qi - kj + ATTN_BLOCK
    valid = (step >= 0) & (step <= N_BACK)
    slopes = _alibi_slopes()
    out = np.zeros((N_CHUNKS, len(DILATIONS), HEADS_PER_CHUNK * ATTN_BLOCK, 2 * ATTN_BLOCK), np.float32)
    for c in range(N_CHUNKS):
        for p, dil in enumerate(DILATIONS):
            for hh in range(HEADS_PER_CHUNK):
                bias = -slopes[c * HEADS_PER_CHUNK + hh] * (step * dil).astype(np.float64)
                out[c, p, hh * ATTN_BLOCK:(hh + 1) * ATTN_BLOCK] = np.where(valid, bias, NEG_INF)
    return jnp.asarray(out)


def _attn_body(q_ref, k_ref, v_ref, bias_ref, o_ref,
               qf1, kf1, vf1, qf4, kf4, vf4, qf16, kf16, vf16, acc_n, acc_m, acc_d):
    i = pl.program_id(2)
    T = ATTN_TILE
    folded = ((1, qf1, kf1, vf1), (4, qf4, kf4, vf4), (16, qf16, kf16, vf16))
    scale = HEAD_DIM ** -0.5

    for dil, _, kf, vf in folded:
        L = T // dil

        @pl.when(i == 0)
        def _():
            kf[:, 0:ATTN_BLOCK, :] = jnp.zeros((dil, ATTN_BLOCK, LANES), _BF16)
            vf[:, 0:ATTN_BLOCK, :] = jnp.zeros((dil, ATTN_BLOCK, LANES), _BF16)

        @pl.when(i > 0)
        def _():
            kf[:, 0:ATTN_BLOCK, :] = kf[:, L:L + ATTN_BLOCK, :]
            vf[:, 0:ATTN_BLOCK, :] = vf[:, L:L + ATTN_BLOCK, :]

    for dil, qf, kf, vf in folded:
        L = T // dil
        for r in range(dil):
            rows = pl.ds(r, L, stride=dil) if dil > 1 else pl.ds(0, L)
            qf[r] = (q_ref[rows, :] * scale).astype(_BF16)
            kf[r, ATTN_BLOCK:ATTN_BLOCK + L, :] = k_ref[rows, :].astype(_BF16)
            vf[r, ATTN_BLOCK:ATTN_BLOCK + L, :] = v_ref[rows, :].astype(_BF16)

    lane = lax.broadcasted_iota(jnp.int32, (ATTN_BLOCK, LANES), 1)
    first_head = lane < HEAD_DIM
    col = lax.broadcasted_iota(jnp.int32, (HEADS_PER_CHUNK * ATTN_BLOCK, 2 * ATTN_BLOCK), 1)
    prev_half = col < ATTN_BLOCK

    for p, (dil, qf, kf, vf) in enumerate(folded):
        blocks_per_res = T // dil // ATTN_BLOCK

        def block(blk, carry, p=p, dil=dil, qf=qf, kf=kf, vf=vf, blocks_per_res=blocks_per_res):
            r = blk // blocks_per_res
            c = blk % blocks_per_res
            row0 = pl.multiple_of(c * ATTN_BLOCK, ATTN_BLOCK)
            q = qf[r, pl.ds(row0, ATTN_BLOCK), :]
            kk = kf[r, pl.ds(row0, 2 * ATTN_BLOCK), :]
            vv = vf[r, pl.ds(row0, 2 * ATTN_BLOCK), :]
            zero = jnp.zeros_like(q)
            qm = jnp.concatenate([jnp.where(first_head, q, zero), jnp.where(first_head, zero, q)], axis=0)
            s = lax.dot_general(qm, kk, (((1,), (1,)), ((), ())), preferred_element_type=_F32)
            s = s + bias_ref[p]
            s = jnp.where(jnp.logical_and(prev_half, jnp.logical_and(i == 0, c == 0)), NEG_INF, s)
            m = jnp.max(s, axis=-1, keepdims=True)
            e = jnp.exp(s - m)
            den = jnp.sum(e, axis=-1, keepdims=True)
            pv = jnp.dot(e.astype(_BF16), vv, preferred_element_type=_F32)
            num = jnp.where(first_head, pv[:ATTN_BLOCK], pv[ATTN_BLOCK:])
            m2 = jnp.where(first_head, m[:ATTN_BLOCK], m[ATTN_BLOCK:])
            d2 = jnp.where(first_head, den[:ATTN_BLOCK], den[ATTN_BLOCK:])
            start = r + row0 * dil
            rows = pl.ds(start, ATTN_BLOCK, stride=dil) if dil > 1 else pl.ds(start, ATTN_BLOCK)
            acc_n[p, rows, :] = num
            acc_m[p, rows, :] = m2
            acc_d[p, rows, :] = d2
            return carry

        lax.fori_loop(0, T // ATTN_BLOCK, block, 0, unroll=BLOCK_UNROLL)

    chunk_rows = 256

    def merge(t, carry):
        rows = pl.ds(pl.multiple_of(t * chunk_rows, chunk_rows), chunk_rows)
        m0, m1, m2 = acc_m[0, rows, :], acc_m[1, rows, :], acc_m[2, rows, :]
        mx = jnp.maximum(jnp.maximum(m0, m1), m2)
        e0, e1, e2 = jnp.exp(m0 - mx), jnp.exp(m1 - mx), jnp.exp(m2 - mx)
        num = acc_n[0, rows, :] * e0 + acc_n[1, rows, :] * e1 + acc_n[2, rows, :] * e2
        den = acc_d[0, rows, :] * e0 + acc_d[1, rows, :] * e1 + acc_d[2, rows, :] * e2
        o_ref[rows, :] = (num / den).astype(o_ref.dtype)
        return carry

    lax.fori_loop(0, T // chunk_rows, merge, 0)


def _attn_prompt(q, k, v, bias):
    B, _, S, _ = q.shape
    T = ATTN_TILE
    io_spec = pl.BlockSpec((None, None, T, LANES), lambda b, c, i: (b, c, i, 0))
    scratch = []
    for dil in DILATIONS:
        L = T // dil
        scratch += [pltpu.VMEM((dil, L, LANES), _BF16),
                    pltpu.VMEM((dil, ATTN_BLOCK + L, LANES), _BF16),
                    pltpu.VMEM((dil, ATTN_BLOCK + L, LANES), _BF16)]
    scratch += [pltpu.VMEM((len(DILATIONS), T, LANES), _F32)] * 3
    return pl.pallas_call(
        _attn_body,
        grid=(B, N_CHUNKS, S // T),
        in_specs=[io_spec, io_spec, io_spec,
                  pl.BlockSpec((None,) + bias.shape[1:], lambda b, c, i: (c, 0, 0, 0))],
        out_specs=io_spec,
        out_shape=jax.ShapeDtypeStruct((B, N_CHUNKS, S, LANES), _BF16),
        scratch_shapes=scratch,
        compiler_params=_params(("arbitrary", "arbitrary", "arbitrary")),
        name="attn_prompt",
    )(q, k, v, bias)


def _layer_norm(z, g, b):
    mu = jnp.mean(z, axis=-1, keepdims=True)
    zc = z - mu
    var = jnp.mean(zc * zc, axis=-1, keepdims=True)
    return zc * lax.rsqrt(var + LN_EPS) * g + b


def _pack_bf16_pairs(h):
    w = h.shape[1] // 2
    bits = pltpu.bitcast(h.astype(_BF16).astype(_F32), jnp.uint32)
    return (bits[:, w:] & jnp.uint32(0xFFFF0000)) | (bits[:, :w] >> 16)


def _unpack_bf16_pairs(words):
    lo = pltpu.bitcast(words << 16, _F32)
    hi = pltpu.bitcast(words & jnp.uint32(0xFFFF0000), _F32)
    return jnp.concatenate([lo, hi], axis=1).astype(_BF16)


def _route(h1, wr_ref, br_ref, cnt_ref, idx_ref, gate_ref, rank_ref):
    tm = h1.shape[0]
    logits = lax.dot_general(wr_ref[...], h1, (((1,), (1,)), ((), ())),
                             precision=lax.Precision.HIGHEST, preferred_element_type=_F32) + br_ref[...]
    eid = lax.broadcasted_iota(jnp.int32, (N_EXPERTS, tm), 0)
    work = logits
    vals, ids, hots = [], [], []
    for _ in range(TOP_K):
        mx = jnp.max(work, axis=0, keepdims=True)
        sel = jnp.min(jnp.where(work == mx, eid, N_EXPERTS), axis=0, keepdims=True)
        hot = eid == sel
        vals.append(mx)
        ids.append(sel)
        hots.append(hot)
        work = jnp.where(hot, -jnp.inf, work)
    ex = [jnp.exp(v - vals[0]) for v in vals]
    tot = ex[0] + ex[1] + ex[2] + ex[3]
    any_hot = jnp.where(hots[0] | hots[1] | hots[2] | hots[3], 1.0, 0.0)
    earlier = (lax.broadcasted_iota(jnp.int32, (tm, tm), 0) < lax.broadcasted_iota(jnp.int32, (tm, tm), 1))
    prefix = jnp.dot(any_hot.astype(_BF16), jnp.where(earlier, 1.0, 0.0).astype(_BF16),
                     preferred_element_type=_F32) + cnt_ref[...]
    for kk in range(TOP_K):
        idx_ref[kk:kk + 1, :] = ids[kk]
        gate_ref[kk:kk + 1, :] = ex[kk] / tot
        rank_ref[kk:kk + 1, :] = jnp.sum(jnp.where(hots[kk], prefix, 0.0), axis=0, keepdims=True).astype(jnp.int32)
    cnt_ref[...] = cnt_ref[...] + jnp.sum(any_hot, axis=1, keepdims=True)


def _pool_mix(win_fn, u_tile, cnt_fn, pw_ref, ps_ref):
    outs = []
    for g, size in enumerate(POOL_SIZES):
        lanes = slice(g * POOL_GROUP_DIM, (g + 1) * POOL_GROUP_DIM)
        win = win_fn(0, lanes)
        for j in range(1, size):
            win = win + win_fn(j, lanes)
        d = win / cnt_fn(size) - u_tile[:, lanes]
        y = jnp.dot(d.astype(_BF16), pw_ref[g], preferred_element_type=_F32)
        outs.append(y * ps_ref[:, lanes])
    return jnp.concatenate(outs, axis=1)


def _mix_prompt_body(alpha, x_ref, a_ref, u_ref, wo_ref, pw_ref, ps_ref, g_ref, b_ref, wr_ref, br_ref,
                     h1_ref, hp_ref, idx_ref, gate_ref, rank_ref, cnt_out_ref, uext, cnt_ref):
    b = pl.program_id(0)
    i = pl.program_id(1)
    tm = x_ref.shape[0]
    halo = 16

    @pl.when(jnp.logical_and(b == 0, i == 0))
    def _():
        cnt_ref[...] = jnp.zeros_like(cnt_ref)

    @pl.when(i == 0)
    def _():
        uext[0:halo, :] = jnp.zeros((halo, POOL_WIDTH), _F32)

    @pl.when(i > 0)
    def _():
        uext[0:halo, :] = uext[tm:tm + halo, :]

    uext[halo:halo + tm, :] = u_ref[...]
    pos = i * tm + lax.broadcasted_iota(jnp.int32, (tm, 1), 0)
    pool = _pool_mix(lambda j, lanes: uext[halo - j:halo - j + tm, lanes], u_ref,
                     lambda size: jnp.minimum(pos + 1, size).astype(_F32), pw_ref, ps_ref)
    cat = jnp.concatenate([a_ref[j] for j in range(N_CHUNKS)] + [pool.astype(_BF16)], axis=1)
    mix = jnp.dot(cat, wo_ref[...], preferred_element_type=_F32)
    h1 = _layer_norm(alpha * x_ref[...] + mix, g_ref[...], b_ref[...])
    h1_ref[...] = h1
    hp_ref[...] = _pack_bf16_pairs(h1)
    _route(h1, wr_ref, br_ref, cnt_ref, idx_ref, gate_ref, rank_ref)
    cnt_out_ref[...] = cnt_ref[...]


def _mix_prompt(alpha, x, attn, u, w_out_bf16, pool_w_bf16, pool_scale, ln_g, ln_b, wr_t, br):
    B, S, D = x.shape
    n_total = B * S
    tm = PROJ_TILE
    n_tiles = S // tm
    tok = lambda b, i: (b * n_tiles + i, 0)
    tok_t = lambda b, i: (0, b * n_tiles + i)
    const2 = lambda b, i: (0, 0)
    return pl.pallas_call(
        functools.partial(_mix_prompt_body, alpha),
        grid=(B, n_tiles),
        in_specs=[pl.BlockSpec((None, tm, D), lambda b, i: (b, i, 0)),
                  pl.BlockSpec((None, N_CHUNKS, tm, LANES), lambda b, i: (b, 0, i, 0)),
                  pl.BlockSpec((None, tm, POOL_WIDTH), lambda b, i: (b, i, 0)),
                  pl.BlockSpec(w_out_bf16.shape, const2),
                  pl.BlockSpec(pool_w_bf16.shape, lambda b, i: (0, 0, 0)),
                  pl.BlockSpec(pool_scale.shape, const2),
                  pl.BlockSpec(ln_g.shape, const2), pl.BlockSpec(ln_b.shape, const2),
                  pl.BlockSpec(wr_t.shape, const2), pl.BlockSpec(br.shape, const2)],
        out_specs=[pl.BlockSpec((tm, D), tok), pl.BlockSpec((tm, D // 2), tok),
                   pl.BlockSpec((TOP_K, tm), tok_t), pl.BlockSpec((TOP_K, tm), tok_t),
                   pl.BlockSpec((TOP_K, tm), tok_t), pl.BlockSpec((N_EXPERTS, 1), const2)],
        out_shape=[jax.ShapeDtypeStruct((n_total, D), _F32), jax.ShapeDtypeStruct((n_total, D // 2), jnp.uint32),
                   jax.ShapeDtypeStruct((TOP_K, n_total), jnp.int32), jax.ShapeDtypeStruct((TOP_K, n_total), _F32),
                   jax.ShapeDtypeStruct((TOP_K, n_total), jnp.int32), jax.ShapeDtypeStruct((N_EXPERTS, 1), _F32)],
        scratch_shapes=[pltpu.VMEM((16 + tm, POOL_WIDTH), _F32), pltpu.VMEM((N_EXPERTS, 1), _F32)],
        compiler_params=_params(("arbitrary", "arbitrary")),
        name="mix_prompt",
    )(x, attn, u, w_out_bf16, pool_w_bf16, pool_scale, ln_g, ln_b, wr_t, br)


def _inproj_sample_body(x_ref, w_ref, o_ref):
    o_ref[...] = jnp.dot(x_ref[...].astype(_BF16), w_ref[...], preferred_element_type=_F32)


def _inproj_sample(x2d, w_in_bf16):
    return pl.pallas_call(
        _inproj_sample_body,
        out_shape=jax.ShapeDtypeStruct((x2d.shape[0], w_in_bf16.shape[1]), _F32),
        compiler_params=_params(None),
        name="inproj_sample",
    )(x2d, w_in_bf16)


def _decode_tables(T, Lb):
    slopes = _alibi_slopes()

    def mult(delta):
        m = np.zeros(delta.shape, np.float64)
        for win, dil in zip(WINDOWS, DILATIONS):
            m += ((delta % dil == 0) & (delta // dil <= win // dil) & (delta >= 0))
        return m

    t = np.arange(T)[:, None]
    d_cache = Lb + t - np.arange(Lb)[None, :]
    d_new = t - np.arange(T)[None, :]
    m_cache, m_new = mult(d_cache), mult(d_new)
    b_cache = np.where(m_cache > 0, -slopes[:, None, None] * d_cache[None], NEG_INF)
    b_new = np.where(m_new > 0, -slopes[:, None, None] * d_new[None], NEG_INF)
    f = lambda a: jnp.asarray(a.astype(np.float32))
    return f(b_cache), f(m_cache), f(b_new), f(m_new)


def _attn_sample_body(q_ref, kn_ref, vn_ref, knt_ref, vnt_ref, kc_ref, vc_ref, bc_ref, mc_ref, bn_ref, mn_ref,
                      o_ref, ko_ref, vo_ref):
    T = q_ref.shape[0]
    scale = HEAD_DIM ** -0.5
    rnd = lambda a: a.astype(_BF16).astype(_F32)
    new_col = lax.broadcasted_iota(jnp.int32, (T, T), 1)
    outs = []
    for h in range(N_HEADS):
        lanes = slice(h * HEAD_DIM, (h + 1) * HEAD_DIM)
        q = rnd(q_ref[:, lanes] * scale)
        kn = rnd(kn_ref[:, lanes])
        vn = rnd(vn_ref[:, lanes])
        s_c = jnp.dot(q.astype(_BF16), kc_ref[h].astype(_BF16), preferred_element_type=_F32) + bc_ref[h]
        s_n = bn_ref[h]
        for t in range(T):
            s_n = s_n + jnp.where(new_col == t, jnp.sum(q * kn[t:t + 1, :], axis=-1, keepdims=True), 0.0)
        m = jnp.maximum(jnp.max(s_c, axis=-1, keepdims=True), jnp.max(s_n, axis=-1, keepdims=True))
        e_c = mc_ref[...] * jnp.exp(s_c - m)
        e_n = mn_ref[...] * jnp.exp(s_n - m)
        den = jnp.sum(e_c, axis=-1, keepdims=True) + jnp.sum(e_n, axis=-1, keepdims=True)
        num = lax.dot_general(e_c.astype(_BF16), vc_ref[h].astype(_BF16), (((1,), (1,)), ((), ())),
                              preferred_element_type=_F32)
        e_nr = rnd(e_n)
        for t in range(T):
            num = num + e_nr[:, t:t + 1] * vn[t:t + 1, :]
        outs.append(num / den)
    o_ref[...] = jnp.concatenate(outs, axis=1)
    ko_ref[...] = jnp.concatenate([kc_ref[...][:, :, T:], knt_ref[...]], axis=-1)
    vo_ref[...] = jnp.concatenate([vc_ref[...][:, :, T:], vnt_ref[...]], axis=-1)


def _attn_sample(q, kn, vn, knt, vnt, kc, vc, tables):
    Bd, T, _ = q.shape
    Lb = kc.shape[-1]
    tok = pl.BlockSpec((None, T, ATTN_WIDTH), lambda b: (b, 0, 0))
    newt = pl.BlockSpec((None, N_HEADS, HEAD_DIM, T), lambda b: (b, 0, 0, 0))
    cache = pl.BlockSpec((None, N_HEADS, HEAD_DIM, Lb), lambda b: (b, 0, 0, 0))
    tabs = [pl.BlockSpec(t.shape, (lambda b, n=t.ndim: (0,) * n)) for t in tables]
    return pl.pallas_call(
        _attn_sample_body,
        grid=(Bd,),
        in_specs=[tok, tok, tok, newt, newt, cache, cache] + tabs,
        out_specs=[tok, cache, cache],
        out_shape=[jax.ShapeDtypeStruct((Bd, T, ATTN_WIDTH), _F32),
                   jax.ShapeDtypeStruct(kc.shape, _F32), jax.ShapeDtypeStruct(vc.shape, _F32)],
        compiler_params=_params(("arbitrary",)),
        name="attn_sample",
    )(q, kn, vn, knt, vnt, kc, vc, *tables)


def _mix_sample_body(alpha, T, x_ref, a_ref, ue_ref, wo_ref, pw_ref, ps_ref, g_ref, b_ref, wr_ref, br_ref, cnt_in_ref,
                     h1_ref, hp_ref, idx_ref, gate_ref, rank_ref, cnt_out_ref, cnt_ref):
    cnt_ref[...] = cnt_in_ref[...]
    hist = POOL_STATE_LEN
    pools = []
    for t in range(T):
        pools.append(_pool_mix(lambda j, lanes, t=t: ue_ref[hist + t - j, :, lanes], ue_ref[hist + t],
                               lambda size: float(size), pw_ref, ps_ref))
    pool = jnp.concatenate(pools, axis=0)
    cat = jnp.concatenate([a_ref[...].astype(_BF16), pool.astype(_BF16)], axis=1)
    mix = jnp.dot(cat, wo_ref[...], preferred_element_type=_F32)
    h1 = _layer_norm(alpha * x_ref[...] + mix, g_ref[...], b_ref[...])
    h1_ref[...] = h1
    hp_ref[...] = _pack_bf16_pairs(h1)
    _route(h1, wr_ref, br_ref, cnt_ref, idx_ref, gate_ref, rank_ref)
    cnt_out_ref[...] = cnt_ref[...]


def _mix_sample(alpha, T, x_tm, attn_tm, uext_tm, w_out_bf16, pool_w_bf16, pool_scale, ln_g, ln_b, wr_t, br, cnt_in):
    n_s, D = x_tm.shape
    full = lambda shape: pl.BlockSpec(shape, lambda i, n=len(shape): (0,) * n)
    ins = [x_tm, attn_tm, uext_tm, w_out_bf16, pool_w_bf16, pool_scale, ln_g, ln_b, wr_t, br, cnt_in]
    outs = [jax.ShapeDtypeStruct((n_s, D), _F32), jax.ShapeDtypeStruct((n_s, D // 2), jnp.uint32),
            jax.ShapeDtypeStruct((TOP_K, n_s), jnp.int32), jax.ShapeDtypeStruct((TOP_K, n_s), _F32),
            jax.ShapeDtypeStruct((TOP_K, n_s), jnp.int32), jax.ShapeDtypeStruct((N_EXPERTS, 1), _F32)]
    return pl.pallas_call(
        functools.partial(_mix_sample_body, alpha, T),
        grid=(1,),
        in_specs=[full(a.shape) for a in ins],
        out_specs=[full(o.shape) for o in outs],
        out_shape=outs,
        scratch_shapes=[pltpu.VMEM((N_EXPERTS, 1), _F32)],
        compiler_params=_params(("arbitrary",)),
        name="mix_sample",
    )(*ins)


def _dispatch_body(n_p_tiles, zflag_ref, dest_hbm, hp_p_ref, hp_s_ref, xs_hbm, tile, zeros, dest_smem,
                   sem_idx, sem_rows, sem_zero):
    i = pl.program_id(0)
    n_steps = pl.num_programs(0)
    tm = tile.shape[1]

    @pl.when(i == 0)
    def _():
        zeros[...] = jnp.zeros_like(zeros)

        blk = zeros.shape[0]

        def zero_copy(j):
            return pltpu.make_async_copy(zeros, xs_hbm.at[pl.ds(pl.multiple_of(j * blk, blk), blk)], sem_zero)

        def start(j, carry):
            @pl.when(zflag_ref[j] != 0)
            def _():
                zero_copy(j).start()
            return carry

        def wait(j, carry):
            @pl.when(zflag_ref[j] != 0)
            def _():
                zero_copy(j).wait()
            return carry

        lax.fori_loop(0, zflag_ref.shape[0], start, 0)
        lax.fori_loop(0, zflag_ref.shape[0], wait, 0)

    def idx_copy(t, s):
        return pltpu.make_async_copy(dest_hbm.at[t], dest_smem.at[pl.ds(s * TOP_K * tm, TOP_K * tm)], sem_idx.at[s])

    def wait_rows(s):
        for _ in range(TOP_K):
            pltpu.make_async_copy(tile.at[s], xs_hbm.at[pl.ds(0, tm)], sem_rows.at[s]).wait()

    @pl.when(i == 0)
    def _():
        idx_copy(i, 0).start()

    staged = jnp.where(i < n_p_tiles, hp_p_ref[...], hp_s_ref[...])

    for s in range(2):
        @pl.when(i % 2 == s)
        def _(s=s):
            @pl.when(i + 1 < n_steps)
            def _():
                idx_copy(i + 1, 1 - s).start()

            tile[s] = staged
            idx_copy(i, s).wait()

            for n in range(tm):
                for kk in range(TOP_K):
                    pltpu.make_async_copy(tile.at[s, pl.ds(n, 1)], xs_hbm.at[pl.ds(dest_smem[(s * TOP_K + kk) * tm + n], 1)],
                                          sem_rows.at[s]).start(priority=kk % 2)

            @pl.when(i > 0)
            def _():
                wait_rows(1 - s)

            @pl.when(i == n_steps - 1)
            def _():
                wait_rows(s)


def _dispatch(zero_flag, dest_tiles, hp_p, hp_s, n_rows):
    w = hp_p.shape[1]
    tm = ROW_TILE
    n_p_tiles, n_s_tiles = hp_p.shape[0] // tm, hp_s.shape[0] // tm
    grid_spec = pltpu.PrefetchScalarGridSpec(
        num_scalar_prefetch=1,
        grid=(n_p_tiles + n_s_tiles,),
        in_specs=[pl.BlockSpec(memory_space=pl.ANY),
                  pl.BlockSpec((tm, w), lambda i, z: (jnp.minimum(i, n_p_tiles - 1), 0)),
                  pl.BlockSpec((tm, w), lambda i, z: (jnp.maximum(i - n_p_tiles, 0), 0))],
        out_specs=pl.BlockSpec(memory_space=pl.ANY),
        scratch_shapes=[pltpu.VMEM((2, tm, w), jnp.uint32), pltpu.VMEM((MOE_BLOCK, w), jnp.uint32),
                        pltpu.SMEM((2 * TOP_K * tm,), jnp.int32),
                        pltpu.SemaphoreType.DMA((2,)), pltpu.SemaphoreType.DMA((2,)), pltpu.SemaphoreType.DMA(())],
    )
    return pl.pallas_call(
        functools.partial(_dispatch_body, n_p_tiles),
        grid_spec=grid_spec,
        out_shape=jax.ShapeDtypeStruct((n_rows, w), jnp.uint32),
        compiler_params=_params(("arbitrary",)),
        name="moe_dispatch",
    )(zero_flag, dest_tiles, hp_p, hp_s)


def _experts_body(be_ref, na_ref, xs_ref, wu_ref, bu_ref, wd_ref, bd_ref, ys_ref, wu_bf, wd_bf):
    j = pl.program_id(0)
    d_ff = wd_ref.shape[0]
    changed = jnp.logical_or(j == 0, be_ref[j] != be_ref[jnp.maximum(j - 1, 0)])

    @pl.when(jnp.logical_and(j < na_ref[0], changed))
    def _():
        wu_bf[...] = wu_ref[...].astype(_BF16)
        wd_bf[...] = wd_ref[...].astype(_BF16)

    @pl.when(j < na_ref[0])
    def _():
        x = _unpack_bf16_pairs(xs_ref[...])
        h = jnp.dot(x, wu_bf[...], preferred_element_type=_F32) + bu_ref[...]
        g = jnp.minimum(h[:, :d_ff], SWIGLU_LIMIT)
        lin = jnp.clip(h[:, d_ff:], -SWIGLU_LIMIT, SWIGLU_LIMIT)
        act = g * jax.nn.sigmoid(SWIGLU_ALPHA * g) * (lin + 1.0)
        y = jnp.dot(act.astype(_BF16), wd_bf[...], preferred_element_type=_F32) + bd_ref[...]
        ys_ref[...] = _pack_bf16_pairs(y)

    @pl.when(j >= na_ref[0])
    def _():
        ys_ref[...] = jnp.zeros_like(ys_ref)


def _experts(block_e, n_active, xs, w_up, b_up, w_down, b_down):
    n_rows, w = xs.shape
    blk = MOE_BLOCK
    n_blocks = n_rows // blk
    d_model, d_ff2 = w_up.shape[1:]
    d_ff = w_down.shape[1]
    act_blk = lambda j, be, na: jnp.minimum(j, jnp.maximum(na[0] - 1, 0))
    grid_spec = pltpu.PrefetchScalarGridSpec(
        num_scalar_prefetch=2,
        grid=(n_blocks,),
        in_specs=[pl.BlockSpec((blk, w), lambda j, be, na: (act_blk(j, be, na), 0)),
                  pl.BlockSpec((None, d_model, d_ff2), lambda j, be, na: (be[j], 0, 0)),
                  pl.BlockSpec((None, 1, d_ff2), lambda j, be, na: (be[j], 0, 0)),
                  pl.BlockSpec((None, d_ff, d_model), lambda j, be, na: (be[j], 0, 0)),
                  pl.BlockSpec((None, 1, d_model), lambda j, be, na: (be[j], 0, 0))],
        out_specs=pl.BlockSpec((blk, d_model // 2), lambda j, be, na: (j, 0)),
        scratch_shapes=[pltpu.VMEM((d_model, d_ff2), _BF16), pltpu.VMEM((d_ff, d_model), _BF16)],
    )
    return pl.pallas_call(
        _experts_body,
        grid_spec=grid_spec,
        out_shape=jax.ShapeDtypeStruct((n_rows, d_model // 2), jnp.uint32),
        compiler_params=_params(("arbitrary",)),
        name="moe_experts",
    )(block_e, n_active, xs, w_up, b_up, w_down, b_down)


def _combine_body(alpha, tile0, dest_hbm, ys_hbm, h1_ref, gate_ref, g_ref, b_ref, o_ref,
                  dest_smem, buf, sem_idx, sem_rows):
    i = pl.program_id(0)
    n_steps = pl.num_programs(0)
    tm = o_ref.shape[0]
    group = 32

    def idx_copy(t, s):
        return pltpu.make_async_copy(dest_hbm.at[tile0 + t], dest_smem.at[pl.ds(s * TOP_K * tm, TOP_K * tm)],
                                     sem_idx.at[s])

    def gather_tile(s):
        for n in range(tm):
            for kk in range(TOP_K):
                row = dest_smem[(s * TOP_K + kk) * tm + n]
                pltpu.make_async_copy(ys_hbm.at[pl.ds(row, 1)], buf.at[s, kk, pl.ds(n, 1)],
                                      sem_rows.at[s]).start(priority=kk % 2)

    def wait_rows(s):
        for kk in range(TOP_K):
            pltpu.make_async_copy(ys_hbm.at[pl.ds(0, tm)], buf.at[s, kk], sem_rows.at[s]).wait()

    def reduce_rows(s, r0):
        rows = pl.ds(r0, group)
        lo = hi = None
        for kk in range(TOP_K):
            words = buf[s, kk, rows, :]
            gate = gate_ref[rows, kk:kk + 1]
            t_lo = gate * pltpu.bitcast(words << 16, _F32)
            t_hi = gate * pltpu.bitcast(words & jnp.uint32(0xFFFF0000), _F32)
            lo, hi = (t_lo, t_hi) if kk == 0 else (lo + t_lo, hi + t_hi)
        o_ref[rows, :] = _layer_norm(alpha * h1_ref[rows, :] + jnp.concatenate([lo, hi], axis=1), g_ref[...], b_ref[...])

    def row_groups(fn, unroll=1):
        def body(gi, carry):
            fn(pl.multiple_of(gi * group, group))
            return carry
        lax.fori_loop(0, tm // group, body, 0, unroll=unroll)

    @pl.when(i == 0)
    def _():
        idx_copy(i, 0).start()
        idx_copy(i, 0).wait()
        gather_tile(0)

        @pl.when(n_steps > 1)
        def _():
            idx_copy(i + 1, 1).start()

    for s in range(2):
        @pl.when(i % 2 == s)
        def _(s=s):
            @pl.when(i + 1 < n_steps)
            def _():
                idx_copy(i + 1, 1 - s).wait()
                gather_tile(1 - s)

            @pl.when(i + 2 < n_steps)
            def _():
                idx_copy(i + 2, s).start()

            wait_rows(s)
            row_groups(lambda r0: reduce_rows(s, r0), unroll=tm // group)


def _combine(alpha, tile0, dest_tiles, ys, h1, gate_rows, ln_g, ln_b):
    n_tok, d_model = h1.shape
    tm = ROW_TILE
    const2 = lambda i: (0, 0)
    return pl.pallas_call(
        functools.partial(_combine_body, alpha, tile0),
        grid=(n_tok // tm,),
        in_specs=[pl.BlockSpec(memory_space=pl.ANY), pl.BlockSpec(memory_space=pl.ANY),
                  pl.BlockSpec((tm, d_model), lambda i: (i, 0)),
                  pl.BlockSpec((tm, TOP_K), lambda i: (tile0 + i, 0)),
                  pl.BlockSpec(ln_g.shape, const2), pl.BlockSpec(ln_b.shape, const2)],
        out_specs=pl.BlockSpec((tm, d_model), lambda i: (i, 0)),
        out_shape=jax.ShapeDtypeStruct(h1.shape, _F32),
        scratch_shapes=[pltpu.SMEM((2 * TOP_K * tm,), jnp.int32), pltpu.VMEM((2, TOP_K, tm, ys.shape[1]), ys.dtype),
                        pltpu.SemaphoreType.DMA((2,)), pltpu.SemaphoreType.DMA((2,))],
        compiler_params=_params(("arbitrary",)),
        name="moe_combine_tile%d" % tile0,
    )(dest_tiles, ys, h1, gate_rows, ln_g, ln_b)


def _moe_layout(idx_all, rank_all, counts, n_total):
    blk = MOE_BLOCK
    n_blocks = (TOP_K * n_total + N_EXPERTS * (blk - 1)) // blk + 1
    cnt = counts.reshape(N_EXPERTS).astype(jnp.int32)
    padded = (cnt + blk - 1) // blk * blk
    pend = jnp.cumsum(padded)
    pstart = pend - padded
    experts = jnp.arange(N_EXPERTS, dtype=jnp.int32)
    dest = rank_all + jnp.sum(jnp.where(idx_all[None] == experts[:, None, None], pstart[:, None, None], 0), axis=0)
    block_row0 = jnp.arange(n_blocks, dtype=jnp.int32) * blk
    block_e = jnp.minimum(jnp.sum((pend[None, :] <= block_row0[:, None]).astype(jnp.int32), axis=1), N_EXPERTS - 1)
    n_active = (pend[-1] // blk).astype(jnp.int32).reshape(1)
    last_e = block_e[jnp.maximum(n_active[0] - 1, 0)]
    block_e = jnp.where(jnp.arange(n_blocks) < n_active[0], block_e, last_e)
    tm = ROW_TILE
    dest_tiles = dest.reshape(TOP_K, n_total // tm, tm).transpose(1, 0, 2).reshape(n_total // tm, TOP_K * tm)
    blocks = jnp.arange(n_blocks, dtype=jnp.int32)
    is_last = jnp.any((pend[None, :] == block_row0[:, None] + blk) & (cnt[None, :] > 0), axis=1)
    zero_flag = (is_last | (blocks >= n_active[0])).astype(jnp.int32)
    return dest_tiles, block_e, n_active, zero_flag, n_blocks * blk


def kernel(x_prompt, x_sample, cache_attn_k, cache_attn_v, state_pool, w_in, w_out, pool_w, pool_scale, ln1_g, ln1_b,
           w_router, b_router, w_up, b_up, w_down, b_down, ln2_g, ln2_b):
    depth = w_in.shape[0]
    assert depth == 1, "single-layer step"
    B, S, D = x_prompt.shape
    Bd, T, _ = x_sample.shape
    Lb = cache_attn_k.shape[2]
    assert S % ATTN_TILE == 0 and Lb == WINDOW_MAX and (B * S) % ROW_TILE == 0 and (Bd * T) % ROW_TILE == 0
    alpha = (2 * depth) ** 0.25
    n_p, n_s = B * S, Bd * T
    n_total = n_p + n_s

    w_in_b = w_in[0].astype(_BF16)
    w_out_b = w_out[0].astype(_BF16)
    pool_w_b = pool_w[0].astype(_BF16)
    wr_t = w_router[0].T
    br = b_router[0].reshape(N_EXPERTS, 1)

    q, k, v, u, kt, vt = _inproj_prompt(x_prompt, w_in_b)
    attn = _attn_prompt(q, k, v, _attn_bias_tables())
    h1_p, hp_p, idx_p, gate_p, rank_p, cnt_p = _mix_prompt(
        alpha, x_prompt, attn, u, w_out_b, pool_w_b, pool_scale, ln1_g, ln1_b, wr_t, br)
    n_keep = kt.shape[-1]
    k_prompt = kt.reshape(1, B, N_HEADS, HEAD_DIM, n_keep).transpose(0, 1, 4, 2, 3)
    v_prompt = vt.reshape(1, B, N_HEADS, HEAD_DIM, n_keep).transpose(0, 1, 4, 2, 3)
    pool_prompt = u[:, S - POOL_STATE_LEN:][None]

    hs = _inproj_sample(x_sample.reshape(n_s, D), w_in_b)
    qs = hs[:, :ATTN_WIDTH].reshape(Bd, T, ATTN_WIDTH)
    ks = hs[:, ATTN_WIDTH:2 * ATTN_WIDTH].reshape(Bd, T, ATTN_WIDTH)
    vs = hs[:, 2 * ATTN_WIDTH:3 * ATTN_WIDTH].reshape(Bd, T, ATTN_WIDTH)
    us = hs[:, 3 * ATTN_WIDTH:].reshape(Bd, T, POOL_WIDTH)
    to_t = lambda a: a.reshape(Bd, T, N_HEADS, HEAD_DIM).transpose(0, 2, 3, 1)
    kc = cache_attn_k[0].transpose(0, 2, 3, 1)
    vc = cache_attn_v[0].transpose(0, 2, 3, 1)
    attn_s, k_new, v_new = _attn_sample(qs, ks, vs, to_t(ks), to_t(vs), kc, vc, _decode_tables(T, Lb))
    k_sample = k_new.transpose(0, 3, 1, 2)[None]
    v_sample = v_new.transpose(0, 3, 1, 2)[None]
    uext_tm = jnp.concatenate([state_pool[0].transpose(1, 0, 2), us.transpose(1, 0, 2)], axis=0)
    pool_sample = uext_tm[T:].transpose(1, 0, 2)[None]
    x_tm = x_sample.transpose(1, 0, 2).reshape(n_s, D)
    attn_tm = attn_s.transpose(1, 0, 2).reshape(n_s, ATTN_WIDTH)
    h1_s, hp_s, idx_s, gate_s, rank_s, counts = _mix_sample(
        alpha, T, x_tm, attn_tm, uext_tm, w_out_b, pool_w_b, pool_scale, ln1_g, ln1_b, wr_t, br, cnt_p)

    idx_all = jnp.concatenate([idx_p, idx_s], axis=1)
    rank_all = jnp.concatenate([rank_p, rank_s], axis=1)
    gate_rows = jnp.concatenate([gate_p, gate_s], axis=1).T
    dest_tiles, block_e, n_active, zero_flag, n_rows = _moe_layout(idx_all, rank_all, counts, n_total)
    xs = _dispatch(zero_flag, dest_tiles, hp_p, hp_s, n_rows)
    ys = _experts(block_e, n_active, xs, w_up[0], b_up[0][:, None, :], w_down[0], b_down[0][:, None, :])
    out_p = _combine(alpha, 0, dest_tiles, ys, h1_p, gate_rows, ln2_g, ln2_b)
    out_s = _combine(alpha, n_p // ROW_TILE, dest_tiles, ys, h1_s, gate_rows, ln2_g, ln2_b)

    y_prompt = out_p.reshape(B, S, D)
    y_sample = out_s.reshape(T, Bd, D).transpose(1, 0, 2)
    return (y_prompt, y_sample, k_prompt, v_prompt, pool_prompt, k_sample, v_sample, pool_sample)
```

```python
import functools

import jax
import jax.numpy as jnp
import numpy as np
from jax import lax
from jax.experimental import pallas as pl
from jax.experimental.pallas import tpu as pltpu

HEAD_DIM = 64
N_HEADS = 8
ATTN_WIDTH = N_HEADS * HEAD_DIM
DILATIONS = (1, 4, 16)
WINDOWS = (128, 512, 2048)
N_BACK = 128
WINDOW_MAX = 2048
ATTN_BLOCK = 128
POOL_SIZES = (2, 4, 8, 16)
POOL_GROUP_DIM = 128
POOL_WIDTH = 512
POOL_STATE_LEN = 15
N_EXPERTS = 32
TOP_K = 4
SWIGLU_ALPHA = 1.702
SWIGLU_LIMIT = 7.0
LN_EPS = 1e-5
NEG_INF = -1e30

LANES = 128
HEADS_PER_CHUNK = LANES // HEAD_DIM
N_CHUNKS = ATTN_WIDTH // LANES
VMEM_LIMIT = 56 * 1024 * 1024

ATTN_TILE = 2048
BLOCK_UNROLL = 16
PROJ_TILE = 512
MOE_BLOCK = 512
EXPERT_ROW_STEPS = 4
ROW_TILE = 128

_F32 = jnp.float32
_BF16 = jnp.bfloat16


def _alibi_slopes():
    return 2.0 ** (-8.0 * np.arange(1, N_HEADS + 1, dtype=np.float64) / N_HEADS)


def _params(sem, **kw):
    return pltpu.CompilerParams(dimension_semantics=sem, vmem_limit_bytes=VMEM_LIMIT, **kw)


def _inproj_body(n_keep_tiles, x_ref, w_ref, q_ref, k_ref, v_ref, u_ref, kt_ref, vt_ref):
    i = pl.program_id(1)
    n_tiles = pl.num_programs(1)
    h = jnp.dot(x_ref[...].astype(_BF16), w_ref[...], preferred_element_type=_F32)
    for j in range(N_CHUNKS):
        q_ref[j] = h[:, j * LANES:(j + 1) * LANES]
        k_ref[j] = h[:, ATTN_WIDTH + j * LANES:ATTN_WIDTH + (j + 1) * LANES]
        v_ref[j] = h[:, 2 * ATTN_WIDTH + j * LANES:2 * ATTN_WIDTH + (j + 1) * LANES]
    u_ref[...] = h[:, 3 * ATTN_WIDTH:]

    @pl.when(i >= n_tiles - n_keep_tiles)
    def _():
        kt_ref[...] = h[:, ATTN_WIDTH:2 * ATTN_WIDTH].T
        vt_ref[...] = h[:, 2 * ATTN_WIDTH:3 * ATTN_WIDTH].T


def _inproj_prompt(x, w_in_bf16):
    B, S, D = x.shape
    tm = PROJ_TILE
    n_tiles = S // tm
    n_keep = min(WINDOW_MAX, S)
    n_keep_tiles = n_keep // tm
    first_keep = n_tiles - n_keep_tiles
    chunked = jax.ShapeDtypeStruct((B, N_CHUNKS, S, LANES), _F32)
    chunk_spec = pl.BlockSpec((None, N_CHUNKS, tm, LANES), lambda b, i: (b, 0, i, 0))
    t_spec = pl.BlockSpec((None, ATTN_WIDTH, tm), lambda b, i: (b, 0, jnp.maximum(i - first_keep, 0)))
    return pl.pallas_call(
        functools.partial(_inproj_body, n_keep_tiles),
        grid=(B, n_tiles),
        in_specs=[pl.BlockSpec((None, tm, D), lambda b, i: (b, i, 0)),
                  pl.BlockSpec(w_in_bf16.shape, lambda b, i: (0, 0))],
        out_specs=[chunk_spec, chunk_spec, chunk_spec,
                   pl.BlockSpec((None, tm, POOL_WIDTH), lambda b, i: (b, i, 0)),
                   t_spec, t_spec],
        out_shape=[chunked, chunked, chunked,
                   jax.ShapeDtypeStruct((B, S, POOL_WIDTH), _F32),
                   jax.ShapeDtypeStruct((B, ATTN_WIDTH, n_keep), _F32),
                   jax.ShapeDtypeStruct((B, ATTN_WIDTH, n_keep), _F32)],
        compiler_params=_params(("arbitrary", "arbitrary")),
        name="inproj_prompt",
    )(x, w_in_bf16)


def _attn_bias_tables():
    qi = np.arange(ATTN_BLOCK)[:, None]
    kj = np.arange(2 * ATTN_BLOCK)[None, :]
    step = qi - kj + ATTN_BLOCK
    valid = (step >= 0) & (step <= N_BACK)
    slopes = _alibi_slopes()
    out = np.zeros((N_CHUNKS, len(DILATIONS), HEADS_PER_CHUNK * ATTN_BLOCK, 2 * ATTN_BLOCK), np.float32)
    for c in range(N_CHUNKS):
        for p, dil in enumerate(DILATIONS):
            for hh in range(HEADS_PER_CHUNK):
                bias = -slopes[c * HEADS_PER_CHUNK + hh] * (step * dil).astype(np.float64)
                out[c, p, hh * ATTN_BLOCK:(hh + 1) * ATTN_BLOCK] = np.where(valid, bias, NEG_INF)
    return jnp.asarray(out)


def _attn_body(q_ref, k_ref, v_ref, bias_ref, o_ref,
               qf1, kf1, vf1, qf4, kf4, vf4, qf16, kf16, vf16, acc_n, acc_m, acc_d):
    i = pl.program_id(2)
    T = ATTN_TILE
    folded = ((1, qf1, kf1, vf1), (4, qf4, kf4, vf4), (16, qf16, kf16, vf16))
    scale = HEAD_DIM ** -0.5

    for dil, _, kf, vf in folded:
        L = T // dil

        @pl.when(i == 0)
        def _():
            kf[:, 0:ATTN_BLOCK, :] = jnp.zeros((dil, ATTN_BLOCK, LANES), _BF16)
            vf[:, 0:ATTN_BLOCK, :] = jnp.zeros((dil, ATTN_BLOCK, LANES), _BF16)

        @pl.when(i > 0)
        def _():
            kf[:, 0:ATTN_BLOCK, :] = kf[:, L:L + ATTN_BLOCK, :]
            vf[:, 0:ATTN_BLOCK, :] = vf[:, L:L + ATTN_BLOCK, :]

    for dil, qf, kf, vf in folded:
        L = T // dil
        for r in range(dil):
            rows = pl.ds(r, L, stride=dil) if dil > 1 else pl.ds(0, L)
            qf[r] = (q_ref[rows, :] * scale).astype(_BF16)
            kf[r, ATTN_BLOCK:ATTN_BLOCK + L, :] = k_ref[rows, :].astype(_BF16)
            vf[r, ATTN_BLOCK:ATTN_BLOCK + L, :] = v_ref[rows, :].astype(_BF16)

    lane = lax.broadcasted_iota(jnp.int32, (ATTN_BLOCK, LANES), 1)
    first_head = lane < HEAD_DIM
    col = lax.broadcasted_iota(jnp.int32, (HEADS_PER_CHUNK * ATTN_BLOCK, 2 * ATTN_BLOCK), 1)
    prev_half = col < ATTN_BLOCK

    for p, (dil, qf, kf, vf) in enumerate(folded):
        blocks_per_res = T // dil // ATTN_BLOCK

        def block(blk, carry, p=p, dil=dil, qf=qf, kf=kf, vf=vf, blocks_per_res=blocks_per_res):
            r = blk // blocks_per_res
            c = blk % blocks_per_res
            row0 = pl.multiple_of(c * ATTN_BLOCK, ATTN_BLOCK)
            q = qf[r, pl.ds(row0, ATTN_BLOCK), :]
            kk = kf[r, pl.ds(row0, 2 * ATTN_BLOCK), :]
            vv = vf[r, pl.ds(row0, 2 * ATTN_BLOCK), :]
            zero = jnp.zeros_like(q)
            qm = jnp.concatenate([jnp.where(first_head, q, zero), jnp.where(first_head, zero, q)], axis=0)
            s = lax.dot_general(qm, kk, (((1,), (1,)), ((), ())), preferred_element_type=_F32)
            s = s + bias_ref[p]
            s = jnp.where(jnp.logical_and(prev_half, jnp.logical_and(i == 0, c == 0)), NEG_INF, s)
            m = jnp.max(s, axis=-1, keepdims=True)
            e = jnp.exp(s - m)
            den = jnp.sum(e, axis=-1, keepdims=True)
            pv = jnp.dot(e.astype(_BF16), vv, preferred_element_type=_F32)
            num = jnp.where(first_head, pv[:ATTN_BLOCK], pv[ATTN_BLOCK:])
            m2 = jnp.where(first_head, m[:ATTN_BLOCK], m[ATTN_BLOCK:])
            d2 = jnp.where(first_head, den[:ATTN_BLOCK], den[ATTN_BLOCK:])
            start = r + row0 * dil
            rows = pl.ds(start, ATTN_BLOCK, stride=dil) if dil > 1 else pl.ds(start, ATTN_BLOCK)
            acc_n[p, rows, :] = num
            acc_m[p, rows, :] = m2
            acc_d[p, rows, :] = d2
            return carry

        lax.fori_loop(0, T // ATTN_BLOCK, block, 0, unroll=BLOCK_UNROLL)

    chunk_rows = 256

    def merge(t, carry):
        rows = pl.ds(pl.multiple_of(t * chunk_rows, chunk_rows), chunk_rows)
        m0, m1, m2 = acc_m[0, rows, :], acc_m[1, rows, :], acc_m[2, rows, :]
        mx = jnp.maximum(jnp.maximum(m0, m1), m2)
        e0, e1, e2 = jnp.exp(m0 - mx), jnp.exp(m1 - mx), jnp.exp(m2 - mx)
        num = acc_n[0, rows, :] * e0 + acc_n[1, rows, :] * e1 + acc_n[2, rows, :] * e2
        den = acc_d[0, rows, :] * e0 + acc_d[1, rows, :] * e1 + acc_d[2, rows, :] * e2
        o_ref[rows, :] = (num / den).astype(o_ref.dtype)
        return carry

    lax.fori_loop(0, T // chunk_rows, merge, 0)


def _attn_prompt(q, k, v, bias):
    B, _, S, _ = q.shape
    T = ATTN_TILE
    io_spec = pl.BlockSpec((None, None, T, LANES), lambda b, c, i: (b, c, i, 0))
    scratch = []
    for dil in DILATIONS:
        L = T // dil
        scratch += [pltpu.VMEM((dil, L, LANES), _BF16),
                    pltpu.VMEM((dil, ATTN_BLOCK + L, LANES), _BF16),
                    pltpu.VMEM((dil, ATTN_BLOCK + L, LANES), _BF16)]
    scratch += [pltpu.VMEM((len(DILATIONS), T, LANES), _F32)] * 3
    return pl.pallas_call(
        _attn_body,
        grid=(B, N_CHUNKS, S // T),
        in_specs=[io_spec, io_spec, io_spec,
                  pl.BlockSpec((None,) + bias.shape[1:], lambda b, c, i: (c, 0, 0, 0))],
        out_specs=io_spec,
        out_shape=jax.ShapeDtypeStruct((B, N_CHUNKS, S, LANES), _BF16),
        scratch_shapes=scratch,
        compiler_params=_params(("arbitrary", "arbitrary", "arbitrary")),
        name="attn_prompt",
    )(q, k, v, bias)


def _layer_norm(z, g, b):
    mu = jnp.mean(z, axis=-1, keepdims=True)
    zc = z - mu
    var = jnp.mean(zc * zc, axis=-1, keepdims=True)
    return zc * lax.rsqrt(var + LN_EPS) * g + b


def _pack_bf16_pairs(h):
    w = h.shape[1] // 2
    bits = pltpu.bitcast(h.astype(_BF16).astype(_F32), jnp.uint32)
    return (bits[:, w:] & jnp.uint32(0xFFFF0000)) | (bits[:, :w] >> 16)


def _unpack_bf16_pairs(words):
    lo = pltpu.bitcast(words << 16, _F32)
    hi = pltpu.bitcast(words & jnp.uint32(0xFFFF0000), _F32)
    return jnp.concatenate([lo, hi], axis=1).astype(_BF16)


def _route(h1, wr_ref, br_ref, cnt_ref, idx_ref, gate_ref, rank_ref):
    tm = h1.shape[0]
    nt = lambda a, b: lax.dot_general(a, b, (((1,), (1,)), ((), ())), preferred_element_type=_F32)
    h_hi = h1.astype(_BF16)
    h_lo = (h1 - h_hi.astype(_F32)).astype(_BF16)
    by_hi, by_lo = nt(wr_ref[...], h_hi), nt(wr_ref[...], h_lo)
    logits = (by_hi[:N_EXPERTS] + by_hi[N_EXPERTS:]) + (by_lo[:N_EXPERTS] + by_lo[N_EXPERTS:]) + br_ref[...]
    eid = lax.broadcasted_iota(jnp.int32, (N_EXPERTS, tm), 0)
    work = logits
    vals, ids, hots = [], [], []
    for _ in range(TOP_K):
        mx = jnp.max(work, axis=0, keepdims=True)
        sel = jnp.min(jnp.where(work == mx, eid, N_EXPERTS), axis=0, keepdims=True)
        hot = eid == sel
        vals.append(mx)
        ids.append(sel)
        hots.append(hot)
        work = jnp.where(hot, -jnp.inf, work)
    ex = [jnp.exp(v - vals[0]) for v in vals]
    tot = ex[0] + ex[1] + ex[2] + ex[3]
    any_hot = jnp.where(hots[0] | hots[1] | hots[2] | hots[3], 1.0, 0.0)
    earlier = (lax.broadcasted_iota(jnp.int32, (tm, tm), 0) < lax.broadcasted_iota(jnp.int32, (tm, tm), 1))
    prefix = jnp.dot(any_hot.astype(_BF16), jnp.where(earlier, 1.0, 0.0).astype(_BF16),
                     preferred_element_type=_F32) + cnt_ref[...]
    for kk in range(TOP_K):
        idx_ref[kk:kk + 1, :] = ids[kk]
        gate_ref[kk:kk + 1, :] = ex[kk] / tot
        rank_ref[kk:kk + 1, :] = jnp.sum(jnp.where(hots[kk], prefix, 0.0), axis=0, keepdims=True).astype(jnp.int32)
    cnt_ref[...] = cnt_ref[...] + jnp.sum(any_hot, axis=1, keepdims=True)


def _pool_mix(win_fn, u_tile, cnt_fn, pw_ref, ps_ref):
    outs = []
    for g, size in enumerate(POOL_SIZES):
        lanes = slice(g * POOL_GROUP_DIM, (g + 1) * POOL_GROUP_DIM)
        win = win_fn(0, lanes)
        for j in range(1, size):
            win = win + win_fn(j, lanes)
        d = win / cnt_fn(size) - u_tile[:, lanes]
        y = jnp.dot(d.astype(_BF16), pw_ref[g], preferred_element_type=_F32)
        outs.append(y * ps_ref[:, lanes])
    return jnp.concatenate(outs, axis=1)


def _mix_prompt_body(alpha, x_ref, a_ref, u_ref, wo_ref, pw_ref, ps_ref, g_ref, b_ref, wr_ref, br_ref,
                     h1_ref, hp_ref, idx_ref, gate_ref, rank_ref, cnt_out_ref, uext, cnt_ref):
    b = pl.program_id(0)
    i = pl.program_id(1)
    tm = x_ref.shape[0]
    halo = 16

    @pl.when(jnp.logical_and(b == 0, i == 0))
    def _():
        cnt_ref[...] = jnp.zeros_like(cnt_ref)

    @pl.when(i == 0)
    def _():
        uext[0:halo, :] = jnp.zeros((halo, POOL_WIDTH), _F32)

    @pl.when(i > 0)
    def _():
        uext[0:halo, :] = uext[tm:tm + halo, :]

    uext[halo:halo + tm, :] = u_ref[...]
    pos = i * tm + lax.broadcasted_iota(jnp.int32, (tm, 1), 0)
    pool = _pool_mix(lambda j, lanes: uext[halo - j:halo - j + tm, lanes], u_ref,
                     lambda size: jnp.minimum(pos + 1, size).astype(_F32), pw_ref, ps_ref)
    cat = jnp.concatenate([a_ref[j] for j in range(N_CHUNKS)] + [pool.astype(_BF16)], axis=1)
    mix = jnp.dot(cat, wo_ref[...], preferred_element_type=_F32)
    h1 = _layer_norm(alpha * x_ref[...] + mix, g_ref[...], b_ref[...])
    h1_ref[...] = h1
    hp_ref[...] = _pack_bf16_pairs(h1)
    _route(h1, wr_ref, br_ref, cnt_ref, idx_ref, gate_ref, rank_ref)
    cnt_out_ref[...] = cnt_ref[...]


def _mix_prompt(alpha, x, attn, u, w_out_bf16, pool_w_bf16, pool_scale, ln_g, ln_b, wr_t, br):
    B, S, D = x.shape
    n_total = B * S
    tm = PROJ_TILE
    n_tiles = S // tm
    tok = lambda b, i: (b * n_tiles + i, 0)
    tok_t = lambda b, i: (0, b * n_tiles + i)
    const2 = lambda b, i: (0, 0)
    return pl.pallas_call(
        functools.partial(_mix_prompt_body, alpha),
        grid=(B, n_tiles),
        in_specs=[pl.BlockSpec((None, tm, D), lambda b, i: (b, i, 0)),
                  pl.BlockSpec((None, N_CHUNKS, tm, LANES), lambda b, i: (b, 0, i, 0)),
                  pl.BlockSpec((None, tm, POOL_WIDTH), lambda b, i: (b, i, 0)),
                  pl.BlockSpec(w_out_bf16.shape, const2),
                  pl.BlockSpec(pool_w_bf16.shape, lambda b, i: (0, 0, 0)),
                  pl.BlockSpec(pool_scale.shape, const2),
                  pl.BlockSpec(ln_g.shape, const2), pl.BlockSpec(ln_b.shape, const2),
                  pl.BlockSpec(wr_t.shape, const2), pl.BlockSpec(br.shape, const2)],
        out_specs=[pl.BlockSpec((tm, D), tok), pl.BlockSpec((tm, D // 2), tok),
                   pl.BlockSpec((TOP_K, tm), tok_t), pl.BlockSpec((TOP_K, tm), tok_t),
                   pl.BlockSpec((TOP_K, tm), tok_t), pl.BlockSpec((N_EXPERTS, 1), const2)],
        out_shape=[jax.ShapeDtypeStruct((n_total, D), _F32), jax.ShapeDtypeStruct((n_total, D // 2), jnp.uint32),
                   jax.ShapeDtypeStruct((TOP_K, n_total), jnp.int32), jax.ShapeDtypeStruct((TOP_K, n_total), _F32),
                   jax.ShapeDtypeStruct((TOP_K, n_total), jnp.int32), jax.ShapeDtypeStruct((N_EXPERTS, 1), _F32)],
        scratch_shapes=[pltpu.VMEM((16 + tm, POOL_WIDTH), _F32), pltpu.VMEM((N_EXPERTS, 1), _F32)],
        compiler_params=_params(("arbitrary", "arbitrary")),
        name="mix_prompt",
    )(x, attn, u, w_out_bf16, pool_w_bf16, pool_scale, ln_g, ln_b, wr_t, br)


def _inproj_sample_body(x_ref, w_ref, o_ref):
    o_ref[...] = jnp.dot(x_ref[...].astype(_BF16), w_ref[...], preferred_element_type=_F32)


def _inproj_sample(x2d, w_in_bf16):
    return pl.pallas_call(
        _inproj_sample_body,
        out_shape=jax.ShapeDtypeStruct((x2d.shape[0], w_in_bf16.shape[1]), _F32),
        compiler_params=_params(None),
        name="inproj_sample",
    )(x2d, w_in_bf16)


def _decode_tables(T, Lb):
    slopes = _alibi_slopes()

    def mult(delta):
        m = np.zeros(delta.shape, np.float64)
        for win, dil in zip(WINDOWS, DILATIONS):
            m += ((delta % dil == 0) & (delta // dil <= win // dil) & (delta >= 0))
        return m

    t = np.arange(T)[:, None]
    d_cache = Lb + t - np.arange(Lb)[None, :]
    d_new = t - np.arange(T)[None, :]
    m_cache, m_new = mult(d_cache), mult(d_new)
    b_cache = np.where(m_cache > 0, -slopes[:, None, None] * d_cache[None], NEG_INF)
    b_new = np.where(m_new > 0, -slopes[:, None, None] * d_new[None], NEG_INF)
    f = lambda a: jnp.asarray(a.astype(np.float32))
    return f(b_cache), f(m_cache), f(b_new), f(m_new)


def _attn_sample_body(q_ref, kn_ref, vn_ref, knt_ref, vnt_ref, kc_ref, vc_ref, bc_ref, mc_ref, bn_ref, mn_ref,
                      o_ref, ko_ref, vo_ref):
    T = q_ref.shape[0]
    scale = HEAD_DIM ** -0.5
    rnd = lambda a: a.astype(_BF16).astype(_F32)
    new_col = lax.broadcasted_iota(jnp.int32, (T, T), 1)
    outs = []
    for h in range(N_HEADS):
        lanes = slice(h * HEAD_DIM, (h + 1) * HEAD_DIM)
        q = rnd(q_ref[:, lanes] * scale)
        kn = rnd(kn_ref[:, lanes])
        vn = rnd(vn_ref[:, lanes])
        s_c = jnp.dot(q.astype(_BF16), kc_ref[h].astype(_BF16), preferred_element_type=_F32) + bc_ref[h]
        s_n = bn_ref[h]
        for t in range(T):
            s_n = s_n + jnp.where(new_col == t, jnp.sum(q * kn[t:t + 1, :], axis=-1, keepdims=True), 0.0)
        m = jnp.maximum(jnp.max(s_c, axis=-1, keepdims=True), jnp.max(s_n, axis=-1, keepdims=True))
        e_c = mc_ref[...] * jnp.exp(s_c - m)
        e_n = mn_ref[...] * jnp.exp(s_n - m)
        den = jnp.sum(e_c, axis=-1, keepdims=True) + jnp.sum(e_n, axis=-1, keepdims=True)
        num = lax.dot_general(e_c.astype(_BF16), vc_ref[h].astype(_BF16), (((1,), (1,)), ((), ())),
                              preferred_element_type=_F32)
        e_nr = rnd(e_n)
        for t in range(T):
            num = num + e_nr[:, t:t + 1] * vn[t:t + 1, :]
        outs.append(num / den)
    o_ref[...] = jnp.concatenate(outs, axis=1)
    ko_ref[...] = jnp.concatenate([kc_ref[...][:, :, T:], knt_ref[...]], axis=-1)
    vo_ref[...] = jnp.concatenate([vc_ref[...][:, :, T:], vnt_ref[...]], axis=-1)


def _attn_sample(q, kn, vn, knt, vnt, kc, vc, tables):
    Bd, T, _ = q.shape
    Lb = kc.shape[-1]
    tok = pl.BlockSpec((None, T, ATTN_WIDTH), lambda b: (b, 0, 0))
    newt = pl.BlockSpec((None, N_HEADS, HEAD_DIM, T), lambda b: (b, 0, 0, 0))
    cache = pl.BlockSpec((None, N_HEADS, HEAD_DIM, Lb), lambda b: (b, 0, 0, 0))
    tabs = [pl.BlockSpec(t.shape, (lambda b, n=t.ndim: (0,) * n)) for t in tables]
    return pl.pallas_call(
        _attn_sample_body,
        grid=(Bd,),
        in_specs=[tok, tok, tok, newt, newt, cache, cache] + tabs,
        out_specs=[tok, cache, cache],
        out_shape=[jax.ShapeDtypeStruct((Bd, T, ATTN_WIDTH), _F32),
                   jax.ShapeDtypeStruct(kc.shape, _F32), jax.ShapeDtypeStruct(vc.shape, _F32)],
        compiler_params=_params(("arbitrary",)),
        name="attn_sample",
    )(q, kn, vn, knt, vnt, kc, vc, *tables)


def _mix_sample_body(alpha, T, x_ref, a_ref, ue_ref, wo_ref, pw_ref, ps_ref, g_ref, b_ref, wr_ref, br_ref, cnt_in_ref,
                     h1_ref, hp_ref, idx_ref, gate_ref, rank_ref, cnt_out_ref, cnt_ref):
    cnt_ref[...] = cnt_in_ref[...]
    hist = POOL_STATE_LEN
    pools = []
    for t in range(T):
        pools.append(_pool_mix(lambda j, lanes, t=t: ue_ref[hist + t - j, :, lanes], ue_ref[hist + t],
                               lambda size: float(size), pw_ref, ps_ref))
    pool = jnp.concatenate(pools, axis=0)
    cat = jnp.concatenate([a_ref[...].astype(_BF16), pool.astype(_BF16)], axis=1)
    mix = jnp.dot(cat, wo_ref[...], preferred_element_type=_F32)
    h1 = _layer_norm(alpha * x_ref[...] + mix, g_ref[...], b_ref[...])
    h1_ref[...] = h1
    hp_ref[...] = _pack_bf16_pairs(h1)
    _route(h1, wr_ref, br_ref, cnt_ref, idx_ref, gate_ref, rank_ref)
    cnt_out_ref[...] = cnt_ref[...]


def _mix_sample(alpha, T, x_tm, attn_tm, uext_tm, w_out_bf16, pool_w_bf16, pool_scale, ln_g, ln_b, wr_t, br, cnt_in):
    n_s, D = x_tm.shape
    full = lambda shape: pl.BlockSpec(shape, lambda i, n=len(shape): (0,) * n)
    ins = [x_tm, attn_tm, uext_tm, w_out_bf16, pool_w_bf16, pool_scale, ln_g, ln_b, wr_t, br, cnt_in]
    outs = [jax.ShapeDtypeStruct((n_s, D), _F32), jax.ShapeDtypeStruct((n_s, D // 2), jnp.uint32),
            jax.ShapeDtypeStruct((TOP_K, n_s), jnp.int32), jax.ShapeDtypeStruct((TOP_K, n_s), _F32),
            jax.ShapeDtypeStruct((TOP_K, n_s), jnp.int32), jax.ShapeDtypeStruct((N_EXPERTS, 1), _F32)]
    return pl.pallas_call(
        functools.partial(_mix_sample_body, alpha, T),
        grid=(1,),
        in_specs=[full(a.shape) for a in ins],
        out_specs=[full(o.shape) for o in outs],
        out_shape=outs,
        scratch_shapes=[pltpu.VMEM((N_EXPERTS, 1), _F32)],
        compiler_params=_params(("arbitrary",)),
        name="mix_sample",
    )(*ins)


def _dispatch_body(n_p_tiles, zflag_ref, dest_hbm, hp_p_ref, hp_s_ref, xs_hbm, tile, zeros, dest_smem,
                   sem_idx, sem_rows, sem_zero):
    i = pl.program_id(0)
    n_steps = pl.num_programs(0)
    tm = tile.shape[1]

    @pl.when(i == 0)
    def _():
        zeros[...] = jnp.zeros_like(zeros)

        blk = zeros.shape[0]

        def zero_copy(j):
            return pltpu.make_async_copy(zeros, xs_hbm.at[pl.ds(pl.multiple_of(j * blk, blk), blk)], sem_zero)

        def start(j, carry):
            @pl.when(zflag_ref[j] != 0)
            def _():
                zero_copy(j).start()
            return carry

        def wait(j, carry):
            @pl.when(zflag_ref[j] != 0)
            def _():
                zero_copy(j).wait()
            return carry

        lax.fori_loop(0, zflag_ref.shape[0], start, 0)
        lax.fori_loop(0, zflag_ref.shape[0], wait, 0)

    def idx_copy(t, s):
        return pltpu.make_async_copy(dest_hbm.at[t], dest_smem.at[pl.ds(s * TOP_K * tm, TOP_K * tm)], sem_idx.at[s])

    def wait_rows(s):
        for _ in range(TOP_K):
            pltpu.make_async_copy(tile.at[s], xs_hbm.at[pl.ds(0, tm)], sem_rows.at[s]).wait()

    @pl.when(i == 0)
    def _():
        idx_copy(i, 0).start()

    staged = jnp.where(i < n_p_tiles, hp_p_ref[...], hp_s_ref[...])

    for s in range(2):
        @pl.when(i % 2 == s)
        def _(s=s):
            @pl.when(i + 1 < n_steps)
            def _():
                idx_copy(i + 1, 1 - s).start()

            tile[s] = staged
            idx_copy(i, s).wait()

            for n in range(tm):
                for kk in range(TOP_K):
                    pltpu.make_async_copy(tile.at[s, pl.ds(n, 1)], xs_hbm.at[pl.ds(dest_smem[(s * TOP_K + kk) * tm + n], 1)],
                                          sem_rows.at[s]).start(priority=kk % 2)

            @pl.when(i > 0)
            def _():
                wait_rows(1 - s)

            @pl.when(i == n_steps - 1)
            def _():
                wait_rows(s)


def _dispatch(zero_flag, dest_tiles, hp_p, hp_s, n_rows):
    w = hp_p.shape[1]
    tm = ROW_TILE
    n_p_tiles, n_s_tiles = hp_p.shape[0] // tm, hp_s.shape[0] // tm
    grid_spec = pltpu.PrefetchScalarGridSpec(
        num_scalar_prefetch=1,
        grid=(n_p_tiles + n_s_tiles,),
        in_specs=[pl.BlockSpec(memory_space=pl.ANY),
                  pl.BlockSpec((tm, w), lambda i, z: (jnp.minimum(i, n_p_tiles - 1), 0)),
                  pl.BlockSpec((tm, w), lambda i, z: (jnp.maximum(i - n_p_tiles, 0), 0))],
        out_specs=pl.BlockSpec(memory_space=pl.ANY),
        scratch_shapes=[pltpu.VMEM((2, tm, w), jnp.uint32), pltpu.VMEM((MOE_BLOCK, w), jnp.uint32),
                        pltpu.SMEM((2 * TOP_K * tm,), jnp.int32),
                        pltpu.SemaphoreType.DMA((2,)), pltpu.SemaphoreType.DMA((2,)), pltpu.SemaphoreType.DMA(())],
    )
    return pl.pallas_call(
        functools.partial(_dispatch_body, n_p_tiles),
        grid_spec=grid_spec,
        out_shape=jax.ShapeDtypeStruct((n_rows, w), jnp.uint32),
        compiler_params=_params(("arbitrary",)),
        name="moe_dispatch",
    )(zero_flag, dest_tiles, hp_p, hp_s)


def _experts_body(be_ref, na_ref, nv_ref, xs_ref, wu_ref, bu_ref, wd_ref, bd_ref, ys_ref, wu_bf, wd_bf):
    j = pl.program_id(0)
    blk = xs_ref.shape[0]
    d_ff = wd_ref.shape[0]
    active = j < na_ref[0]
    changed = jnp.logical_or(j == 0, be_ref[j] != be_ref[jnp.maximum(j - 1, 0)])

    @pl.when(jnp.logical_and(active, changed))
    def _():
        wu_bf[...] = wu_ref[...].astype(_BF16)
        wd_bf[...] = wd_ref[...].astype(_BF16)

    def ffn(rows):
        x = _unpack_bf16_pairs(xs_ref[0:rows, :])
        h = jnp.dot(x, wu_bf[...], preferred_element_type=_F32) + bu_ref[...]
        g = jnp.minimum(h[:, :d_ff], SWIGLU_LIMIT)
        lin = jnp.clip(h[:, d_ff:], -SWIGLU_LIMIT, SWIGLU_LIMIT)
        act = g * jax.nn.sigmoid(SWIGLU_ALPHA * g) * (lin + 1.0)
        y = jnp.dot(act.astype(_BF16), wd_bf[...], preferred_element_type=_F32) + bd_ref[...]
        ys_ref[0:rows, :] = _pack_bf16_pairs(y)
        if rows < blk:
            ys_ref[rows:, :] = jnp.zeros((blk - rows, ys_ref.shape[1]), ys_ref.dtype)

    quarter = blk // EXPERT_ROW_STEPS
    n_quarters = (nv_ref[j] + quarter - 1) // quarter
    for q in range(1, EXPERT_ROW_STEPS + 1):
        @pl.when(jnp.logical_and(active, n_quarters == q))
        def _(q=q):
            ffn(q * quarter)

    @pl.when(jnp.logical_or(jnp.logical_not(active), n_quarters == 0))
    def _():
        ys_ref[...] = jnp.zeros_like(ys_ref)


def _experts(block_e, n_active, n_valid, xs, w_up, b_up, w_down, b_down):
    n_rows, w = xs.shape
    blk = MOE_BLOCK
    n_blocks = n_rows // blk
    d_model, d_ff2 = w_up.shape[1:]
    d_ff = w_down.shape[1]
    act_blk = lambda j, be, na, nv: jnp.minimum(j, jnp.maximum(na[0] - 1, 0))
    grid_spec = pltpu.PrefetchScalarGridSpec(
        num_scalar_prefetch=3,
        grid=(n_blocks,),
        in_specs=[pl.BlockSpec((blk, w), lambda j, be, na, nv: (act_blk(j, be, na, nv), 0)),
                  pl.BlockSpec((None, d_model, d_ff2), lambda j, be, na, nv: (be[j], 0, 0)),
                  pl.BlockSpec((None, 1, d_ff2), lambda j, be, na, nv: (be[j], 0, 0)),
                  pl.BlockSpec((None, d_ff, d_model), lambda j, be, na, nv: (be[j], 0, 0)),
                  pl.BlockSpec((None, 1, d_model), lambda j, be, na, nv: (be[j], 0, 0))],
        out_specs=pl.BlockSpec((blk, d_model // 2), lambda j, be, na, nv: (j, 0)),
        scratch_shapes=[pltpu.VMEM((d_model, d_ff2), _BF16), pltpu.VMEM((d_ff, d_model), _BF16)],
    )
    return pl.pallas_call(
        _experts_body,
        grid_spec=grid_spec,
        out_shape=jax.ShapeDtypeStruct((n_rows, d_model // 2), jnp.uint32),
        compiler_params=_params(("arbitrary",)),
        name="moe_experts",
    )(block_e, n_active, n_valid, xs, w_up, b_up, w_down, b_down)


def _combine_body(alpha, tile0, dest_hbm, ys_hbm, h1_ref, gate_ref, g_ref, b_ref, o_ref,
                  dest_smem, buf, sem_idx, sem_rows):
    i = pl.program_id(0)
    n_steps = pl.num_programs(0)
    tm = o_ref.shape[0]
    group = 32

    def idx_copy(t, s):
        return pltpu.make_async_copy(dest_hbm.at[tile0 + t], dest_smem.at[pl.ds(s * TOP_K * tm, TOP_K * tm)],
                                     sem_idx.at[s])

    def gather_tile(s):
        for n in range(tm):
            for kk in range(TOP_K):
                row = dest_smem[(s * TOP_K + kk) * tm + n]
                pltpu.make_async_copy(ys_hbm.at[pl.ds(row, 1)], buf.at[s, kk, pl.ds(n, 1)],
                                      sem_rows.at[s]).start(priority=kk % 2)

    def wait_rows(s):
        for kk in range(TOP_K):
            pltpu.make_async_copy(ys_hbm.at[pl.ds(0, tm)], buf.at[s, kk], sem_rows.at[s]).wait()

    def reduce_rows(s, r0):
        rows = pl.ds(r0, group)
        lo = hi = None
        for kk in range(TOP_K):
            words = buf[s, kk, rows, :]
            gate = gate_ref[rows, kk:kk + 1]
            t_lo = gate * pltpu.bitcast(words << 16, _F32)
            t_hi = gate * pltpu.bitcast(words & jnp.uint32(0xFFFF0000), _F32)
            lo, hi = (t_lo, t_hi) if kk == 0 else (lo + t_lo, hi + t_hi)
        o_ref[rows, :] = _layer_norm(alpha * h1_ref[rows, :] + jnp.concatenate([lo, hi], axis=1), g_ref[...], b_ref[...])

    def row_groups(fn, unroll=1):
        def body(gi, carry):
            fn(pl.multiple_of(gi * group, group))
            return carry
        lax.fori_loop(0, tm // group, body, 0, unroll=unroll)

    @pl.when(i == 0)
    def _():
        idx_copy(i, 0).start()
        idx_copy(i, 0).wait()
        gather_tile(0)

        @pl.when(n_steps > 1)
        def _():
            idx_copy(i + 1, 1).start()

    for s in range(2):
        @pl.when(i % 2 == s)
        def _(s=s):
            @pl.when(i + 1 < n_steps)
            def _():
                idx_copy(i + 1, 1 - s).wait()
                gather_tile(1 - s)

            @pl.when(i + 2 < n_steps)
            def _():
                idx_copy(i + 2, s).start()

            wait_rows(s)
            row_groups(lambda r0: reduce_rows(s, r0), unroll=tm // group)


def _combine(alpha, tile0, dest_tiles, ys, h1, gate_rows, ln_g, ln_b):
    n_tok, d_model = h1.shape
    tm = ROW_TILE
    const2 = lambda i: (0, 0)
    return pl.pallas_call(
        functools.partial(_combine_body, alpha, tile0),
        grid=(n_tok // tm,),
        in_specs=[pl.BlockSpec(memory_space=pl.ANY), pl.BlockSpec(memory_space=pl.ANY),
                  pl.BlockSpec((tm, d_model), lambda i: (i, 0)),
                  pl.BlockSpec((tm, TOP_K), lambda i: (tile0 + i, 0)),
                  pl.BlockSpec(ln_g.shape, const2), pl.BlockSpec(ln_b.shape, const2)],
        out_specs=pl.BlockSpec((tm, d_model), lambda i: (i, 0)),
        out_shape=jax.ShapeDtypeStruct(h1.shape, _F32),
        scratch_shapes=[pltpu.SMEM((2 * TOP_K * tm,), jnp.int32), pltpu.VMEM((2, TOP_K, tm, ys.shape[1]), ys.dtype),
                        pltpu.SemaphoreType.DMA((2,)), pltpu.SemaphoreType.DMA((2,))],
        compiler_params=_params(("arbitrary",)),
        name="moe_combine_tile%d" % tile0,
    )(dest_tiles, ys, h1, gate_rows, ln_g, ln_b)


def _moe_layout(idx_all, rank_all, counts, n_total):
    blk = MOE_BLOCK
    n_blocks = (TOP_K * n_total + N_EXPERTS * (blk - 1)) // blk + 1
    cnt = counts.reshape(N_EXPERTS).astype(jnp.int32)
    padded = (cnt + blk - 1) // blk * blk
    pend = jnp.cumsum(padded)
    pstart = pend - padded
    experts = jnp.arange(N_EXPERTS, dtype=jnp.int32)
    dest = rank_all + jnp.sum(jnp.where(idx_all[None] == experts[:, None, None], pstart[:, None, None], 0), axis=0)
    block_row0 = jnp.arange(n_blocks, dtype=jnp.int32) * blk
    block_e = jnp.minimum(jnp.sum((pend[None, :] <= block_row0[:, None]).astype(jnp.int32), axis=1), N_EXPERTS - 1)
    n_active = (pend[-1] // blk).astype(jnp.int32).reshape(1)
    last_e = block_e[jnp.maximum(n_active[0] - 1, 0)]
    block_e = jnp.where(jnp.arange(n_blocks) < n_active[0], block_e, last_e)
    tm = ROW_TILE
    dest_tiles = dest.reshape(TOP_K, n_total // tm, tm).transpose(1, 0, 2).reshape(n_total // tm, TOP_K * tm)
    blocks = jnp.arange(n_blocks, dtype=jnp.int32)
    is_last = jnp.any((pend[None, :] == block_row0[:, None] + blk) & (cnt[None, :] > 0), axis=1)
    zero_flag = (is_last | (blocks >= n_active[0])).astype(jnp.int32)
    is_e = block_e[:, None] == experts[None, :]
    rows_left = jnp.sum(jnp.where(is_e, (pstart + cnt)[None, :], 0), axis=1) - block_row0
    n_valid = jnp.where(blocks < n_active[0], jnp.clip(rows_left, 0, blk), 0).astype(jnp.int32)
    return dest_tiles, block_e, n_active, n_valid, zero_flag, n_blocks * blk


def kernel(x_prompt, x_sample, cache_attn_k, cache_attn_v, state_pool, w_in, w_out, pool_w, pool_scale, ln1_g, ln1_b,
           w_router, b_router, w_up, b_up, w_down, b_down, ln2_g, ln2_b):
    depth = w_in.shape[0]
    assert depth == 1, "single-layer step"
    B, S, D = x_prompt.shape
    Bd, T, _ = x_sample.shape
    Lb = cache_attn_k.shape[2]
    assert S % ATTN_TILE == 0 and Lb == WINDOW_MAX and (B * S) % ROW_TILE == 0 and (Bd * T) % ROW_TILE == 0
    alpha = (2 * depth) ** 0.25
    n_p, n_s = B * S, Bd * T
    n_total = n_p + n_s

    w_in_b = w_in[0].astype(_BF16)
    w_out_b = w_out[0].astype(_BF16)
    pool_w_b = pool_w[0].astype(_BF16)
    wr_f32 = w_router[0].T
    wr_hi = wr_f32.astype(_BF16)
    wr_t = jnp.concatenate([wr_hi, (wr_f32 - wr_hi.astype(_F32)).astype(_BF16)], axis=0)
    br = b_router[0].reshape(N_EXPERTS, 1)

    q, k, v, u, kt, vt = _inproj_prompt(x_prompt, w_in_b)
    attn = _attn_prompt(q, k, v, _attn_bias_tables())
    h1_p, hp_p, idx_p, gate_p, rank_p, cnt_p = _mix_prompt(
        alpha, x_prompt, attn, u, w_out_b, pool_w_b, pool_scale, ln1_g, ln1_b, wr_t, br)
    n_keep = kt.shape[-1]
    k_prompt = kt.reshape(1, B, N_HEADS, HEAD_DIM, n_keep).transpose(0, 1, 4, 2, 3)
    v_prompt = vt.reshape(1, B, N_HEADS, HEAD_DIM, n_keep).transpose(0, 1, 4, 2, 3)
    pool_prompt = u[:, S - POOL_STATE_LEN:][None]

    hs = _inproj_sample(x_sample.reshape(n_s, D), w_in_b)
    qs = hs[:, :ATTN_WIDTH].reshape(Bd, T, ATTN_WIDTH)
    ks = hs[:, ATTN_WIDTH:2 * ATTN_WIDTH].reshape(Bd, T, ATTN_WIDTH)
    vs = hs[:, 2 * ATTN_WIDTH:3 * ATTN_WIDTH].reshape(Bd, T, ATTN_WIDTH)
    us = hs[:, 3 * ATTN_WIDTH:].reshape(Bd, T, POOL_WIDTH)
    to_t = lambda a: a.reshape(Bd, T, N_HEADS, HEAD_DIM).transpose(0, 2, 3, 1)
    kc = cache_attn_k[0].transpose(0, 2, 3, 1)
    vc = cache_attn_v[0].transpose(0, 2, 3, 1)
    attn_s, k_new, v_new = _attn_sample(qs, ks, vs, to_t(ks), to_t(vs), kc, vc, _decode_tables(T, Lb))
    k_sample = k_new.transpose(0, 3, 1, 2)[None]
    v_sample = v_new.transpose(0, 3, 1, 2)[None]
    uext_tm = jnp.concatenate([state_pool[0].transpose(1, 0, 2), us.transpose(1, 0, 2)], axis=0)
    pool_sample = uext_tm[T:].transpose(1, 0, 2)[None]
    x_tm = x_sample.transpose(1, 0, 2).reshape(n_s, D)
    attn_tm = attn_s.transpose(1, 0, 2).reshape(n_s, ATTN_WIDTH)
    h1_s, hp_s, idx_s, gate_s, rank_s, counts = _mix_sample(
        alpha, T, x_tm, attn_tm, uext_tm, w_out_b, pool_w_b, pool_scale, ln1_g, ln1_b, wr_t, br, cnt_p)

    idx_all = jnp.concatenate([idx_p, idx_s], axis=1)
    rank_all = jnp.concatenate([rank_p, rank_s], axis=1)
    gate_rows = jnp.concatenate([gate_p, gate_s], axis=1).T
    dest_tiles, block_e, n_active, n_valid, zero_flag, n_rows = _moe_layout(idx_all, rank_all, counts, n_total)
    xs = _dispatch(zero_flag, dest_tiles, hp_p, hp_s, n_rows)
    ys = _experts(block_e, n_active, n_valid, xs, w_up[0], b_up[0][:, None, :], w_down[0], b_down[0][:, None, :])
    out_p = _combine(alpha, 0, dest_tiles, ys, h1_p, gate_rows, ln2_g, ln2_b)
    out_s = _combine(alpha, n_p // ROW_TILE, dest_tiles, ys, h1_s, gate_rows, ln2_g, ln2_b)

    y_prompt = out_p.reshape(B, S, D)
    y_sample = out_s.reshape(T, Bd, D).transpose(1, 0, 2)
    return (y_prompt, y_sample, k_prompt, v_prompt, pool_prompt, k_sample, v_sample, pool_sample)
```

```python
import functools

import jax
import jax.numpy as jnp
import numpy as np
from jax import lax
from jax.experimental import pallas as pl
from jax.experimental.pallas import tpu as pltpu

HEAD_DIM = 64
N_HEADS = 8
ATTN_WIDTH = N_HEADS * HEAD_DIM
DILATIONS = (1, 4, 16)
WINDOWS = (128, 512, 2048)
N_BACK = 128
WINDOW_MAX = 2048
ATTN_BLOCK = 128
POOL_SIZES = (2, 4, 8, 16)
POOL_GROUP_DIM = 128
POOL_WIDTH = 512
POOL_STATE_LEN = 15
N_EXPERTS = 32
TOP_K = 4
SWIGLU_ALPHA = 1.702
SWIGLU_LIMIT = 7.0
LN_EPS = 1e-5
NEG_INF = -1e30

LANES = 128
HEADS_PER_CHUNK = LANES // HEAD_DIM
N_CHUNKS = ATTN_WIDTH // LANES
VMEM_LIMIT = 56 * 1024 * 1024

ATTN_TILE = 2048
BLOCK_UNROLL = 16
PROJ_TILE = 512
MOE_BLOCK = 512
EXPERT_ROW_STEPS = 4
ROW_TILE = 128

_F32 = jnp.float32
_BF16 = jnp.bfloat16


def _alibi_slopes():
    return 2.0 ** (-8.0 * np.arange(1, N_HEADS + 1, dtype=np.float64) / N_HEADS)


def _params(sem, **kw):
    return pltpu.CompilerParams(dimension_semantics=sem, vmem_limit_bytes=VMEM_LIMIT, **kw)


def _inproj_body(n_keep_tiles, x_ref, w_ref, q_ref, k_ref, v_ref, u_ref, kt_ref, vt_ref):
    i = pl.program_id(1)
    n_tiles = pl.num_programs(1)
    h = jnp.dot(x_ref[...].astype(_BF16), w_ref[...], preferred_element_type=_F32)
    for j in range(N_CHUNKS):
        q_ref[j] = h[:, j * LANES:(j + 1) * LANES]
        k_ref[j] = h[:, ATTN_WIDTH + j * LANES:ATTN_WIDTH + (j + 1) * LANES]
        v_ref[j] = h[:, 2 * ATTN_WIDTH + j * LANES:2 * ATTN_WIDTH + (j + 1) * LANES]
    u_ref[...] = h[:, 3 * ATTN_WIDTH:]

    @pl.when(i >= n_tiles - n_keep_tiles)
    def _():
        kt_ref[...] = h[:, ATTN_WIDTH:2 * ATTN_WIDTH].T
        vt_ref[...] = h[:, 2 * ATTN_WIDTH:3 * ATTN_WIDTH].T


def _inproj_prompt(x, w_in_bf16):
    B, S, D = x.shape
    tm = PROJ_TILE
    n_tiles = S // tm
    n_keep = min(WINDOW_MAX, S)
    n_keep_tiles = n_keep // tm
    first_keep = n_tiles - n_keep_tiles
    chunked = jax.ShapeDtypeStruct((B, N_CHUNKS, S, LANES), _F32)
    chunk_spec = pl.BlockSpec((None, N_CHUNKS, tm, LANES), lambda b, i: (b, 0, i, 0))
    t_spec = pl.BlockSpec((None, ATTN_WIDTH, tm), lambda b, i: (b, 0, jnp.maximum(i - first_keep, 0)))
    return pl.pallas_call(
        functools.partial(_inproj_body, n_keep_tiles),
        grid=(B, n_tiles),
        in_specs=[pl.BlockSpec((None, tm, D), lambda b, i: (b, i, 0)),
                  pl.BlockSpec(w_in_bf16.shape, lambda b, i: (0, 0))],
        out_specs=[chunk_spec, chunk_spec, chunk_spec,
                   pl.BlockSpec((None, tm, POOL_WIDTH), lambda b, i: (b, i, 0)),
                   t_spec, t_spec],
        out_shape=[chunked, chunked, chunked,
                   jax.ShapeDtypeStruct((B, S, POOL_WIDTH), _F32),
                   jax.ShapeDtypeStruct((B, ATTN_WIDTH, n_keep), _F32),
                   jax.ShapeDtypeStruct((B, ATTN_WIDTH, n_keep), _F32)],
        compiler_params=_params(("arbitrary", "arbitrary")),
        name="inproj_prompt",
    )(x, w_in_bf16)


def _attn_bias_tables():
    qi = np.arange(ATTN_BLOCK)[:, None]
    kj = np.arange(2 * ATTN_BLOCK)[None, :]
    step = qi - kj + ATTN_BLOCK
    valid = (step >= 0) & (step <= N_BACK)
    slopes = _alibi_slopes()
    out = np.zeros((N_CHUNKS, len(DILATIONS), HEADS_PER_CHUNK * ATTN_BLOCK, 2 * ATTN_BLOCK), np.float32)
    for c in range(N_CHUNKS):
        for p, dil in enumerate(DILATIONS):
            for hh in range(HEADS_PER_CHUNK):
                bias = -slopes[c * HEADS_PER_CHUNK + hh] * (step * dil).astype(np.float64)
                out[c, p, hh * ATTN_BLOCK:(hh + 1) * ATTN_BLOCK] = np.where(valid, bias, NEG_INF)
    return jnp.asarray(out)


def _attn_body(q_ref, k_ref, v_ref, bias_ref, o_ref,
               qf1, kf1, vf1, qf4, kf4, vf4, qf16, kf16, vf16, acc_n, acc_m, acc_d):
    i = pl.program_id(2)
    T = ATTN_TILE
    folded = ((1, qf1, kf1, vf1), (4, qf4, kf4, vf4), (16, qf16, kf16, vf16))
    scale = HEAD_DIM ** -0.5

    for dil, _, kf, vf in folded:
        L = T // dil

        @pl.when(i == 0)
        def _():
            kf[:, 0:ATTN_BLOCK, :] = jnp.zeros((dil, ATTN_BLOCK, LANES), _BF16)
            vf[:, 0:ATTN_BLOCK, :] = jnp.zeros((dil, ATTN_BLOCK, LANES), _BF16)

        @pl.when(i > 0)
        def _():
            kf[:, 0:ATTN_BLOCK, :] = kf[:, L:L + ATTN_BLOCK, :]
            vf[:, 0:ATTN_BLOCK, :] = vf[:, L:L + ATTN_BLOCK, :]

    for dil, qf, kf, vf in folded:
        L = T // dil
        for r in range(dil):
            rows = pl.ds(r, L, stride=dil) if dil > 1 else pl.ds(0, L)
            qf[r] = (q_ref[rows, :] * scale).astype(_BF16)
            kf[r, ATTN_BLOCK:ATTN_BLOCK + L, :] = k_ref[rows, :].astype(_BF16)
            vf[r, ATTN_BLOCK:ATTN_BLOCK + L, :] = v_ref[rows, :].astype(_BF16)

    lane = lax.broadcasted_iota(jnp.int32, (ATTN_BLOCK, LANES), 1)
    first_head = lane < HEAD_DIM
    col = lax.broadcasted_iota(jnp.int32, (HEADS_PER_CHUNK * ATTN_BLOCK, 2 * ATTN_BLOCK), 1)
    prev_half = col < ATTN_BLOCK

    for p, (dil, qf, kf, vf) in enumerate(folded):
        blocks_per_res = T // dil // ATTN_BLOCK

        def block(blk, carry, p=p, dil=dil, qf=qf, kf=kf, vf=vf, blocks_per_res=blocks_per_res):
            r = blk // blocks_per_res
            c = blk % blocks_per_res
            row0 = pl.multiple_of(c * ATTN_BLOCK, ATTN_BLOCK)
            q = qf[r, pl.ds(row0, ATTN_BLOCK), :]
            kk = kf[r, pl.ds(row0, 2 * ATTN_BLOCK), :]
            vv = vf[r, pl.ds(row0, 2 * ATTN_BLOCK), :]
            zero = jnp.zeros_like(q)
            qm = jnp.concatenate([jnp.where(first_head, q, zero), jnp.where(first_head, zero, q)], axis=0)
            s = lax.dot_general(qm, kk, (((1,), (1,)), ((), ())), preferred_element_type=_F32)
            s = s + bias_ref[p]
            s = jnp.where(jnp.logical_and(prev_half, jnp.logical_and(i == 0, c == 0)), NEG_INF, s)
            m = jnp.max(s, axis=-1, keepdims=True)
            e = jnp.exp(s - m)
            den = jnp.sum(e, axis=-1, keepdims=True)
            pv = jnp.dot(e.astype(_BF16), vv, preferred_element_type=_F32)
            num = jnp.where(first_head, pv[:ATTN_BLOCK], pv[ATTN_BLOCK:])
            m2 = jnp.where(first_head, m[:ATTN_BLOCK], m[ATTN_BLOCK:])
            d2 = jnp.where(first_head, den[:ATTN_BLOCK], den[ATTN_BLOCK:])
            start = r + row0 * dil
            rows = pl.ds(start, ATTN_BLOCK, stride=dil) if dil > 1 else pl.ds(start, ATTN_BLOCK)
            acc_n[p, rows, :] = num
            acc_m[p, rows, :] = m2
            acc_d[p, rows, :] = d2
            return carry

        lax.fori_loop(0, T // ATTN_BLOCK, block, 0, unroll=BLOCK_UNROLL)

    chunk_rows = 256

    def merge(t, carry):
        rows = pl.ds(pl.multiple_of(t * chunk_rows, chunk_rows), chunk_rows)
        m0, m1, m2 = acc_m[0, rows, :], acc_m[1, rows, :], acc_m[2, rows, :]
        mx = jnp.maximum(jnp.maximum(m0, m1), m2)
        e0, e1, e2 = jnp.exp(m0 - mx), jnp.exp(m1 - mx), jnp.exp(m2 - mx)
        num = acc_n[0, rows, :] * e0 + acc_n[1, rows, :] * e1 + acc_n[2, rows, :] * e2
        den = acc_d[0, rows, :] * e0 + acc_d[1, rows, :] * e1 + acc_d[2, rows, :] * e2
        o_ref[rows, :] = (num / den).astype(o_ref.dtype)
        return carry

    lax.fori_loop(0, T // chunk_rows, merge, 0)


def _attn_prompt(q, k, v, bias):
    B, _, S, _ = q.shape
    T = ATTN_TILE
    io_spec = pl.BlockSpec((None, None, T, LANES), lambda b, c, i: (b, c, i, 0))
    scratch = []
    for dil in DILATIONS:
        L = T // dil
        scratch += [pltpu.VMEM((dil, L, LANES), _BF16),
                    pltpu.VMEM((dil, ATTN_BLOCK + L, LANES), _BF16),
                    pltpu.VMEM((dil, ATTN_BLOCK + L, LANES), _BF16)]
    scratch += [pltpu.VMEM((len(DILATIONS), T, LANES), _F32)] * 3
    return pl.pallas_call(
        _attn_body,
        grid=(B, N_CHUNKS, S // T),
        in_specs=[io_spec, io_spec, io_spec,
                  pl.BlockSpec((None,) + bias.shape[1:], lambda b, c, i: (c, 0, 0, 0))],
        out_specs=io_spec,
        out_shape=jax.ShapeDtypeStruct((B, N_CHUNKS, S, LANES), _BF16),
        scratch_shapes=scratch,
        compiler_params=_params(("arbitrary", "arbitrary", "arbitrary")),
        name="attn_prompt",
    )(q, k, v, bias)


def _layer_norm(z, g, b):
    mu = jnp.mean(z, axis=-1, keepdims=True)
    zc = z - mu
    var = jnp.mean(zc * zc, axis=-1, keepdims=True)
    return zc * lax.rsqrt(var + LN_EPS) * g + b


def _pack_bf16_pairs(h):
    w = h.shape[1] // 2
    bits = pltpu.bitcast(h.astype(_BF16).astype(_F32), jnp.uint32)
    return (bits[:, w:] & jnp.uint32(0xFFFF0000)) | (bits[:, :w] >> 16)


def _unpack_bf16_pairs(words):
    lo = pltpu.bitcast(words << 16, _F32)
    hi = pltpu.bitcast(words & jnp.uint32(0xFFFF0000), _F32)
    return jnp.concatenate([lo, hi], axis=1).astype(_BF16)


def _route(h1, wr_ref, br_ref, cnt_ref, idx_ref, gate_ref, rank_ref):
    tm = h1.shape[0]
    nt = lambda a, b: lax.dot_general(a, b, (((1,), (1,)), ((), ())), preferred_element_type=_F32)
    h_hi = h1.astype(_BF16)
    h_lo = (h1 - h_hi.astype(_F32)).astype(_BF16)
    by_hi, by_lo = nt(wr_ref[...], h_hi), nt(wr_ref[...], h_lo)
    logits = (by_hi[:N_EXPERTS] + by_hi[N_EXPERTS:]) + (by_lo[:N_EXPERTS] + by_lo[N_EXPERTS:]) + br_ref[...]
    eid = lax.broadcasted_iota(jnp.int32, (N_EXPERTS, tm), 0)
    work = logits
    vals, ids, hots = [], [], []
    for _ in range(TOP_K):
        mx = jnp.max(work, axis=0, keepdims=True)
        sel = jnp.min(jnp.where(work == mx, eid, N_EXPERTS), axis=0, keepdims=True)
        hot = eid == sel
        vals.append(mx)
        ids.append(sel)
        hots.append(hot)
        work = jnp.where(hot, -jnp.inf, work)
    ex = [jnp.exp(v - vals[0]) for v in vals]
    tot = ex[0] + ex[1] + ex[2] + ex[3]
    any_hot = jnp.where(hots[0] | hots[1] | hots[2] | hots[3], 1.0, 0.0)
    earlier = (lax.broadcasted_iota(jnp.int32, (tm, tm), 0) < lax.broadcasted_iota(jnp.int32, (tm, tm), 1))
    prefix = jnp.dot(any_hot.astype(_BF16), jnp.where(earlier, 1.0, 0.0).astype(_BF16),
                     preferred_element_type=_F32) + cnt_ref[...]
    for kk in range(TOP_K):
        idx_ref[kk:kk + 1, :] = ids[kk]
        gate_ref[kk:kk + 1, :] = ex[kk] / tot
        rank_ref[kk:kk + 1, :] = jnp.sum(jnp.where(hots[kk], prefix, 0.0), axis=0, keepdims=True).astype(jnp.int32)
    cnt_ref[...] = cnt_ref[...] + jnp.sum(any_hot, axis=1, keepdims=True)


def _pool_mix(win_fn, u_tile, cnt_fn, pw_ref, ps_ref):
    outs = []
    for g, size in enumerate(POOL_SIZES):
        lanes = slice(g * POOL_GROUP_DIM, (g + 1) * POOL_GROUP_DIM)
        win = win_fn(0, lanes)
        for j in range(1, size):
            win = win + win_fn(j, lanes)
        d = win / cnt_fn(size) - u_tile[:, lanes]
        y = jnp.dot(d.astype(_BF16), pw_ref[g], preferred_element_type=_F32)
        outs.append(y * ps_ref[:, lanes])
    return jnp.concatenate(outs, axis=1)


def _mix_prompt_body(alpha, x_ref, a_ref, u_ref, wo_ref, pw_ref, ps_ref, g_ref, b_ref, wr_ref, br_ref,
                     h1_ref, hp_ref, idx_ref, gate_ref, rank_ref, cnt_out_ref, uext, cnt_ref):
    b = pl.program_id(0)
    i = pl.program_id(1)
    tm = x_ref.shape[0]
    halo = 16

    @pl.when(jnp.logical_and(b == 0, i == 0))
    def _():
        cnt_ref[...] = jnp.zeros_like(cnt_ref)

    @pl.when(i == 0)
    def _():
        uext[0:halo, :] = jnp.zeros((halo, POOL_WIDTH), _F32)

    @pl.when(i > 0)
    def _():
        uext[0:halo, :] = uext[tm:tm + halo, :]

    uext[halo:halo + tm, :] = u_ref[...]
    pos = i * tm + lax.broadcasted_iota(jnp.int32, (tm, 1), 0)
    pool = _pool_mix(lambda j, lanes: uext[halo - j:halo - j + tm, lanes], u_ref,
                     lambda size: jnp.minimum(pos + 1, size).astype(_F32), pw_ref, ps_ref)
    cat = jnp.concatenate([a_ref[j] for j in range(N_CHUNKS)] + [pool.astype(_BF16)], axis=1)
    mix = jnp.dot(cat, wo_ref[...], preferred_element_type=_F32)
    h1 = _layer_norm(alpha * x_ref[...] + mix, g_ref[...], b_ref[...])
    h1_ref[...] = h1
    hp_ref[...] = _pack_bf16_pairs(h1)
    _route(h1, wr_ref, br_ref, cnt_ref, idx_ref, gate_ref, rank_ref)
    cnt_out_ref[...] = cnt_ref[...]


def _mix_prompt(alpha, x, attn, u, w_out_bf16, pool_w_bf16, pool_scale, ln_g, ln_b, wr_t, br):
    B, S, D = x.shape
    n_total = B * S
    tm = PROJ_TILE
    n_tiles = S // tm
    tok = lambda b, i: (b * n_tiles + i, 0)
    tok_t = lambda b, i: (0, b * n_tiles + i)
    const2 = lambda b, i: (0, 0)
    return pl.pallas_call(
        functools.partial(_mix_prompt_body, alpha),
        grid=(B, n_tiles),
        in_specs=[pl.BlockSpec((None, tm, D), lambda b, i: (b, i, 0)),
                  pl.BlockSpec((None, N_CHUNKS, tm, LANES), lambda b, i: (b, 0, i, 0)),
                  pl.BlockSpec((None, tm, POOL_WIDTH), lambda b, i: (b, i, 0)),
                  pl.BlockSpec(w_out_bf16.shape, const2),
                  pl.BlockSpec(pool_w_bf16.shape, lambda b, i: (0, 0, 0)),
                  pl.BlockSpec(pool_scale.shape, const2),
                  pl.BlockSpec(ln_g.shape, const2), pl.BlockSpec(ln_b.shape, const2),
                  pl.BlockSpec(wr_t.shape, const2), pl.BlockSpec(br.shape, const2)],
        out_specs=[pl.BlockSpec((tm, D), tok), pl.BlockSpec((tm, D // 2), tok),
                   pl.BlockSpec((TOP_K, tm), tok_t), pl.BlockSpec((TOP_K, tm), tok_t),
                   pl.BlockSpec((TOP_K, tm), tok_t), pl.BlockSpec((N_EXPERTS, 1), const2)],
        out_shape=[jax.ShapeDtypeStruct((n_total, D), _F32), jax.ShapeDtypeStruct((n_total, D // 2), jnp.uint32),
                   jax.ShapeDtypeStruct((TOP_K, n_total), jnp.int32), jax.ShapeDtypeStruct((TOP_K, n_total), _F32),
                   jax.ShapeDtypeStruct((TOP_K, n_total), jnp.int32), jax.ShapeDtypeStruct((N_EXPERTS, 1), _F32)],
        scratch_shapes=[pltpu.VMEM((16 + tm, POOL_WIDTH), _F32), pltpu.VMEM((N_EXPERTS, 1), _F32)],
        compiler_params=_params(("arbitrary", "arbitrary")),
        name="mix_prompt",
    )(x, attn, u, w_out_bf16, pool_w_bf16, pool_scale, ln_g, ln_b, wr_t, br)


def _inproj_sample_body(x_ref, w_ref, o_ref):
    o_ref[...] = jnp.dot(x_ref[...].astype(_BF16), w_ref[...], preferred_element_type=_F32)


def _inproj_sample(x2d, w_in_bf16):
    return pl.pallas_call(
        _inproj_sample_body,
        out_shape=jax.ShapeDtypeStruct((x2d.shape[0], w_in_bf16.shape[1]), _F32),
        compiler_params=_params(None),
        name="inproj_sample",
    )(x2d, w_in_bf16)


def _decode_tables(T, Lb):
    slopes = _alibi_slopes()

    def mult(delta):
        m = np.zeros(delta.shape, np.float64)
        for win, dil in zip(WINDOWS, DILATIONS):
            m += ((delta % dil == 0) & (delta // dil <= win // dil) & (delta >= 0))
        return m

    t = np.arange(T)[:, None]
    d_cache = Lb + t - np.arange(Lb)[None, :]
    d_new = t - np.arange(T)[None, :]
    m_cache, m_new = mult(d_cache), mult(d_new)
    b_cache = np.where(m_cache > 0, -slopes[:, None, None] * d_cache[None], NEG_INF)
    b_new = np.where(m_new > 0, -slopes[:, None, None] * d_new[None], NEG_INF)
    f = lambda a: jnp.asarray(a.astype(np.float32))
    return f(b_cache), f(m_cache), f(b_new), f(m_new)


def _attn_sample_body(q_ref, kn_ref, vn_ref, knt_ref, vnt_ref, kc_ref, vc_ref, bc_ref, mc_ref, bn_ref, mn_ref,
                      o_ref, ko_ref, vo_ref):
    T = q_ref.shape[0]
    scale = HEAD_DIM ** -0.5
    rnd = lambda a: a.astype(_BF16).astype(_F32)
    new_col = lax.broadcasted_iota(jnp.int32, (T, T), 1)
    outs = []
    for h in range(N_HEADS):
        lanes = slice(h * HEAD_DIM, (h + 1) * HEAD_DIM)
        q = rnd(q_ref[:, lanes] * scale)
        kn = rnd(kn_ref[:, lanes])
        vn = rnd(vn_ref[:, lanes])
        s_c = jnp.dot(q.astype(_BF16), kc_ref[h].astype(_BF16), preferred_element_type=_F32) + bc_ref[h]
        s_n = bn_ref[h]
        for t in range(T):
            s_n = s_n + jnp.where(new_col == t, jnp.sum(q * kn[t:t + 1, :], axis=-1, keepdims=True), 0.0)
        m = jnp.maximum(jnp.max(s_c, axis=-1, keepdims=True), jnp.max(s_n, axis=-1, keepdims=True))
        e_c = mc_ref[...] * jnp.exp(s_c - m)
        e_n = mn_ref[...] * jnp.exp(s_n - m)
        den = jnp.sum(e_c, axis=-1, keepdims=True) + jnp.sum(e_n, axis=-1, keepdims=True)
        num = lax.dot_general(e_c.astype(_BF16), vc_ref[h].astype(_BF16), (((1,), (1,)), ((), ())),
                              preferred_element_type=_F32)
        e_nr = rnd(e_n)
        for t in range(T):
            num = num + e_nr[:, t:t + 1] * vn[t:t + 1, :]
        outs.append(num / den)
    o_ref[...] = jnp.concatenate(outs, axis=1)
    ko_ref[...] = jnp.concatenate([kc_ref[...][:, :, T:], knt_ref[...]], axis=-1)
    vo_ref[...] = jnp.concatenate([vc_ref[...][:, :, T:], vnt_ref[...]], axis=-1)


def _attn_sample(q, kn, vn, knt, vnt, kc, vc, tables):
    Bd, T, _ = q.shape
    Lb = kc.shape[-1]
    tok = pl.BlockSpec((None, T, ATTN_WIDTH), lambda b: (b, 0, 0))
    newt = pl.BlockSpec((None, N_HEADS, HEAD_DIM, T), lambda b: (b, 0, 0, 0))
    cache = pl.BlockSpec((None, N_HEADS, HEAD_DIM, Lb), lambda b: (b, 0, 0, 0))
    tabs = [pl.BlockSpec(t.shape, (lambda b, n=t.ndim: (0,) * n)) for t in tables]
    return pl.pallas_call(
        _attn_sample_body,
        grid=(Bd,),
        in_specs=[tok, tok, tok, newt, newt, cache, cache] + tabs,
        out_specs=[tok, cache, cache],
        out_shape=[jax.ShapeDtypeStruct((Bd, T, ATTN_WIDTH), _F32),
                   jax.ShapeDtypeStruct(kc.shape, _F32), jax.ShapeDtypeStruct(vc.shape, _F32)],
        compiler_params=_params(("arbitrary",)),
        name="attn_sample",
    )(q, kn, vn, knt, vnt, kc, vc, *tables)


def _mix_sample_body(alpha, T, x_ref, a_ref, ue_ref, wo_ref, pw_ref, ps_ref, g_ref, b_ref, wr_ref, br_ref, cnt_in_ref,
                     h1_ref, hp_ref, idx_ref, gate_ref, rank_ref, cnt_out_ref, cnt_ref):
    cnt_ref[...] = cnt_in_ref[...]
    hist = POOL_STATE_LEN
    pools = []
    for t in range(T):
        pools.append(_pool_mix(lambda j, lanes, t=t: ue_ref[hist + t - j, :, lanes], ue_ref[hist + t],
                               lambda size: float(size), pw_ref, ps_ref))
    pool = jnp.concatenate(pools, axis=0)
    cat = jnp.concatenate([a_ref[...].astype(_BF16), pool.astype(_BF16)], axis=1)
    mix = jnp.dot(cat, wo_ref[...], preferred_element_type=_F32)
    h1 = _layer_norm(alpha * x_ref[...] + mix, g_ref[...], b_ref[...])
    h1_ref[...] = h1
    hp_ref[...] = _pack_bf16_pairs(h1)
    _route(h1, wr_ref, br_ref, cnt_ref, idx_ref, gate_ref, rank_ref)
    cnt_out_ref[...] = cnt_ref[...]


def _mix_sample(alpha, T, x_tm, attn_tm, uext_tm, w_out_bf16, pool_w_bf16, pool_scale, ln_g, ln_b, wr_t, br, cnt_in):
    n_s, D = x_tm.shape
    full = lambda shape: pl.BlockSpec(shape, lambda i, n=len(shape): (0,) * n)
    ins = [x_tm, attn_tm, uext_tm, w_out_bf16, pool_w_bf16, pool_scale, ln_g, ln_b, wr_t, br, cnt_in]
    outs = [jax.ShapeDtypeStruct((n_s, D), _F32), jax.ShapeDtypeStruct((n_s, D // 2), jnp.uint32),
            jax.ShapeDtypeStruct((TOP_K, n_s), jnp.int32), jax.ShapeDtypeStruct((TOP_K, n_s), _F32),
            jax.ShapeDtypeStruct((TOP_K, n_s), jnp.int32), jax.ShapeDtypeStruct((N_EXPERTS, 1), _F32)]
    return pl.pallas_call(
        functools.partial(_mix_sample_body, alpha, T),
        grid=(1,),
        in_specs=[full(a.shape) for a in ins],
        out_specs=[full(o.shape) for o in outs],
        out_shape=outs,
        scratch_shapes=[pltpu.VMEM((N_EXPERTS, 1), _F32)],
        compiler_params=_params(("arbitrary",)),
        name="mix_sample",
    )(*ins)


def _dispatch_body(n_p_tiles, zflag_ref, dest_hbm, hp_p_ref, hp_s_ref, xs_hbm, tile, zeros, dest_smem,
                   sem_idx, sem_rows, sem_zero):
    i = pl.program_id(0)
    n_steps = pl.num_programs(0)
    tm = tile.shape[1]

    @pl.when(i == 0)
    def _():
        zeros[...] = jnp.zeros_like(zeros)

        blk = zeros.shape[0]

        def zero_copy(j):
            return pltpu.make_async_copy(zeros, xs_hbm.at[pl.ds(pl.multiple_of(j * blk, blk), blk)], sem_zero)

        def start(j, carry):
            @pl.when(zflag_ref[j] != 0)
            def _():
                zero_copy(j).start()
            return carry

        def wait(j, carry):
            @pl.when(zflag_ref[j] != 0)
            def _():
                zero_copy(j).wait()
            return carry

        lax.fori_loop(0, zflag_ref.shape[0], start, 0)
        lax.fori_loop(0, zflag_ref.shape[0], wait, 0)

    def idx_copy(t, s):
        return pltpu.make_async_copy(dest_hbm.at[t], dest_smem.at[pl.ds(s * TOP_K * tm, TOP_K * tm)], sem_idx.at[s])

    def wait_rows(s):
        for _ in range(TOP_K):
            pltpu.make_async_copy(tile.at[s], xs_hbm.at[pl.ds(0, tm)], sem_rows.at[s]).wait()

    @pl.when(i == 0)
    def _():
        idx_copy(i, 0).start()

    staged = jnp.where(i < n_p_tiles, hp_p_ref[...], hp_s_ref[...])

    for s in range(2):
        @pl.when(i % 2 == s)
        def _(s=s):
            @pl.when(i + 1 < n_steps)
            def _():
                idx_copy(i + 1, 1 - s).start()

            tile[s] = staged
            idx_copy(i, s).wait()

            for n in range(tm):
                for kk in range(TOP_K):
                    pltpu.make_async_copy(tile.at[s, pl.ds(n, 1)], xs_hbm.at[pl.ds(dest_smem[(s * TOP_K + kk) * tm + n], 1)],
                                          sem_rows.at[s]).start(priority=kk % 2)

            @pl.when(i > 0)
            def _():
                wait_rows(1 - s)

            @pl.when(i == n_steps - 1)
            def _():
                wait_rows(s)


def _dispatch(zero_flag, dest_tiles, hp_p, hp_s, n_rows):
    w = hp_p.shape[1]
    tm = ROW_TILE
    n_p_tiles, n_s_tiles = hp_p.shape[0] // tm, hp_s.shape[0] // tm
    grid_spec = pltpu.PrefetchScalarGridSpec(
        num_scalar_prefetch=1,
        grid=(n_p_tiles + n_s_tiles,),
        in_specs=[pl.BlockSpec(memory_space=pl.ANY),
                  pl.BlockSpec((tm, w), lambda i, z: (jnp.minimum(i, n_p_tiles - 1), 0)),
                  pl.BlockSpec((tm, w), lambda i, z: (jnp.maximum(i - n_p_tiles, 0), 0))],
        out_specs=pl.BlockSpec(memory_space=pl.ANY),
        scratch_shapes=[pltpu.VMEM((2, tm, w), jnp.uint32), pltpu.VMEM((MOE_BLOCK, w), jnp.uint32),
                        pltpu.SMEM((2 * TOP_K * tm,), jnp.int32),
                        pltpu.SemaphoreType.DMA((2,)), pltpu.SemaphoreType.DMA((2,)), pltpu.SemaphoreType.DMA(())],
    )
    return pl.pallas_call(
        functools.partial(_dispatch_body, n_p_tiles),
        grid_spec=grid_spec,
        out_shape=jax.ShapeDtypeStruct((n_rows, w), jnp.uint32),
        compiler_params=_params(("arbitrary",)),
        name="moe_dispatch",
    )(zero_flag, dest_tiles, hp_p, hp_s)


def _experts_body(be_ref, na_ref, nv_ref, eseq_ref, enext_ref, xs_ref, wu_hbm, bu_ref, wd_hbm, bd_ref, ys_ref,
                  wu_bf, wd_bf, wu_f32, wd_f32, sem_w):
    j = pl.program_id(0)
    blk = xs_ref.shape[0]
    d_ff = wd_bf.shape[0]
    active = j < na_ref[0]
    changed = jnp.logical_or(j == 0, be_ref[j] != be_ref[jnp.maximum(j - 1, 0)])

    def weight_copies(e, s):
        return (pltpu.make_async_copy(wu_hbm.at[e], wu_f32.at[s], sem_w.at[s]),
                pltpu.make_async_copy(wd_hbm.at[e], wd_f32.at[s], sem_w.at[s]))

    @pl.when(j == 0)
    def _():
        for c in weight_copies(be_ref[0], 0):
            c.start()

    for s in range(2):
        @pl.when(jnp.logical_and(jnp.logical_and(active, changed), eseq_ref[j] % 2 == s))
        def _(s=s):
            for c in weight_copies(be_ref[j], s):
                c.wait()

            @pl.when(enext_ref[j] >= 0)
            def _():
                for c in weight_copies(enext_ref[j], 1 - s):
                    c.start()

            wu_bf[...] = wu_f32[s].astype(_BF16)
            wd_bf[...] = wd_f32[s].astype(_BF16)

    def ffn(rows):
        x = _unpack_bf16_pairs(xs_ref[0:rows, :])
        h = jnp.dot(x, wu_bf[...], preferred_element_type=_F32) + bu_ref[...]
        g = jnp.minimum(h[:, :d_ff], SWIGLU_LIMIT)
        lin = jnp.clip(h[:, d_ff:], -SWIGLU_LIMIT, SWIGLU_LIMIT)
        act = g * jax.nn.sigmoid(SWIGLU_ALPHA * g) * (lin + 1.0)
        y = jnp.dot(act.astype(_BF16), wd_bf[...], preferred_element_type=_F32) + bd_ref[...]
        ys_ref[0:rows, :] = _pack_bf16_pairs(y)
        if rows < blk:
            ys_ref[rows:, :] = jnp.zeros((blk - rows, ys_ref.shape[1]), ys_ref.dtype)

    quarter = blk // EXPERT_ROW_STEPS
    n_quarters = (nv_ref[j] + quarter - 1) // quarter
    for q in range(1, EXPERT_ROW_STEPS + 1):
        @pl.when(jnp.logical_and(active, n_quarters == q))
        def _(q=q):
            ffn(q * quarter)

    @pl.when(jnp.logical_or(jnp.logical_not(active), n_quarters == 0))
    def _():
        ys_ref[...] = jnp.zeros_like(ys_ref)


def _experts(block_e, n_active, n_valid, expert_seq, expert_next, xs, w_up, b_up, w_down, b_down):
    n_rows, w = xs.shape
    blk = MOE_BLOCK
    n_blocks = n_rows // blk
    d_model, d_ff2 = w_up.shape[1:]
    d_ff = w_down.shape[1]
    act_blk = lambda j, be, na, *_: jnp.minimum(j, jnp.maximum(na[0] - 1, 0))
    grid_spec = pltpu.PrefetchScalarGridSpec(
        num_scalar_prefetch=5,
        grid=(n_blocks,),
        in_specs=[pl.BlockSpec((blk, w), lambda j, be, na, *_: (act_blk(j, be, na), 0)),
                  pl.BlockSpec(memory_space=pl.ANY),
                  pl.BlockSpec((None, 1, d_ff2), lambda j, be, *_: (be[j], 0, 0)),
                  pl.BlockSpec(memory_space=pl.ANY),
                  pl.BlockSpec((None, 1, d_model), lambda j, be, *_: (be[j], 0, 0))],
        out_specs=pl.BlockSpec((blk, d_model // 2), lambda j, *_: (j, 0)),
        scratch_shapes=[pltpu.VMEM((d_model, d_ff2), _BF16), pltpu.VMEM((d_ff, d_model), _BF16),
                        pltpu.VMEM((2, d_model, d_ff2), _F32), pltpu.VMEM((2, d_ff, d_model), _F32),
                        pltpu.SemaphoreType.DMA((2,))],
    )
    return pl.pallas_call(
        _experts_body,
        grid_spec=grid_spec,
        out_shape=jax.ShapeDtypeStruct((n_rows, d_model // 2), jnp.uint32),
        compiler_params=_params(("arbitrary",)),
        name="moe_experts",
    )(block_e, n_active, n_valid, expert_seq, expert_next, xs, w_up, b_up, w_down, b_down)


def _combine_body(alpha, tile0, dest_hbm, ys_hbm, h1_ref, gate_ref, g_ref, b_ref, o_ref,
                  dest_smem, buf, sem_idx, sem_rows):
    i = pl.program_id(0)
    n_steps = pl.num_programs(0)
    tm = o_ref.shape[0]
    group = 32

    def idx_copy(t, s):
        return pltpu.make_async_copy(dest_hbm.at[tile0 + t], dest_smem.at[pl.ds(s * TOP_K * tm, TOP_K * tm)],
                                     sem_idx.at[s])

    def gather_tile(s):
        for n in range(tm):
            for kk in range(TOP_K):
                row = dest_smem[(s * TOP_K + kk) * tm + n]
                pltpu.make_async_copy(ys_hbm.at[pl.ds(row, 1)], buf.at[s, kk, pl.ds(n, 1)],
                                      sem_rows.at[s]).start(priority=kk % 2)

    def wait_rows(s):
        for kk in range(TOP_K):
            pltpu.make_async_copy(ys_hbm.at[pl.ds(0, tm)], buf.at[s, kk], sem_rows.at[s]).wait()

    def reduce_rows(s, r0):
        rows = pl.ds(r0, group)
        lo = hi = None
        for kk in range(TOP_K):
            words = buf[s, kk, rows, :]
            gate = gate_ref[rows, kk:kk + 1]
            t_lo = gate * pltpu.bitcast(words << 16, _F32)
            t_hi = gate * pltpu.bitcast(words & jnp.uint32(0xFFFF0000), _F32)
            lo, hi = (t_lo, t_hi) if kk == 0 else (lo + t_lo, hi + t_hi)
        o_ref[rows, :] = _layer_norm(alpha * h1_ref[rows, :] + jnp.concatenate([lo, hi], axis=1), g_ref[...], b_ref[...])

    def row_groups(fn, unroll=1):
        def body(gi, carry):
            fn(pl.multiple_of(gi * group, group))
            return carry
        lax.fori_loop(0, tm // group, body, 0, unroll=unroll)

    @pl.when(i == 0)
    def _():
        idx_copy(i, 0).start()
        idx_copy(i, 0).wait()
        gather_tile(0)

        @pl.when(n_steps > 1)
        def _():
            idx_copy(i + 1, 1).start()

    for s in range(2):
        @pl.when(i % 2 == s)
        def _(s=s):
            @pl.when(i + 1 < n_steps)
            def _():
                idx_copy(i + 1, 1 - s).wait()
                gather_tile(1 - s)

            @pl.when(i + 2 < n_steps)
            def _():
                idx_copy(i + 2, s).start()

            wait_rows(s)
            row_groups(lambda r0: reduce_rows(s, r0), unroll=tm // group)


def _combine(alpha, tile0, dest_tiles, ys, h1, gate_rows, ln_g, ln_b):
    n_tok, d_model = h1.shape
    tm = ROW_TILE
    const2 = lambda i: (0, 0)
    return pl.pallas_call(
        functools.partial(_combine_body, alpha, tile0),
        grid=(n_tok // tm,),
        in_specs=[pl.BlockSpec(memory_space=pl.ANY), pl.BlockSpec(memory_space=pl.ANY),
                  pl.BlockSpec((tm, d_model), lambda i: (i, 0)),
                  pl.BlockSpec((tm, TOP_K), lambda i: (tile0 + i, 0)),
                  pl.BlockSpec(ln_g.shape, const2), pl.BlockSpec(ln_b.shape, const2)],
        out_specs=pl.BlockSpec((tm, d_model), lambda i: (i, 0)),
        out_shape=jax.ShapeDtypeStruct(h1.shape, _F32),
        scratch_shapes=[pltpu.SMEM((2 * TOP_K * tm,), jnp.int32), pltpu.VMEM((2, TOP_K, tm, ys.shape[1]), ys.dtype),
                        pltpu.SemaphoreType.DMA((2,)), pltpu.SemaphoreType.DMA((2,))],
        compiler_params=_params(("arbitrary",)),
        name="moe_combine_tile%d" % tile0,
    )(dest_tiles, ys, h1, gate_rows, ln_g, ln_b)


def _moe_layout(idx_all, rank_all, counts, n_total):
    blk = MOE_BLOCK
    n_blocks = (TOP_K * n_total + N_EXPERTS * (blk - 1)) // blk + 1
    cnt = counts.reshape(N_EXPERTS).astype(jnp.int32)
    padded = (cnt + blk - 1) // blk * blk
    pend = jnp.cumsum(padded)
    pstart = pend - padded
    experts = jnp.arange(N_EXPERTS, dtype=jnp.int32)
    dest = rank_all + jnp.sum(jnp.where(idx_all[None] == experts[:, None, None], pstart[:, None, None], 0), axis=0)
    block_row0 = jnp.arange(n_blocks, dtype=jnp.int32) * blk
    block_e = jnp.minimum(jnp.sum((pend[None, :] <= block_row0[:, None]).astype(jnp.int32), axis=1), N_EXPERTS - 1)
    n_active = (pend[-1] // blk).astype(jnp.int32).reshape(1)
    last_e = block_e[jnp.maximum(n_active[0] - 1, 0)]
    block_e = jnp.where(jnp.arange(n_blocks) < n_active[0], block_e, last_e)
    tm = ROW_TILE
    dest_tiles = dest.reshape(TOP_K, n_total // tm, tm).transpose(1, 0, 2).reshape(n_total // tm, TOP_K * tm)
    blocks = jnp.arange(n_blocks, dtype=jnp.int32)
    is_last = jnp.any((pend[None, :] == block_row0[:, None] + blk) & (cnt[None, :] > 0), axis=1)
    zero_flag = (is_last | (blocks >= n_active[0])).astype(jnp.int32)
    is_e = block_e[:, None] == experts[None, :]
    rows_left = jnp.sum(jnp.where(is_e, (pstart + cnt)[None, :], 0), axis=1) - block_row0
    n_valid = jnp.where(blocks < n_active[0], jnp.clip(rows_left, 0, blk), 0).astype(jnp.int32)
    switch = (blocks < n_active[0]) & jnp.concatenate([jnp.ones((1,), bool), block_e[1:] != block_e[:-1]])
    expert_seq = (jnp.cumsum(switch.astype(jnp.int32)) - 1).astype(jnp.int32)
    switch_at = jnp.where(switch, blocks, n_blocks)
    next_switch = jnp.concatenate([lax.cummin(switch_at[::-1])[::-1][1:], jnp.full((1,), n_blocks, jnp.int32)])
    next_e = jnp.sum(jnp.where(blocks[None, :] == next_switch[:, None], block_e[None, :], 0), axis=1)
    expert_next = jnp.where(next_switch < n_blocks, next_e, -1).astype(jnp.int32)
    return dest_tiles, block_e, n_active, n_valid, expert_seq, expert_next, zero_flag, n_blocks * blk


def kernel(x_prompt, x_sample, cache_attn_k, cache_attn_v, state_pool, w_in, w_out, pool_w, pool_scale, ln1_g, ln1_b,
           w_router, b_router, w_up, b_up, w_down, b_down, ln2_g, ln2_b):
    depth = w_in.shape[0]
    assert depth == 1, "single-layer step"
    B, S, D = x_prompt.shape
    Bd, T, _ = x_sample.shape
    Lb = cache_attn_k.shape[2]
    assert S % ATTN_TILE == 0 and Lb == WINDOW_MAX and (B * S) % ROW_TILE == 0 and (Bd * T) % ROW_TILE == 0
    alpha = (2 * depth) ** 0.25
    n_p, n_s = B * S, Bd * T
    n_total = n_p + n_s

    w_in_b = w_in[0].astype(_BF16)
    w_out_b = w_out[0].astype(_BF16)
    pool_w_b = pool_w[0].astype(_BF16)
    wr_f32 = w_router[0].T
    wr_hi = wr_f32.astype(_BF16)
    wr_t = jnp.concatenate([wr_hi, (wr_f32 - wr_hi.astype(_F32)).astype(_BF16)], axis=0)
    br = b_router[0].reshape(N_EXPERTS, 1)

    q, k, v, u, kt, vt = _inproj_prompt(x_prompt, w_in_b)
    attn = _attn_prompt(q, k, v, _attn_bias_tables())
    h1_p, hp_p, idx_p, gate_p, rank_p, cnt_p = _mix_prompt(
        alpha, x_prompt, attn, u, w_out_b, pool_w_b, pool_scale, ln1_g, ln1_b, wr_t, br)
    n_keep = kt.shape[-1]
    k_prompt = kt.reshape(1, B, N_HEADS, HEAD_DIM, n_keep).transpose(0, 1, 4, 2, 3)
    v_prompt = vt.reshape(1, B, N_HEADS, HEAD_DIM, n_keep).transpose(0, 1, 4, 2, 3)
    pool_prompt = u[:, S - POOL_STATE_LEN:][None]

    hs = _inproj_sample(x_sample.reshape(n_s, D), w_in_b)
    qs = hs[:, :ATTN_WIDTH].reshape(Bd, T, ATTN_WIDTH)
    ks = hs[:, ATTN_WIDTH:2 * ATTN_WIDTH].reshape(Bd, T, ATTN_WIDTH)
    vs = hs[:, 2 * ATTN_WIDTH:3 * ATTN_WIDTH].reshape(Bd, T, ATTN_WIDTH)
    us = hs[:, 3 * ATTN_WIDTH:].reshape(Bd, T, POOL_WIDTH)
    to_t = lambda a: a.reshape(Bd, T, N_HEADS, HEAD_DIM).transpose(0, 2, 3, 1)
    kc = cache_attn_k[0].transpose(0, 2, 3, 1)
    vc = cache_attn_v[0].transpose(0, 2, 3, 1)
    attn_s, k_new, v_new = _attn_sample(qs, ks, vs, to_t(ks), to_t(vs), kc, vc, _decode_tables(T, Lb))
    k_sample = k_new.transpose(0, 3, 1, 2)[None]
    v_sample = v_new.transpose(0, 3, 1, 2)[None]
    uext_tm = jnp.concatenate([state_pool[0].transpose(1, 0, 2), us.transpose(1, 0, 2)], axis=0)
    pool_sample = uext_tm[T:].transpose(1, 0, 2)[None]
    x_tm = x_sample.transpose(1, 0, 2).reshape(n_s, D)
    attn_tm = attn_s.transpose(1, 0, 2).reshape(n_s, ATTN_WIDTH)
    h1_s, hp_s, idx_s, gate_s, rank_s, counts = _mix_sample(
        alpha, T, x_tm, attn_tm, uext_tm, w_out_b, pool_w_b, pool_scale, ln1_g, ln1_b, wr_t, br, cnt_p)

    idx_all = jnp.concatenate([idx_p, idx_s], axis=1)
    rank_all = jnp.concatenate([rank_p, rank_s], axis=1)
    gate_rows = jnp.concatenate([gate_p, gate_s], axis=1).T
    dest_tiles, block_e, n_active, n_valid, expert_seq, expert_next, zero_flag, n_rows = _moe_layout(
        idx_all, rank_all, counts, n_total)
    xs = _dispatch(zero_flag, dest_tiles, hp_p, hp_s, n_rows)
    ys = _experts(block_e, n_active, n_valid, expert_seq, expert_next, xs,
                  w_up[0], b_up[0][:, None, :], w_down[0], b_down[0][:, None, :])
    out_p = _combine(alpha, 0, dest_tiles, ys, h1_p, gate_rows, ln2_g, ln2_b)
    out_s = _combine(alpha, n_p // ROW_TILE, dest_tiles, ys, h1_s, gate_rows, ln2_g, ln2_b)

    y_prompt = out_p.reshape(B, S, D)
    y_sample = out_s.reshape(T, Bd, D).transpose(1, 0, 2)
    return (y_prompt, y_sample, k_prompt, v_prompt, pool_prompt, k_sample, v_sample, pool_sample)
```

```python
import functools

import jax
import jax.numpy as jnp
import numpy as np
from jax import lax
from jax.experimental import pallas as pl
from jax.experimental.pallas import tpu as pltpu

HEAD_DIM = 64
N_HEADS = 8
ATTN_WIDTH = N_HEADS * HEAD_DIM
DILATIONS = (1, 4, 16)
WINDOWS = (128, 512, 2048)
N_BACK = 128
WINDOW_MAX = 2048
ATTN_BLOCK = 128
POOL_SIZES = (2, 4, 8, 16)
POOL_GROUP_DIM = 128
POOL_WIDTH = 512
POOL_STATE_LEN = 15
N_EXPERTS = 32
TOP_K = 4
SWIGLU_ALPHA = 1.702
SWIGLU_LIMIT = 7.0
LN_EPS = 1e-5
NEG_INF = -1e30

LANES = 128
HEADS_PER_CHUNK = LANES // HEAD_DIM
N_CHUNKS = ATTN_WIDTH // LANES
VMEM_LIMIT = 56 * 1024 * 1024

ATTN_TILE = 2048
BLOCK_UNROLL = 16
INPROJ_TILE = 1024
PROJ_TILE = 512
MOE_BLOCK = 512
EXPERT_ROW_STEPS = 4
ROW_TILE = 128

_F32 = jnp.float32
_BF16 = jnp.bfloat16


def _alibi_slopes():
    return 2.0 ** (-8.0 * np.arange(1, N_HEADS + 1, dtype=np.float64) / N_HEADS)


def _params(sem, **kw):
    return pltpu.CompilerParams(dimension_semantics=sem, vmem_limit_bytes=VMEM_LIMIT, **kw)


def _inproj_body(n_keep_tiles, x_ref, w_ref, q_ref, k_ref, v_ref, u_ref, kt_ref, vt_ref):
    i = pl.program_id(1)
    n_tiles = pl.num_programs(1)
    h = jnp.dot(x_ref[...].astype(_BF16), w_ref[...], preferred_element_type=_F32)
    for j in range(N_CHUNKS):
        q_ref[j] = h[:, j * LANES:(j + 1) * LANES]
        k_ref[j] = h[:, ATTN_WIDTH + j * LANES:ATTN_WIDTH + (j + 1) * LANES]
        v_ref[j] = h[:, 2 * ATTN_WIDTH + j * LANES:2 * ATTN_WIDTH + (j + 1) * LANES]
    u_ref[...] = h[:, 3 * ATTN_WIDTH:]

    @pl.when(i >= n_tiles - n_keep_tiles)
    def _():
        kt_ref[...] = h[:, ATTN_WIDTH:2 * ATTN_WIDTH].T
        vt_ref[...] = h[:, 2 * ATTN_WIDTH:3 * ATTN_WIDTH].T


def _inproj_prompt(x, w_in_bf16):
    B, S, D = x.shape
    tm = INPROJ_TILE
    n_tiles = S // tm
    n_keep = min(WINDOW_MAX, S)
    n_keep_tiles = n_keep // tm
    first_keep = n_tiles - n_keep_tiles
    chunked = jax.ShapeDtypeStruct((B, N_CHUNKS, S, LANES), _F32)
    chunk_spec = pl.BlockSpec((None, N_CHUNKS, tm, LANES), lambda b, i: (b, 0, i, 0))
    t_spec = pl.BlockSpec((None, ATTN_WIDTH, tm), lambda b, i: (b, 0, jnp.maximum(i - first_keep, 0)))
    return pl.pallas_call(
        functools.partial(_inproj_body, n_keep_tiles),
        grid=(B, n_tiles),
        in_specs=[pl.BlockSpec((None, tm, D), lambda b, i: (b, i, 0)),
                  pl.BlockSpec(w_in_bf16.shape, lambda b, i: (0, 0))],
        out_specs=[chunk_spec, chunk_spec, chunk_spec,
                   pl.BlockSpec((None, tm, POOL_WIDTH), lambda b, i: (b, i, 0)),
                   t_spec, t_spec],
        out_shape=[chunked, chunked, chunked,
                   jax.ShapeDtypeStruct((B, S, POOL_WIDTH), _F32),
                   jax.ShapeDtypeStruct((B, ATTN_WIDTH, n_keep), _F32),
                   jax.ShapeDtypeStruct((B, ATTN_WIDTH, n_keep), _F32)],
        compiler_params=_params(("arbitrary", "arbitrary")),
        name="inproj_prompt",
    )(x, w_in_bf16)


def _attn_bias_tables():
    qi = np.arange(ATTN_BLOCK)[:, None]
    kj = np.arange(2 * ATTN_BLOCK)[None, :]
    step = qi - kj + ATTN_BLOCK
    valid = (step >= 0) & (step <= N_BACK)
    slopes = _alibi_slopes()
    out = np.zeros((N_CHUNKS, len(DILATIONS), HEADS_PER_CHUNK * ATTN_BLOCK, 2 * ATTN_BLOCK), np.float32)
    for c in range(N_CHUNKS):
        for p, dil in enumerate(DILATIONS):
            for hh in range(HEADS_PER_CHUNK):
                bias = -slopes[c * HEADS_PER_CHUNK + hh] * (step * dil).astype(np.float64)
                out[c, p, hh * ATTN_BLOCK:(hh + 1) * ATTN_BLOCK] = np.where(valid, bias, NEG_INF)
    return jnp.asarray(out)


def _attn_body(q_ref, k_ref, v_ref, bias_ref, o_ref,
               qf1, kf1, vf1, qf4, kf4, vf4, qf16, kf16, vf16, acc_n, acc_m, acc_d):
    i = pl.program_id(2)
    T = ATTN_TILE
    folded = ((1, qf1, kf1, vf1), (4, qf4, kf4, vf4), (16, qf16, kf16, vf16))
    scale = HEAD_DIM ** -0.5

    for dil, _, kf, vf in folded:
        L = T // dil

        @pl.when(i == 0)
        def _():
            kf[:, 0:ATTN_BLOCK, :] = jnp.zeros((dil, ATTN_BLOCK, LANES), _BF16)
            vf[:, 0:ATTN_BLOCK, :] = jnp.zeros((dil, ATTN_BLOCK, LANES), _BF16)

        @pl.when(i > 0)
        def _():
            kf[:, 0:ATTN_BLOCK, :] = kf[:, L:L + ATTN_BLOCK, :]
            vf[:, 0:ATTN_BLOCK, :] = vf[:, L:L + ATTN_BLOCK, :]

    for dil, qf, kf, vf in folded:
        L = T // dil
        for r in range(dil):
            rows = pl.ds(r, L, stride=dil) if dil > 1 else pl.ds(0, L)
            qf[r] = (q_ref[rows, :] * scale).astype(_BF16)
            kf[r, ATTN_BLOCK:ATTN_BLOCK + L, :] = k_ref[rows, :].astype(_BF16)
            vf[r, ATTN_BLOCK:ATTN_BLOCK + L, :] = v_ref[rows, :].astype(_BF16)

    lane = lax.broadcasted_iota(jnp.int32, (ATTN_BLOCK, LANES), 1)
    first_head = lane < HEAD_DIM
    col = lax.broadcasted_iota(jnp.int32, (HEADS_PER_CHUNK * ATTN_BLOCK, 2 * ATTN_BLOCK), 1)
    prev_half = col < ATTN_BLOCK

    for p, (dil, qf, kf, vf) in enumerate(folded):
        blocks_per_res = T // dil // ATTN_BLOCK

        def block(blk, carry, p=p, dil=dil, qf=qf, kf=kf, vf=vf, blocks_per_res=blocks_per_res):
            r = blk // blocks_per_res
            c = blk % blocks_per_res
            row0 = pl.multiple_of(c * ATTN_BLOCK, ATTN_BLOCK)
            q = qf[r, pl.ds(row0, ATTN_BLOCK), :]
            kk = kf[r, pl.ds(row0, 2 * ATTN_BLOCK), :]
            vv = vf[r, pl.ds(row0, 2 * ATTN_BLOCK), :]
            zero = jnp.zeros_like(q)
            qm = jnp.concatenate([jnp.where(first_head, q, zero), jnp.where(first_head, zero, q)], axis=0)
            s = lax.dot_general(qm, kk, (((1,), (1,)), ((), ())), preferred_element_type=_F32)
            s = s + bias_ref[p]
            s = jnp.where(jnp.logical_and(prev_half, jnp.logical_and(i == 0, c == 0)), NEG_INF, s)
            m = jnp.max(s, axis=-1, keepdims=True)
            e = jnp.exp(s - m)
            den = jnp.sum(e, axis=-1, keepdims=True)
            pv = jnp.dot(e.astype(_BF16), vv, preferred_element_type=_F32)
            num = jnp.where(first_head, pv[:ATTN_BLOCK], pv[ATTN_BLOCK:])
            m2 = jnp.where(first_head, m[:ATTN_BLOCK], m[ATTN_BLOCK:])
            d2 = jnp.where(first_head, den[:ATTN_BLOCK], den[ATTN_BLOCK:])
            start = r + row0 * dil
            rows = pl.ds(start, ATTN_BLOCK, stride=dil) if dil > 1 else pl.ds(start, ATTN_BLOCK)
            acc_n[p, rows, :] = num
            acc_m[p, rows, :] = m2
            acc_d[p, rows, :] = d2
            return carry

        lax.fori_loop(0, T // ATTN_BLOCK, block, 0, unroll=BLOCK_UNROLL)

    chunk_rows = 256

    def merge(t, carry):
        rows = pl.ds(pl.multiple_of(t * chunk_rows, chunk_rows), chunk_rows)
        m0, m1, m2 = acc_m[0, rows, :], acc_m[1, rows, :], acc_m[2, rows, :]
        mx = jnp.maximum(jnp.maximum(m0, m1), m2)
        e0, e1, e2 = jnp.exp(m0 - mx), jnp.exp(m1 - mx), jnp.exp(m2 - mx)
        num = acc_n[0, rows, :] * e0 + acc_n[1, rows, :] * e1 + acc_n[2, rows, :] * e2
        den = acc_d[0, rows, :] * e0 + acc_d[1, rows, :] * e1 + acc_d[2, rows, :] * e2
        o_ref[rows, :] = (num / den).astype(o_ref.dtype)
        return carry

    lax.fori_loop(0, T // chunk_rows, merge, 0)


def _attn_prompt(q, k, v, bias):
    B, _, S, _ = q.shape
    T = ATTN_TILE
    io_spec = pl.BlockSpec((None, None, T, LANES), lambda b, c, i: (b, c, i, 0))
    scratch = []
    for dil in DILATIONS:
        L = T // dil
        scratch += [pltpu.VMEM((dil, L, LANES), _BF16),
                    pltpu.VMEM((dil, ATTN_BLOCK + L, LANES), _BF16),
                    pltpu.VMEM((dil, ATTN_BLOCK + L, LANES), _BF16)]
    scratch += [pltpu.VMEM((len(DILATIONS), T, LANES), _F32)] * 3
    return pl.pallas_call(
        _attn_body,
        grid=(B, N_CHUNKS, S // T),
        in_specs=[io_spec, io_spec, io_spec,
                  pl.BlockSpec((None,) + bias.shape[1:], lambda b, c, i: (c, 0, 0, 0))],
        out_specs=io_spec,
        out_shape=jax.ShapeDtypeStruct((B, N_CHUNKS, S, LANES), _BF16),
        scratch_shapes=scratch,
        compiler_params=_params(("arbitrary", "arbitrary", "arbitrary")),
        name="attn_prompt",
    )(q, k, v, bias)


def _layer_norm(z, g, b):
    mu = jnp.mean(z, axis=-1, keepdims=True)
    zc = z - mu
    var = jnp.mean(zc * zc, axis=-1, keepdims=True)
    return zc * lax.rsqrt(var + LN_EPS) * g + b


def _pack_bf16_pairs(h):
    w = h.shape[1] // 2
    bits = pltpu.bitcast(h.astype(_BF16).astype(_F32), jnp.uint32)
    return (bits[:, w:] & jnp.uint32(0xFFFF0000)) | (bits[:, :w] >> 16)


def _unpack_bf16_pairs(words):
    lo = pltpu.bitcast(words << 16, _F32)
    hi = pltpu.bitcast(words & jnp.uint32(0xFFFF0000), _F32)
    return jnp.concatenate([lo, hi], axis=1).astype(_BF16)


def _route(h1, wr_ref, br_ref, cnt_ref, idx_ref, gate_ref, rank_ref):
    tm = h1.shape[0]
    nt = lambda a, b: lax.dot_general(a, b, (((1,), (1,)), ((), ())), preferred_element_type=_F32)
    h_hi = h1.astype(_BF16)
    h_lo = (h1 - h_hi.astype(_F32)).astype(_BF16)
    by_hi, by_lo = nt(wr_ref[...], h_hi), nt(wr_ref[...], h_lo)
    logits = (by_hi[:N_EXPERTS] + by_hi[N_EXPERTS:]) + (by_lo[:N_EXPERTS] + by_lo[N_EXPERTS:]) + br_ref[...]
    eid = lax.broadcasted_iota(jnp.int32, (N_EXPERTS, tm), 0)
    work = logits
    vals, ids, hots = [], [], []
    for _ in range(TOP_K):
        mx = jnp.max(work, axis=0, keepdims=True)
        sel = jnp.min(jnp.where(work == mx, eid, N_EXPERTS), axis=0, keepdims=True)
        hot = eid == sel
        vals.append(mx)
        ids.append(sel)
        hots.append(hot)
        work = jnp.where(hot, -jnp.inf, work)
    ex = [jnp.exp(v - vals[0]) for v in vals]
    tot = ex[0] + ex[1] + ex[2] + ex[3]
    any_hot = jnp.where(hots[0] | hots[1] | hots[2] | hots[3], 1.0, 0.0)
    earlier = (lax.broadcasted_iota(jnp.int32, (tm, tm), 0) < lax.broadcasted_iota(jnp.int32, (tm, tm), 1))
    prefix = jnp.dot(any_hot.astype(_BF16), jnp.where(earlier, 1.0, 0.0).astype(_BF16),
                     preferred_element_type=_F32) + cnt_ref[...]
    for kk in range(TOP_K):
        idx_ref[kk:kk + 1, :] = ids[kk]
        gate_ref[kk:kk + 1, :] = ex[kk] / tot
        rank_ref[kk:kk + 1, :] = jnp.sum(jnp.where(hots[kk], prefix, 0.0), axis=0, keepdims=True).astype(jnp.int32)
    cnt_ref[...] = cnt_ref[...] + jnp.sum(any_hot, axis=1, keepdims=True)


def _pool_mix(win_fn, u_tile, cnt_fn, pw_ref, ps_ref):
    outs = []
    for g, size in enumerate(POOL_SIZES):
        lanes = slice(g * POOL_GROUP_DIM, (g + 1) * POOL_GROUP_DIM)
        win = win_fn(0, lanes)
        for j in range(1, size):
            win = win + win_fn(j, lanes)
        d = win / cnt_fn(size) - u_tile[:, lanes]
        y = jnp.dot(d.astype(_BF16), pw_ref[g], preferred_element_type=_F32)
        outs.append(y * ps_ref[:, lanes])
    return jnp.concatenate(outs, axis=1)


def _mix_prompt_body(alpha, x_ref, a_ref, u_ref, wo_ref, pw_ref, ps_ref, g_ref, b_ref, wr_ref, br_ref,
                     h1_ref, hp_ref, idx_ref, gate_ref, rank_ref, cnt_out_ref, uext, cnt_ref):
    b = pl.program_id(0)
    i = pl.program_id(1)
    tm = x_ref.shape[0]
    halo = 16

    @pl.when(jnp.logical_and(b == 0, i == 0))
    def _():
        cnt_ref[...] = jnp.zeros_like(cnt_ref)

    @pl.when(i == 0)
    def _():
        uext[0:halo, :] = jnp.zeros((halo, POOL_WIDTH), _F32)

    @pl.when(i > 0)
    def _():
        uext[0:halo, :] = uext[tm:tm + halo, :]

    uext[halo:halo + tm, :] = u_ref[...]
    pos = i * tm + lax.broadcasted_iota(jnp.int32, (tm, 1), 0)
    pool = _pool_mix(lambda j, lanes: uext[halo - j:halo - j + tm, lanes], u_ref,
                     lambda size: jnp.minimum(pos + 1, size).astype(_F32), pw_ref, ps_ref)
    cat = jnp.concatenate([a_ref[j] for j in range(N_CHUNKS)] + [pool.astype(_BF16)], axis=1)
    mix = jnp.dot(cat, wo_ref[...], preferred_element_type=_F32)
    h1 = _layer_norm(alpha * x_ref[...] + mix, g_ref[...], b_ref[...])
    h1_ref[...] = h1
    hp_ref[...] = _pack_bf16_pairs(h1)
    _route(h1, wr_ref, br_ref, cnt_ref, idx_ref, gate_ref, rank_ref)
    cnt_out_ref[...] = cnt_ref[...]


def _mix_prompt(alpha, x, attn, u, w_out_bf16, pool_w_bf16, pool_scale, ln_g, ln_b, wr_t, br):
    B, S, D = x.shape
    n_total = B * S
    tm = PROJ_TILE
    n_tiles = S // tm
    tok = lambda b, i: (b * n_tiles + i, 0)
    tok_t = lambda b, i: (0, b * n_tiles + i)
    const2 = lambda b, i: (0, 0)
    return pl.pallas_call(
        functools.partial(_mix_prompt_body, alpha),
        grid=(B, n_tiles),
        in_specs=[pl.BlockSpec((None, tm, D), lambda b, i: (b, i, 0)),
                  pl.BlockSpec((None, N_CHUNKS, tm, LANES), lambda b, i: (b, 0, i, 0)),
                  pl.BlockSpec((None, tm, POOL_WIDTH), lambda b, i: (b, i, 0)),
                  pl.BlockSpec(w_out_bf16.shape, const2),
                  pl.BlockSpec(pool_w_bf16.shape, lambda b, i: (0, 0, 0)),
                  pl.BlockSpec(pool_scale.shape, const2),
                  pl.BlockSpec(ln_g.shape, const2), pl.BlockSpec(ln_b.shape, const2),
                  pl.BlockSpec(wr_t.shape, const2), pl.BlockSpec(br.shape, const2)],
        out_specs=[pl.BlockSpec((tm, D), tok), pl.BlockSpec((tm, D // 2), tok),
                   pl.BlockSpec((TOP_K, tm), tok_t), pl.BlockSpec((TOP_K, tm), tok_t),
                   pl.BlockSpec((TOP_K, tm), tok_t), pl.BlockSpec((N_EXPERTS, 1), const2)],
        out_shape=[jax.ShapeDtypeStruct((n_total, D), _F32), jax.ShapeDtypeStruct((n_total, D // 2), jnp.uint32),
                   jax.ShapeDtypeStruct((TOP_K, n_total), jnp.int32), jax.ShapeDtypeStruct((TOP_K, n_total), _F32),
                   jax.ShapeDtypeStruct((TOP_K, n_total), jnp.int32), jax.ShapeDtypeStruct((N_EXPERTS, 1), _F32)],
        scratch_shapes=[pltpu.VMEM((16 + tm, POOL_WIDTH), _F32), pltpu.VMEM((N_EXPERTS, 1), _F32)],
        compiler_params=_params(("arbitrary", "arbitrary")),
        name="mix_prompt",
    )(x, attn, u, w_out_bf16, pool_w_bf16, pool_scale, ln_g, ln_b, wr_t, br)


def _inproj_sample_body(x_ref, w_ref, o_ref):
    o_ref[...] = jnp.dot(x_ref[...].astype(_BF16), w_ref[...], preferred_element_type=_F32)


def _inproj_sample(x2d, w_in_bf16):
    return pl.pallas_call(
        _inproj_sample_body,
        out_shape=jax.ShapeDtypeStruct((x2d.shape[0], w_in_bf16.shape[1]), _F32),
        compiler_params=_params(None),
        name="inproj_sample",
    )(x2d, w_in_bf16)


def _decode_tables(T, Lb):
    slopes = _alibi_slopes()

    def mult(delta):
        m = np.zeros(delta.shape, np.float64)
        for win, dil in zip(WINDOWS, DILATIONS):
            m += ((delta % dil == 0) & (delta // dil <= win // dil) & (delta >= 0))
        return m

    t = np.arange(T)[:, None]
    d_cache = Lb + t - np.arange(Lb)[None, :]
    d_new = t - np.arange(T)[None, :]
    m_cache, m_new = mult(d_cache), mult(d_new)
    b_cache = np.where(m_cache > 0, -slopes[:, None, None] * d_cache[None], NEG_INF)
    b_new = np.where(m_new > 0, -slopes[:, None, None] * d_new[None], NEG_INF)
    f = lambda a: jnp.asarray(a.astype(np.float32))
    return f(b_cache), f(m_cache), f(b_new), f(m_new)


def _attn_sample_body(q_ref, kn_ref, vn_ref, knt_ref, vnt_ref, kc_ref, vc_ref, bc_ref, mc_ref, bn_ref, mn_ref,
                      o_ref, ko_ref, vo_ref):
    T = q_ref.shape[0]
    scale = HEAD_DIM ** -0.5
    rnd = lambda a: a.astype(_BF16).astype(_F32)
    new_col = lax.broadcasted_iota(jnp.int32, (T, T), 1)
    outs = []
    for h in range(N_HEADS):
        lanes = slice(h * HEAD_DIM, (h + 1) * HEAD_DIM)
        q = rnd(q_ref[:, lanes] * scale)
        kn = rnd(kn_ref[:, lanes])
        vn = rnd(vn_ref[:, lanes])
        s_c = jnp.dot(q.astype(_BF16), kc_ref[h].astype(_BF16), preferred_element_type=_F32) + bc_ref[h]
        s_n = bn_ref[h]
        for t in range(T):
            s_n = s_n + jnp.where(new_col == t, jnp.sum(q * kn[t:t + 1, :], axis=-1, keepdims=True), 0.0)
        m = jnp.maximum(jnp.max(s_c, axis=-1, keepdims=True), jnp.max(s_n, axis=-1, keepdims=True))
        e_c = mc_ref[...] * jnp.exp(s_c - m)
        e_n = mn_ref[...] * jnp.exp(s_n - m)
        den = jnp.sum(e_c, axis=-1, keepdims=True) + jnp.sum(e_n, axis=-1, keepdims=True)
        num = lax.dot_general(e_c.astype(_BF16), vc_ref[h].astype(_BF16), (((1,), (1,)), ((), ())),
                              preferred_element_type=_F32)
        e_nr = rnd(e_n)
        for t in range(T):
            num = num + e_nr[:, t:t + 1] * vn[t:t + 1, :]
        outs.append(num / den)
    o_ref[...] = jnp.concatenate(outs, axis=1)
    ko_ref[...] = jnp.concatenate([kc_ref[...][:, :, T:], knt_ref[...]], axis=-1)
    vo_ref[...] = jnp.concatenate([vc_ref[...][:, :, T:], vnt_ref[...]], axis=-1)


def _attn_sample(q, kn, vn, knt, vnt, kc, vc, tables):
    Bd, T, _ = q.shape
    Lb = kc.shape[-1]
    tok = pl.BlockSpec((None, T, ATTN_WIDTH), lambda b: (b, 0, 0))
    newt = pl.BlockSpec((None, N_HEADS, HEAD_DIM, T), lambda b: (b, 0, 0, 0))
    cache = pl.BlockSpec((None, N_HEADS, HEAD_DIM, Lb), lambda b: (b, 0, 0, 0))
    tabs = [pl.BlockSpec(t.shape, (lambda b, n=t.ndim: (0,) * n)) for t in tables]
    return pl.pallas_call(
        _attn_sample_body,
        grid=(Bd,),
        in_specs=[tok, tok, tok, newt, newt, cache, cache] + tabs,
        out_specs=[tok, cache, cache],
        out_shape=[jax.ShapeDtypeStruct((Bd, T, ATTN_WIDTH), _F32),
                   jax.ShapeDtypeStruct(kc.shape, _F32), jax.ShapeDtypeStruct(vc.shape, _F32)],
        compiler_params=_params(("arbitrary",)),
        name="attn_sample",
    )(q, kn, vn, knt, vnt, kc, vc, *tables)


def _mix_sample_body(alpha, T, x_ref, a_ref, ue_ref, wo_ref, pw_ref, ps_ref, g_ref, b_ref, wr_ref, br_ref, cnt_in_ref,
                     h1_ref, hp_ref, idx_ref, gate_ref, rank_ref, cnt_out_ref, cnt_ref):
    cnt_ref[...] = cnt_in_ref[...]
    hist = POOL_STATE_LEN
    pools = []
    for t in range(T):
        pools.append(_pool_mix(lambda j, lanes, t=t: ue_ref[hist + t - j, :, lanes], ue_ref[hist + t],
                               lambda size: float(size), pw_ref, ps_ref))
    pool = jnp.concatenate(pools, axis=0)
    cat = jnp.concatenate([a_ref[...].astype(_BF16), pool.astype(_BF16)], axis=1)
    mix = jnp.dot(cat, wo_ref[...], preferred_element_type=_F32)
    h1 = _layer_norm(alpha * x_ref[...] + mix, g_ref[...], b_ref[...])
    h1_ref[...] = h1
    hp_ref[...] = _pack_bf16_pairs(h1)
    _route(h1, wr_ref, br_ref, cnt_ref, idx_ref, gate_ref, rank_ref)
    cnt_out_ref[...] = cnt_ref[...]


def _mix_sample(alpha, T, x_tm, attn_tm, uext_tm, w_out_bf16, pool_w_bf16, pool_scale, ln_g, ln_b, wr_t, br, cnt_in):
    n_s, D = x_tm.shape
    full = lambda shape: pl.BlockSpec(shape, lambda i, n=len(shape): (0,) * n)
    ins = [x_tm, attn_tm, uext_tm, w_out_bf16, pool_w_bf16, pool_scale, ln_g, ln_b, wr_t, br, cnt_in]
    outs = [jax.ShapeDtypeStruct((n_s, D), _F32), jax.ShapeDtypeStruct((n_s, D // 2), jnp.uint32),
            jax.ShapeDtypeStruct((TOP_K, n_s), jnp.int32), jax.ShapeDtypeStruct((TOP_K, n_s), _F32),
            jax.ShapeDtypeStruct((TOP_K, n_s), jnp.int32), jax.ShapeDtypeStruct((N_EXPERTS, 1), _F32)]
    return pl.pallas_call(
        functools.partial(_mix_sample_body, alpha, T),
        grid=(1,),
        in_specs=[full(a.shape) for a in ins],
        out_specs=[full(o.shape) for o in outs],
        out_shape=outs,
        scratch_shapes=[pltpu.VMEM((N_EXPERTS, 1), _F32)],
        compiler_params=_params(("arbitrary",)),
        name="mix_sample",
    )(*ins)


def _dispatch_body(n_p_tiles, zflag_ref, dest_hbm, hp_p_ref, hp_s_ref, xs_hbm, tile, zeros, dest_smem,
                   sem_idx, sem_rows, sem_zero):
    i = pl.program_id(0)
    n_steps = pl.num_programs(0)
    tm = tile.shape[1]

    @pl.when(i == 0)
    def _():
        zeros[...] = jnp.zeros_like(zeros)

        blk = zeros.shape[0]

        def zero_copy(j):
            return pltpu.make_async_copy(zeros, xs_hbm.at[pl.ds(pl.multiple_of(j * blk, blk), blk)], sem_zero)

        def start(j, carry):
            @pl.when(zflag_ref[j] != 0)
            def _():
                zero_copy(j).start()
            return carry

        def wait(j, carry):
            @pl.when(zflag_ref[j] != 0)
            def _():
                zero_copy(j).wait()
            return carry

        lax.fori_loop(0, zflag_ref.shape[0], start, 0)
        lax.fori_loop(0, zflag_ref.shape[0], wait, 0)

    def idx_copy(t, s):
        return pltpu.make_async_copy(dest_hbm.at[t], dest_smem.at[pl.ds(s * TOP_K * tm, TOP_K * tm)], sem_idx.at[s])

    def wait_rows(s):
        for _ in range(TOP_K):
            pltpu.make_async_copy(tile.at[s], xs_hbm.at[pl.ds(0, tm)], sem_rows.at[s]).wait()

    @pl.when(i == 0)
    def _():
        idx_copy(i, 0).start()

    staged = jnp.where(i < n_p_tiles, hp_p_ref[...], hp_s_ref[...])

    for s in range(2):
        @pl.when(i % 2 == s)
        def _(s=s):
            @pl.when(i + 1 < n_steps)
            def _():
                idx_copy(i + 1, 1 - s).start()

            tile[s] = staged
            idx_copy(i, s).wait()

            for n in range(tm):
                for kk in range(TOP_K):
                    pltpu.make_async_copy(tile.at[s, pl.ds(n, 1)], xs_hbm.at[pl.ds(dest_smem[(s * TOP_K + kk) * tm + n], 1)],
                                          sem_rows.at[s]).start(priority=kk % 2)

            @pl.when(i > 0)
            def _():
                wait_rows(1 - s)

            @pl.when(i == n_steps - 1)
            def _():
                wait_rows(s)


def _dispatch(zero_flag, dest_tiles, hp_p, hp_s, n_rows):
    w = hp_p.shape[1]
    tm = ROW_TILE
    n_p_tiles, n_s_tiles = hp_p.shape[0] // tm, hp_s.shape[0] // tm
    grid_spec = pltpu.PrefetchScalarGridSpec(
        num_scalar_prefetch=1,
        grid=(n_p_tiles + n_s_tiles,),
        in_specs=[pl.BlockSpec(memory_space=pl.ANY),
                  pl.BlockSpec((tm, w), lambda i, z: (jnp.minimum(i, n_p_tiles - 1), 0)),
                  pl.BlockSpec((tm, w), lambda i, z: (jnp.maximum(i - n_p_tiles, 0), 0))],
        out_specs=pl.BlockSpec(memory_space=pl.ANY),
        scratch_shapes=[pltpu.VMEM((2, tm, w), jnp.uint32), pltpu.VMEM((MOE_BLOCK, w), jnp.uint32),
                        pltpu.SMEM((2 * TOP_K * tm,), jnp.int32),
                        pltpu.SemaphoreType.DMA((2,)), pltpu.SemaphoreType.DMA((2,)), pltpu.SemaphoreType.DMA(())],
    )
    return pl.pallas_call(
        functools.partial(_dispatch_body, n_p_tiles),
        grid_spec=grid_spec,
        out_shape=jax.ShapeDtypeStruct((n_rows, w), jnp.uint32),
        compiler_params=_params(("arbitrary",)),
        name="moe_dispatch",
    )(zero_flag, dest_tiles, hp_p, hp_s)


def _experts_body(be_ref, na_ref, nv_ref, eseq_ref, enext_ref, xs_ref, wu_hbm, bu_ref, wd_hbm, bd_ref, ys_ref,
                  wu_bf, wd_bf, wu_f32, wd_f32, sem_w):
    j = pl.program_id(0)
    blk = xs_ref.shape[0]
    d_ff = wd_bf.shape[0]
    active = j < na_ref[0]
    changed = jnp.logical_or(j == 0, be_ref[j] != be_ref[jnp.maximum(j - 1, 0)])

    def weight_copies(e, s):
        return (pltpu.make_async_copy(wu_hbm.at[e], wu_f32.at[s], sem_w.at[s]),
                pltpu.make_async_copy(wd_hbm.at[e], wd_f32.at[s], sem_w.at[s]))

    @pl.when(j == 0)
    def _():
        for c in weight_copies(be_ref[0], 0):
            c.start()

    for s in range(2):
        @pl.when(jnp.logical_and(jnp.logical_and(active, changed), eseq_ref[j] % 2 == s))
        def _(s=s):
            for c in weight_copies(be_ref[j], s):
                c.wait()

            @pl.when(enext_ref[j] >= 0)
            def _():
                for c in weight_copies(enext_ref[j], 1 - s):
                    c.start()

            wu_bf[...] = wu_f32[s].astype(_BF16)
            wd_bf[...] = wd_f32[s].astype(_BF16)

    def ffn(rows):
        x = _unpack_bf16_pairs(xs_ref[0:rows, :])
        h = jnp.dot(x, wu_bf[...], preferred_element_type=_F32) + bu_ref[...]
        g = jnp.minimum(h[:, :d_ff], SWIGLU_LIMIT)
        lin = jnp.clip(h[:, d_ff:], -SWIGLU_LIMIT, SWIGLU_LIMIT)
        act = g * jax.nn.sigmoid(SWIGLU_ALPHA * g) * (lin + 1.0)
        y = jnp.dot(act.astype(_BF16), wd_bf[...], preferred_element_type=_F32) + bd_ref[...]
        ys_ref[0:rows, :] = _pack_bf16_pairs(y)
        if rows < blk:
            ys_ref[rows:, :] = jnp.zeros((blk - rows, ys_ref.shape[1]), ys_ref.dtype)

    quarter = blk // EXPERT_ROW_STEPS
    n_quarters = (nv_ref[j] + quarter - 1) // quarter
    for q in range(1, EXPERT_ROW_STEPS + 1):
        @pl.when(jnp.logical_and(active, n_quarters == q))
        def _(q=q):
            ffn(q * quarter)

    @pl.when(jnp.logical_or(jnp.logical_not(active), n_quarters == 0))
    def _():
        ys_ref[...] = jnp.zeros_like(ys_ref)


def _experts(block_e, n_active, n_valid, expert_seq, expert_next, xs, w_up, b_up, w_down, b_down):
    n_rows, w = xs.shape
    blk = MOE_BLOCK
    n_blocks = n_rows // blk
    d_model, d_ff2 = w_up.shape[1:]
    d_ff = w_down.shape[1]
    act_blk = lambda j, be, na, *_: jnp.minimum(j, jnp.maximum(na[0] - 1, 0))
    grid_spec = pltpu.PrefetchScalarGridSpec(
        num_scalar_prefetch=5,
        grid=(n_blocks,),
        in_specs=[pl.BlockSpec((blk, w), lambda j, be, na, *_: (act_blk(j, be, na), 0)),
                  pl.BlockSpec(memory_space=pl.ANY),
                  pl.BlockSpec((None, 1, d_ff2), lambda j, be, *_: (be[j], 0, 0)),
                  pl.BlockSpec(memory_space=pl.ANY),
                  pl.BlockSpec((None, 1, d_model), lambda j, be, *_: (be[j], 0, 0))],
        out_specs=pl.BlockSpec((blk, d_model // 2), lambda j, *_: (j, 0)),
        scratch_shapes=[pltpu.VMEM((d_model, d_ff2), _BF16), pltpu.VMEM((d_ff, d_model), _BF16),
                        pltpu.VMEM((2, d_model, d_ff2), _F32), pltpu.VMEM((2, d_ff, d_model), _F32),
                        pltpu.SemaphoreType.DMA((2,))],
    )
    return pl.pallas_call(
        _experts_body,
        grid_spec=grid_spec,
        out_shape=jax.ShapeDtypeStruct((n_rows, d_model // 2), jnp.uint32),
        compiler_params=_params(("arbitrary",)),
        name="moe_experts",
    )(block_e, n_active, n_valid, expert_seq, expert_next, xs, w_up, b_up, w_down, b_down)


def _combine_body(alpha, tile0, dest_hbm, ys_hbm, h1_ref, gate_ref, g_ref, b_ref, o_ref,
                  dest_smem, buf, sem_idx, sem_rows):
    i = pl.program_id(0)
    n_steps = pl.num_programs(0)
    tm = o_ref.shape[0]
    group = 32

    def idx_copy(t, s):
        return pltpu.make_async_copy(dest_hbm.at[tile0 + t], dest_smem.at[pl.ds(s * TOP_K * tm, TOP_K * tm)],
                                     sem_idx.at[s])

    def gather_tile(s):
        for n in range(tm):
            for kk in range(TOP_K):
                row = dest_smem[(s * TOP_K + kk) * tm + n]
                pltpu.make_async_copy(ys_hbm.at[pl.ds(row, 1)], buf.at[s, kk, pl.ds(n, 1)],
                                      sem_rows.at[s]).start(priority=kk % 2)

    def wait_rows(s):
        for kk in range(TOP_K):
            pltpu.make_async_copy(ys_hbm.at[pl.ds(0, tm)], buf.at[s, kk], sem_rows.at[s]).wait()

    def reduce_rows(s, r0):
        rows = pl.ds(r0, group)
        lo = hi = None
        for kk in range(TOP_K):
            words = buf[s, kk, rows, :]
            gate = gate_ref[rows, kk:kk + 1]
            t_lo = gate * pltpu.bitcast(words << 16, _F32)
            t_hi = gate * pltpu.bitcast(words & jnp.uint32(0xFFFF0000), _F32)
            lo, hi = (t_lo, t_hi) if kk == 0 else (lo + t_lo, hi + t_hi)
        o_ref[rows, :] = _layer_norm(alpha * h1_ref[rows, :] + jnp.concatenate([lo, hi], axis=1), g_ref[...], b_ref[...])

    def row_groups(fn, unroll=1):
        def body(gi, carry):
            fn(pl.multiple_of(gi * group, group))
            return carry
        lax.fori_loop(0, tm // group, body, 0, unroll=unroll)

    @pl.when(i == 0)
    def _():
        idx_copy(i, 0).start()
        idx_copy(i, 0).wait()
        gather_tile(0)

        @pl.when(n_steps > 1)
        def _():
            idx_copy(i + 1, 1).start()

    for s in range(2):
        @pl.when(i % 2 == s)
        def _(s=s):
            @pl.when(i + 1 < n_steps)
            def _():
                idx_copy(i + 1, 1 - s).wait()
                gather_tile(1 - s)

            @pl.when(i + 2 < n_steps)
            def _():
                idx_copy(i + 2, s).start()

            wait_rows(s)
            row_groups(lambda r0: reduce_rows(s, r0), unroll=tm // group)


def _combine(alpha, tile0, dest_tiles, ys, h1, gate_rows, ln_g, ln_b):
    n_tok, d_model = h1.shape
    tm = ROW_TILE
    const2 = lambda i: (0, 0)
    return pl.pallas_call(
        functools.partial(_combine_body, alpha, tile0),
        grid=(n_tok // tm,),
        in_specs=[pl.BlockSpec(memory_space=pl.ANY), pl.BlockSpec(memory_space=pl.ANY),
                  pl.BlockSpec((tm, d_model), lambda i: (i, 0)),
                  pl.BlockSpec((tm, TOP_K), lambda i: (tile0 + i, 0)),
                  pl.BlockSpec(ln_g.shape, const2), pl.BlockSpec(ln_b.shape, const2)],
        out_specs=pl.BlockSpec((tm, d_model), lambda i: (i, 0)),
        out_shape=jax.ShapeDtypeStruct(h1.shape, _F32),
        scratch_shapes=[pltpu.SMEM((2 * TOP_K * tm,), jnp.int32), pltpu.VMEM((2, TOP_K, tm, ys.shape[1]), ys.dtype),
                        pltpu.SemaphoreType.DMA((2,)), pltpu.SemaphoreType.DMA((2,))],
        compiler_params=_params(("arbitrary",)),
        name="moe_combine_tile%d" % tile0,
    )(dest_tiles, ys, h1, gate_rows, ln_g, ln_b)


def _moe_layout(idx_all, rank_all, counts, n_total):
    blk = MOE_BLOCK
    n_blocks = (TOP_K * n_total + N_EXPERTS * (blk - 1)) // blk + 1
    cnt = counts.reshape(N_EXPERTS).astype(jnp.int32)
    padded = (cnt + blk - 1) // blk * blk
    pend = jnp.cumsum(padded)
    pstart = pend - padded
    experts = jnp.arange(N_EXPERTS, dtype=jnp.int32)
    dest = rank_all + jnp.sum(jnp.where(idx_all[None] == experts[:, None, None], pstart[:, None, None], 0), axis=0)
    block_row0 = jnp.arange(n_blocks, dtype=jnp.int32) * blk
    block_e = jnp.minimum(jnp.sum((pend[None, :] <= block_row0[:, None]).astype(jnp.int32), axis=1), N_EXPERTS - 1)
    n_active = (pend[-1] // blk).astype(jnp.int32).reshape(1)
    last_e = block_e[jnp.maximum(n_active[0] - 1, 0)]
    block_e = jnp.where(jnp.arange(n_blocks) < n_active[0], block_e, last_e)
    tm = ROW_TILE
    dest_tiles = dest.reshape(TOP_K, n_total // tm, tm).transpose(1, 0, 2).reshape(n_total // tm, TOP_K * tm)
    blocks = jnp.arange(n_blocks, dtype=jnp.int32)
    is_last = jnp.any((pend[None, :] == block_row0[:, None] + blk) & (cnt[None, :] > 0), axis=1)
    zero_flag = (is_last | (blocks >= n_active[0])).astype(jnp.int32)
    is_e = block_e[:, None] == experts[None, :]
    rows_left = jnp.sum(jnp.where(is_e, (pstart + cnt)[None, :], 0), axis=1) - block_row0
    n_valid = jnp.where(blocks < n_active[0], jnp.clip(rows_left, 0, blk), 0).astype(jnp.int32)
    switch = (blocks < n_active[0]) & jnp.concatenate([jnp.ones((1,), bool), block_e[1:] != block_e[:-1]])
    expert_seq = (jnp.cumsum(switch.astype(jnp.int32)) - 1).astype(jnp.int32)
    switch_at = jnp.where(switch, blocks, n_blocks)
    next_switch = jnp.concatenate([lax.cummin(switch_at[::-1])[::-1][1:], jnp.full((1,), n_blocks, jnp.int32)])
    next_e = jnp.sum(jnp.where(blocks[None, :] == next_switch[:, None], block_e[None, :], 0), axis=1)
    expert_next = jnp.where(next_switch < n_blocks, next_e, -1).astype(jnp.int32)
    return dest_tiles, block_e, n_active, n_valid, expert_seq, expert_next, zero_flag, n_blocks * blk


def kernel(x_prompt, x_sample, cache_attn_k, cache_attn_v, state_pool, w_in, w_out, pool_w, pool_scale, ln1_g, ln1_b,
           w_router, b_router, w_up, b_up, w_down, b_down, ln2_g, ln2_b):
    depth = w_in.shape[0]
    assert depth == 1, "single-layer step"
    B, S, D = x_prompt.shape
    Bd, T, _ = x_sample.shape
    Lb = cache_attn_k.shape[2]
    assert S % ATTN_TILE == 0 and Lb == WINDOW_MAX and (B * S) % ROW_TILE == 0 and (Bd * T) % ROW_TILE == 0
    alpha = (2 * depth) ** 0.25
    n_p, n_s = B * S, Bd * T
    n_total = n_p + n_s

    w_in_b = w_in[0].astype(_BF16)
    w_out_b = w_out[0].astype(_BF16)
    pool_w_b = pool_w[0].astype(_BF16)
    wr_f32 = w_router[0].T
    wr_hi = wr_f32.astype(_BF16)
    wr_t = jnp.concatenate([wr_hi, (wr_f32 - wr_hi.astype(_F32)).astype(_BF16)], axis=0)
    br = b_router[0].reshape(N_EXPERTS, 1)

    q, k, v, u, kt, vt = _inproj_prompt(x_prompt, w_in_b)
    attn = _attn_prompt(q, k, v, _attn_bias_tables())
    h1_p, hp_p, idx_p, gate_p, rank_p, cnt_p = _mix_prompt(
        alpha, x_prompt, attn, u, w_out_b, pool_w_b, pool_scale, ln1_g, ln1_b, wr_t, br)
    n_keep = kt.shape[-1]
    k_prompt = kt.reshape(1, B, N_HEADS, HEAD_DIM, n_keep).transpose(0, 1, 4, 2, 3)
    v_prompt = vt.reshape(1, B, N_HEADS, HEAD_DIM, n_keep).transpose(0, 1, 4, 2, 3)
    pool_prompt = u[:, S - POOL_STATE_LEN:][None]

    hs = _inproj_sample(x_sample.reshape(n_s, D), w_in_b)
    qs = hs[:, :ATTN_WIDTH].reshape(Bd, T, ATTN_WIDTH)
    ks = hs[:, ATTN_WIDTH:2 * ATTN_WIDTH].reshape(Bd, T, ATTN_WIDTH)
    vs = hs[:, 2 * ATTN_WIDTH:3 * ATTN_WIDTH].reshape(Bd, T, ATTN_WIDTH)
    us = hs[:, 3 * ATTN_WIDTH:].reshape(Bd, T, POOL_WIDTH)
    to_t = lambda a: a.reshape(Bd, T, N_HEADS, HEAD_DIM).transpose(0, 2, 3, 1)
    kc = cache_attn_k[0].transpose(0, 2, 3, 1)
    vc = cache_attn_v[0].transpose(0, 2, 3, 1)
    attn_s, k_new, v_new = _attn_sample(qs, ks, vs, to_t(ks), to_t(vs), kc, vc, _decode_tables(T, Lb))
    k_sample = k_new.transpose(0, 3, 1, 2)[None]
    v_sample = v_new.transpose(0, 3, 1, 2)[None]
    uext_tm = jnp.concatenate([state_pool[0].transpose(1, 0, 2), us.transpose(1, 0, 2)], axis=0)
    pool_sample = uext_tm[T:].transpose(1, 0, 2)[None]
    x_tm = x_sample.transpose(1, 0, 2).reshape(n_s, D)
    attn_tm = attn_s.transpose(1, 0, 2).reshape(n_s, ATTN_WIDTH)
    h1_s, hp_s, idx_s, gate_s, rank_s, counts = _mix_sample(
        alpha, T, x_tm, attn_tm, uext_tm, w_out_b, pool_w_b, pool_scale, ln1_g, ln1_b, wr_t, br, cnt_p)

    idx_all = jnp.concatenate([idx_p, idx_s], axis=1)
    rank_all = jnp.concatenate([rank_p, rank_s], axis=1)
    gate_rows = jnp.concatenate([gate_p, gate_s], axis=1).T
    dest_tiles, block_e, n_active, n_valid, expert_seq, expert_next, zero_flag, n_rows = _moe_layout(
        idx_all, rank_all, counts, n_total)
    xs = _dispatch(zero_flag, dest_tiles, hp_p, hp_s, n_rows)
    ys = _experts(block_e, n_active, n_valid, expert_seq, expert_next, xs,
                  w_up[0], b_up[0][:, None, :], w_down[0], b_down[0][:, None, :])
    out_p = _combine(alpha, 0, dest_tiles, ys, h1_p, gate_rows, ln2_g, ln2_b)
    out_s = _combine(alpha, n_p // ROW_TILE, dest_tiles, ys, h1_s, gate_rows, ln2_g, ln2_b)

    y_prompt = out_p.reshape(B, S, D)
    y_sample = out_s.reshape(T, Bd, D).transpose(1, 0, 2)
    return (y_prompt, y_sample, k_prompt, v_prompt, pool_prompt, k_sample, v_sample, pool_sample)
```

```python
import functools

import jax
import jax.numpy as jnp
import numpy as np
from jax import lax
from jax.experimental import pallas as pl
from jax.experimental.pallas import tpu as pltpu

HEAD_DIM = 64
N_HEADS = 8
ATTN_WIDTH = N_HEADS * HEAD_DIM
DILATIONS = (1, 4, 16)
WINDOWS = (128, 512, 2048)
N_BACK = 128
WINDOW_MAX = 2048
ATTN_BLOCK = 128
POOL_SIZES = (2, 4, 8, 16)
POOL_GROUP_DIM = 128
POOL_WIDTH = 512
POOL_STATE_LEN = 15
N_EXPERTS = 32
TOP_K = 4
SWIGLU_ALPHA = 1.702
SWIGLU_LIMIT = 7.0
LN_EPS = 1e-5
NEG_INF = -1e30

LANES = 128
HEADS_PER_CHUNK = LANES // HEAD_DIM
N_CHUNKS = ATTN_WIDTH // LANES
VMEM_LIMIT = 56 * 1024 * 1024

ATTN_TILE = 2048
BLOCK_UNROLL = 16
INPROJ_TILE = 1024
PROJ_TILE = 1024
MOE_BLOCK = 512
EXPERT_ROW_STEPS = 4
ROW_TILE = 128

_F32 = jnp.float32
_BF16 = jnp.bfloat16


def _alibi_slopes():
    return 2.0 ** (-8.0 * np.arange(1, N_HEADS + 1, dtype=np.float64) / N_HEADS)


def _params(sem, **kw):
    return pltpu.CompilerParams(dimension_semantics=sem, vmem_limit_bytes=VMEM_LIMIT, **kw)


def _inproj_body(n_keep_tiles, x_ref, w_ref, q_ref, k_ref, v_ref, u_ref, kt_ref, vt_ref):
    i = pl.program_id(1)
    n_tiles = pl.num_programs(1)
    h = jnp.dot(x_ref[...].astype(_BF16), w_ref[...], preferred_element_type=_F32)
    for j in range(N_CHUNKS):
        q_ref[j] = h[:, j * LANES:(j + 1) * LANES]
        k_ref[j] = h[:, ATTN_WIDTH + j * LANES:ATTN_WIDTH + (j + 1) * LANES]
        v_ref[j] = h[:, 2 * ATTN_WIDTH + j * LANES:2 * ATTN_WIDTH + (j + 1) * LANES]
    u_ref[...] = h[:, 3 * ATTN_WIDTH:]

    @pl.when(i >= n_tiles - n_keep_tiles)
    def _():
        kt_ref[...] = h[:, ATTN_WIDTH:2 * ATTN_WIDTH].T
        vt_ref[...] = h[:, 2 * ATTN_WIDTH:3 * ATTN_WIDTH].T


def _inproj_prompt(x, w_in_bf16):
    B, S, D = x.shape
    tm = INPROJ_TILE
    n_tiles = S // tm
    n_keep = min(WINDOW_MAX, S)
    n_keep_tiles = n_keep // tm
    first_keep = n_tiles - n_keep_tiles
    chunked = jax.ShapeDtypeStruct((B, N_CHUNKS, S, LANES), _F32)
    chunk_spec = pl.BlockSpec((None, N_CHUNKS, tm, LANES), lambda b, i: (b, 0, i, 0))
    t_spec = pl.BlockSpec((None, ATTN_WIDTH, tm), lambda b, i: (b, 0, jnp.maximum(i - first_keep, 0)))
    return pl.pallas_call(
        functools.partial(_inproj_body, n_keep_tiles),
        grid=(B, n_tiles),
        in_specs=[pl.BlockSpec((None, tm, D), lambda b, i: (b, i, 0)),
                  pl.BlockSpec(w_in_bf16.shape, lambda b, i: (0, 0))],
        out_specs=[chunk_spec, chunk_spec, chunk_spec,
                   pl.BlockSpec((None, tm, POOL_WIDTH), lambda b, i: (b, i, 0)),
                   t_spec, t_spec],
        out_shape=[chunked, chunked, chunked,
                   jax.ShapeDtypeStruct((B, S, POOL_WIDTH), _F32),
                   jax.ShapeDtypeStruct((B, ATTN_WIDTH, n_keep), _F32),
                   jax.ShapeDtypeStruct((B, ATTN_WIDTH, n_keep), _F32)],
        compiler_params=_params(("arbitrary", "arbitrary")),
        name="inproj_prompt",
    )(x, w_in_bf16)


def _attn_bias_tables():
    qi = np.arange(ATTN_BLOCK)[:, None]
    kj = np.arange(2 * ATTN_BLOCK)[None, :]
    step = qi - kj + ATTN_BLOCK
    valid = (step >= 0) & (step <= N_BACK)
    slopes = _alibi_slopes()
    out = np.zeros((N_CHUNKS, len(DILATIONS), HEADS_PER_CHUNK * ATTN_BLOCK, 2 * ATTN_BLOCK), np.float32)
    for c in range(N_CHUNKS):
        for p, dil in enumerate(DILATIONS):
            for hh in range(HEADS_PER_CHUNK):
                bias = -slopes[c * HEADS_PER_CHUNK + hh] * (step * dil).astype(np.float64)
                out[c, p, hh * ATTN_BLOCK:(hh + 1) * ATTN_BLOCK] = np.where(valid, bias, NEG_INF)
    return jnp.asarray(out)


def _attn_body(q_ref, k_ref, v_ref, bias_ref, o_ref,
               qf1, kf1, vf1, qf4, kf4, vf4, qf16, kf16, vf16, acc_n, acc_m, acc_d):
    i = pl.program_id(2)
    T = ATTN_TILE
    folded = ((1, qf1, kf1, vf1), (4, qf4, kf4, vf4), (16, qf16, kf16, vf16))
    scale = HEAD_DIM ** -0.5

    for dil, _, kf, vf in folded:
        L = T // dil

        @pl.when(i == 0)
        def _():
            kf[:, 0:ATTN_BLOCK, :] = jnp.zeros((dil, ATTN_BLOCK, LANES), _BF16)
            vf[:, 0:ATTN_BLOCK, :] = jnp.zeros((dil, ATTN_BLOCK, LANES), _BF16)

        @pl.when(i > 0)
        def _():
            kf[:, 0:ATTN_BLOCK, :] = kf[:, L:L + ATTN_BLOCK, :]
            vf[:, 0:ATTN_BLOCK, :] = vf[:, L:L + ATTN_BLOCK, :]

    for dil, qf, kf, vf in folded:
        L = T // dil
        for r in range(dil):
            rows = pl.ds(r, L, stride=dil) if dil > 1 else pl.ds(0, L)
            qf[r] = (q_ref[rows, :] * scale).astype(_BF16)
            kf[r, ATTN_BLOCK:ATTN_BLOCK + L, :] = k_ref[rows, :].astype(_BF16)
            vf[r, ATTN_BLOCK:ATTN_BLOCK + L, :] = v_ref[rows, :].astype(_BF16)

    lane = lax.broadcasted_iota(jnp.int32, (ATTN_BLOCK, LANES), 1)
    first_head = lane < HEAD_DIM
    col = lax.broadcasted_iota(jnp.int32, (HEADS_PER_CHUNK * ATTN_BLOCK, 2 * ATTN_BLOCK), 1)
    prev_half = col < ATTN_BLOCK

    for p, (dil, qf, kf, vf) in enumerate(folded):
        blocks_per_res = T // dil // ATTN_BLOCK

        def block(blk, carry, p=p, dil=dil, qf=qf, kf=kf, vf=vf, blocks_per_res=blocks_per_res):
            r = blk // blocks_per_res
            c = blk % blocks_per_res
            row0 = pl.multiple_of(c * ATTN_BLOCK, ATTN_BLOCK)
            q = qf[r, pl.ds(row0, ATTN_BLOCK), :]
            kk = kf[r, pl.ds(row0, 2 * ATTN_BLOCK), :]
            vv = vf[r, pl.ds(row0, 2 * ATTN_BLOCK), :]
            zero = jnp.zeros_like(q)
            qm = jnp.concatenate([jnp.where(first_head, q, zero), jnp.where(first_head, zero, q)], axis=0)
            s = lax.dot_general(qm, kk, (((1,), (1,)), ((), ())), preferred_element_type=_F32)
            s = s + bias_ref[p]
            s = jnp.where(jnp.logical_and(prev_half, jnp.logical_and(i == 0, c == 0)), NEG_INF, s)
            m = jnp.max(s, axis=-1, keepdims=True)
            e = jnp.exp(s - m)
            den = jnp.sum(e, axis=-1, keepdims=True)
            pv = jnp.dot(e.astype(_BF16), vv, preferred_element_type=_F32)
            num = jnp.where(first_head, pv[:ATTN_BLOCK], pv[ATTN_BLOCK:])
            m2 = jnp.where(first_head, m[:ATTN_BLOCK], m[ATTN_BLOCK:])
            d2 = jnp.where(first_head, den[:ATTN_BLOCK], den[ATTN_BLOCK:])
            start = r + row0 * dil
            rows = pl.ds(start, ATTN_BLOCK, stride=dil) if dil > 1 else pl.ds(start, ATTN_BLOCK)
            acc_n[p, rows, :] = num
            acc_m[p, rows, :] = m2
            acc_d[p, rows, :] = d2
            return carry

        lax.fori_loop(0, T // ATTN_BLOCK, block, 0, unroll=BLOCK_UNROLL)

    chunk_rows = 256

    def merge(t, carry):
        rows = pl.ds(pl.multiple_of(t * chunk_rows, chunk_rows), chunk_rows)
        m0, m1, m2 = acc_m[0, rows, :], acc_m[1, rows, :], acc_m[2, rows, :]
        mx = jnp.maximum(jnp.maximum(m0, m1), m2)
        e0, e1, e2 = jnp.exp(m0 - mx), jnp.exp(m1 - mx), jnp.exp(m2 - mx)
        num = acc_n[0, rows, :] * e0 + acc_n[1, rows, :] * e1 + acc_n[2, rows, :] * e2
        den = acc_d[0, rows, :] * e0 + acc_d[1, rows, :] * e1 + acc_d[2, rows, :] * e2
        o_ref[rows, :] = (num / den).astype(o_ref.dtype)
        return carry

    lax.fori_loop(0, T // chunk_rows, merge, 0)


def _attn_prompt(q, k, v, bias):
    B, _, S, _ = q.shape
    T = ATTN_TILE
    io_spec = pl.BlockSpec((None, None, T, LANES), lambda b, c, i: (b, c, i, 0))
    scratch = []
    for dil in DILATIONS:
        L = T // dil
        scratch += [pltpu.VMEM((dil, L, LANES), _BF16),
                    pltpu.VMEM((dil, ATTN_BLOCK + L, LANES), _BF16),
                    pltpu.VMEM((dil, ATTN_BLOCK + L, LANES), _BF16)]
    scratch += [pltpu.VMEM((len(DILATIONS), T, LANES), _F32)] * 3
    return pl.pallas_call(
        _attn_body,
        grid=(B, N_CHUNKS, S // T),
        in_specs=[io_spec, io_spec, io_spec,
                  pl.BlockSpec((None,) + bias.shape[1:], lambda b, c, i: (c, 0, 0, 0))],
        out_specs=io_spec,
        out_shape=jax.ShapeDtypeStruct((B, N_CHUNKS, S, LANES), _BF16),
        scratch_shapes=scratch,
        compiler_params=_params(("arbitrary", "arbitrary", "arbitrary")),
        name="attn_prompt",
    )(q, k, v, bias)


def _layer_norm(z, g, b):
    mu = jnp.mean(z, axis=-1, keepdims=True)
    zc = z - mu
    var = jnp.mean(zc * zc, axis=-1, keepdims=True)
    return zc * lax.rsqrt(var + LN_EPS) * g + b


def _pack_bf16_pairs(h):
    w = h.shape[1] // 2
    bits = pltpu.bitcast(h.astype(_BF16).astype(_F32), jnp.uint32)
    return (bits[:, w:] & jnp.uint32(0xFFFF0000)) | (bits[:, :w] >> 16)


def _unpack_bf16_pairs(words):
    lo = pltpu.bitcast(words << 16, _F32)
    hi = pltpu.bitcast(words & jnp.uint32(0xFFFF0000), _F32)
    return jnp.concatenate([lo, hi], axis=1).astype(_BF16)


def _route(h1, wr_ref, br_ref, cnt_ref, idx_ref, gate_ref, rank_ref):
    tm = h1.shape[0]
    nt = lambda a, b: lax.dot_general(a, b, (((1,), (1,)), ((), ())), preferred_element_type=_F32)
    h_hi = h1.astype(_BF16)
    h_lo = (h1 - h_hi.astype(_F32)).astype(_BF16)
    by_hi, by_lo = nt(wr_ref[...], h_hi), nt(wr_ref[...], h_lo)
    logits = (by_hi[:N_EXPERTS] + by_hi[N_EXPERTS:]) + (by_lo[:N_EXPERTS] + by_lo[N_EXPERTS:]) + br_ref[...]
    eid = lax.broadcasted_iota(jnp.int32, (N_EXPERTS, tm), 0)
    work = logits
    vals, ids, hots = [], [], []
    for _ in range(TOP_K):
        mx = jnp.max(work, axis=0, keepdims=True)
        sel = jnp.min(jnp.where(work == mx, eid, N_EXPERTS), axis=0, keepdims=True)
        hot = eid == sel
        vals.append(mx)
        ids.append(sel)
        hots.append(hot)
        work = jnp.where(hot, -jnp.inf, work)
    ex = [jnp.exp(v - vals[0]) for v in vals]
    tot = ex[0] + ex[1] + ex[2] + ex[3]
    any_hot = jnp.where(hots[0] | hots[1] | hots[2] | hots[3], 1.0, 0.0)
    earlier = (lax.broadcasted_iota(jnp.int32, (tm, tm), 0) < lax.broadcasted_iota(jnp.int32, (tm, tm), 1))
    prefix = jnp.dot(any_hot.astype(_BF16), jnp.where(earlier, 1.0, 0.0).astype(_BF16),
                     preferred_element_type=_F32) + cnt_ref[...]
    for kk in range(TOP_K):
        idx_ref[kk:kk + 1, :] = ids[kk]
        gate_ref[kk:kk + 1, :] = ex[kk] / tot
        rank_ref[kk:kk + 1, :] = jnp.sum(jnp.where(hots[kk], prefix, 0.0), axis=0, keepdims=True).astype(jnp.int32)
    cnt_ref[...] = cnt_ref[...] + jnp.sum(any_hot, axis=1, keepdims=True)


def _pool_mix(win_fn, u_tile, cnt_fn, pw_ref, ps_ref):
    outs = []
    for g, size in enumerate(POOL_SIZES):
        lanes = slice(g * POOL_GROUP_DIM, (g + 1) * POOL_GROUP_DIM)
        win = win_fn(0, lanes)
        for j in range(1, size):
            win = win + win_fn(j, lanes)
        d = win / cnt_fn(size) - u_tile[:, lanes]
        y = jnp.dot(d.astype(_BF16), pw_ref[g], preferred_element_type=_F32)
        outs.append(y * ps_ref[:, lanes])
    return jnp.concatenate(outs, axis=1)


def _mix_prompt_body(alpha, x_ref, a_ref, u_ref, wo_ref, pw_ref, ps_ref, g_ref, b_ref, wr_ref, br_ref,
                     h1_ref, hp_ref, idx_ref, gate_ref, rank_ref, cnt_out_ref, uext, cnt_ref):
    b = pl.program_id(0)
    i = pl.program_id(1)
    tm = x_ref.shape[0]
    halo = 16

    @pl.when(jnp.logical_and(b == 0, i == 0))
    def _():
        cnt_ref[...] = jnp.zeros_like(cnt_ref)

    @pl.when(i == 0)
    def _():
        uext[0:halo, :] = jnp.zeros((halo, POOL_WIDTH), _F32)

    @pl.when(i > 0)
    def _():
        uext[0:halo, :] = uext[tm:tm + halo, :]

    uext[halo:halo + tm, :] = u_ref[...]
    pos = i * tm + lax.broadcasted_iota(jnp.int32, (tm, 1), 0)
    pool = _pool_mix(lambda j, lanes: uext[halo - j:halo - j + tm, lanes], u_ref,
                     lambda size: jnp.minimum(pos + 1, size).astype(_F32), pw_ref, ps_ref)
    cat = jnp.concatenate([a_ref[j] for j in range(N_CHUNKS)] + [pool.astype(_BF16)], axis=1)
    mix = jnp.dot(cat, wo_ref[...], preferred_element_type=_F32)
    h1 = _layer_norm(alpha * x_ref[...] + mix, g_ref[...], b_ref[...])
    h1_ref[...] = h1
    hp_ref[...] = _pack_bf16_pairs(h1)
    _route(h1, wr_ref, br_ref, cnt_ref, idx_ref, gate_ref, rank_ref)
    cnt_out_ref[...] = cnt_ref[...]


def _mix_prompt(alpha, x, attn, u, w_out_bf16, pool_w_bf16, pool_scale, ln_g, ln_b, wr_t, br):
    B, S, D = x.shape
    n_total = B * S
    tm = PROJ_TILE
    n_tiles = S // tm
    tok = lambda b, i: (b * n_tiles + i, 0)
    tok_t = lambda b, i: (0, b * n_tiles + i)
    const2 = lambda b, i: (0, 0)
    return pl.pallas_call(
        functools.partial(_mix_prompt_body, alpha),
        grid=(B, n_tiles),
        in_specs=[pl.BlockSpec((None, tm, D), lambda b, i: (b, i, 0)),
                  pl.BlockSpec((None, N_CHUNKS, tm, LANES), lambda b, i: (b, 0, i, 0)),
                  pl.BlockSpec((None, tm, POOL_WIDTH), lambda b, i: (b, i, 0)),
                  pl.BlockSpec(w_out_bf16.shape, const2),
                  pl.BlockSpec(pool_w_bf16.shape, lambda b, i: (0, 0, 0)),
                  pl.BlockSpec(pool_scale.shape, const2),
                  pl.BlockSpec(ln_g.shape, const2), pl.BlockSpec(ln_b.shape, const2),
                  pl.BlockSpec(wr_t.shape, const2), pl.BlockSpec(br.shape, const2)],
        out_specs=[pl.BlockSpec((tm, D), tok), pl.BlockSpec((tm, D // 2), tok),
                   pl.BlockSpec((TOP_K, tm), tok_t), pl.BlockSpec((TOP_K, tm), tok_t),
                   pl.BlockSpec((TOP_K, tm), tok_t), pl.BlockSpec((N_EXPERTS, 1), const2)],
        out_shape=[jax.ShapeDtypeStruct((n_total, D), _F32), jax.ShapeDtypeStruct((n_total, D // 2), jnp.uint32),
                   jax.ShapeDtypeStruct((TOP_K, n_total), jnp.int32), jax.ShapeDtypeStruct((TOP_K, n_total), _F32),
                   jax.ShapeDtypeStruct((TOP_K, n_total), jnp.int32), jax.ShapeDtypeStruct((N_EXPERTS, 1), _F32)],
        scratch_shapes=[pltpu.VMEM((16 + tm, POOL_WIDTH), _F32), pltpu.VMEM((N_EXPERTS, 1), _F32)],
        compiler_params=_params(("arbitrary", "arbitrary")),
        name="mix_prompt",
    )(x, attn, u, w_out_bf16, pool_w_bf16, pool_scale, ln_g, ln_b, wr_t, br)


def _inproj_sample_body(x_ref, w_ref, o_ref):
    o_ref[...] = jnp.dot(x_ref[...].astype(_BF16), w_ref[...], preferred_element_type=_F32)


def _inproj_sample(x2d, w_in_bf16):
    return pl.pallas_call(
        _inproj_sample_body,
        out_shape=jax.ShapeDtypeStruct((x2d.shape[0], w_in_bf16.shape[1]), _F32),
        compiler_params=_params(None),
        name="inproj_sample",
    )(x2d, w_in_bf16)


def _decode_tables(T, Lb):
    slopes = _alibi_slopes()

    def mult(delta):
        m = np.zeros(delta.shape, np.float64)
        for win, dil in zip(WINDOWS, DILATIONS):
            m += ((delta % dil == 0) & (delta // dil <= win // dil) & (delta >= 0))
        return m

    t = np.arange(T)[:, None]
    d_cache = Lb + t - np.arange(Lb)[None, :]
    d_new = t - np.arange(T)[None, :]
    m_cache, m_new = mult(d_cache), mult(d_new)
    b_cache = np.where(m_cache > 0, -slopes[:, None, None] * d_cache[None], NEG_INF)
    b_new = np.where(m_new > 0, -slopes[:, None, None] * d_new[None], NEG_INF)
    f = lambda a: jnp.asarray(a.astype(np.float32))
    return f(b_cache), f(m_cache), f(b_new), f(m_new)


def _attn_sample_body(q_ref, kn_ref, vn_ref, knt_ref, vnt_ref, kc_ref, vc_ref, bc_ref, mc_ref, bn_ref, mn_ref,
                      o_ref, ko_ref, vo_ref):
    T = q_ref.shape[0]
    scale = HEAD_DIM ** -0.5
    rnd = lambda a: a.astype(_BF16).astype(_F32)
    new_col = lax.broadcasted_iota(jnp.int32, (T, T), 1)
    outs = []
    for h in range(N_HEADS):
        lanes = slice(h * HEAD_DIM, (h + 1) * HEAD_DIM)
        q = rnd(q_ref[:, lanes] * scale)
        kn = rnd(kn_ref[:, lanes])
        vn = rnd(vn_ref[:, lanes])
        s_c = jnp.dot(q.astype(_BF16), kc_ref[h].astype(_BF16), preferred_element_type=_F32) + bc_ref[h]
        s_n = bn_ref[h]
        for t in range(T):
            s_n = s_n + jnp.where(new_col == t, jnp.sum(q * kn[t:t + 1, :], axis=-1, keepdims=True), 0.0)
        m = jnp.maximum(jnp.max(s_c, axis=-1, keepdims=True), jnp.max(s_n, axis=-1, keepdims=True))
        e_c = mc_ref[...] * jnp.exp(s_c - m)
        e_n = mn_ref[...] * jnp.exp(s_n - m)
        den = jnp.sum(e_c, axis=-1, keepdims=True) + jnp.sum(e_n, axis=-1, keepdims=True)
        num = lax.dot_general(e_c.astype(_BF16), vc_ref[h].astype(_BF16), (((1,), (1,)), ((), ())),
                              preferred_element_type=_F32)
        e_nr = rnd(e_n)
        for t in range(T):
            num = num + e_nr[:, t:t + 1] * vn[t:t + 1, :]
        outs.append(num / den)
    o_ref[...] = jnp.concatenate(outs, axis=1)
    ko_ref[...] = jnp.concatenate([kc_ref[...][:, :, T:], knt_ref[...]], axis=-1)
    vo_ref[...] = jnp.concatenate([vc_ref[...][:, :, T:], vnt_ref[...]], axis=-1)


def _attn_sample(q, kn, vn, knt, vnt, kc, vc, tables):
    Bd, T, _ = q.shape
    Lb = kc.shape[-1]
    tok = pl.BlockSpec((None, T, ATTN_WIDTH), lambda b: (b, 0, 0))
    newt = pl.BlockSpec((None, N_HEADS, HEAD_DIM, T), lambda b: (b, 0, 0, 0))
    cache = pl.BlockSpec((None, N_HEADS, HEAD_DIM, Lb), lambda b: (b, 0, 0, 0))
    tabs = [pl.BlockSpec(t.shape, (lambda b, n=t.ndim: (0,) * n)) for t in tables]
    return pl.pallas_call(
        _attn_sample_body,
        grid=(Bd,),
        in_specs=[tok, tok, tok, newt, newt, cache, cache] + tabs,
        out_specs=[tok, cache, cache],
        out_shape=[jax.ShapeDtypeStruct((Bd, T, ATTN_WIDTH), _F32),
                   jax.ShapeDtypeStruct(kc.shape, _F32), jax.ShapeDtypeStruct(vc.shape, _F32)],
        compiler_params=_params(("arbitrary",)),
        name="attn_sample",
    )(q, kn, vn, knt, vnt, kc, vc, *tables)


def _mix_sample_body(alpha, T, x_ref, a_ref, ue_ref, wo_ref, pw_ref, ps_ref, g_ref, b_ref, wr_ref, br_ref, cnt_in_ref,
                     h1_ref, hp_ref, idx_ref, gate_ref, rank_ref, cnt_out_ref, cnt_ref):
    cnt_ref[...] = cnt_in_ref[...]
    hist = POOL_STATE_LEN
    pools = []
    for t in range(T):
        pools.append(_pool_mix(lambda j, lanes, t=t: ue_ref[hist + t - j, :, lanes], ue_ref[hist + t],
                               lambda size: float(size), pw_ref, ps_ref))
    pool = jnp.concatenate(pools, axis=0)
    cat = jnp.concatenate([a_ref[...].astype(_BF16), pool.astype(_BF16)], axis=1)
    mix = jnp.dot(cat, wo_ref[...], preferred_element_type=_F32)
    h1 = _layer_norm(alpha * x_ref[...] + mix, g_ref[...], b_ref[...])
    h1_ref[...] = h1
    hp_ref[...] = _pack_bf16_pairs(h1)
    _route(h1, wr_ref, br_ref, cnt_ref, idx_ref, gate_ref, rank_ref)
    cnt_out_ref[...] = cnt_ref[...]


def _mix_sample(alpha, T, x_tm, attn_tm, uext_tm, w_out_bf16, pool_w_bf16, pool_scale, ln_g, ln_b, wr_t, br, cnt_in):
    n_s, D = x_tm.shape
    full = lambda shape: pl.BlockSpec(shape, lambda i, n=len(shape): (0,) * n)
    ins = [x_tm, attn_tm, uext_tm, w_out_bf16, pool_w_bf16, pool_scale, ln_g, ln_b, wr_t, br, cnt_in]
    outs = [jax.ShapeDtypeStruct((n_s, D), _F32), jax.ShapeDtypeStruct((n_s, D // 2), jnp.uint32),
            jax.ShapeDtypeStruct((TOP_K, n_s), jnp.int32), jax.ShapeDtypeStruct((TOP_K, n_s), _F32),
            jax.ShapeDtypeStruct((TOP_K, n_s), jnp.int32), jax.ShapeDtypeStruct((N_EXPERTS, 1), _F32)]
    return pl.pallas_call(
        functools.partial(_mix_sample_body, alpha, T),
        grid=(1,),
        in_specs=[full(a.shape) for a in ins],
        out_specs=[full(o.shape) for o in outs],
        out_shape=outs,
        scratch_shapes=[pltpu.VMEM((N_EXPERTS, 1), _F32)],
        compiler_params=_params(("arbitrary",)),
        name="mix_sample",
    )(*ins)


def _dispatch_body(n_p_tiles, zflag_ref, dest_hbm, hp_p_ref, hp_s_ref, xs_hbm, tile, zeros, dest_smem,
                   sem_idx, sem_rows, sem_zero):
    i = pl.program_id(0)
    n_steps = pl.num_programs(0)
    tm = tile.shape[1]

    @pl.when(i == 0)
    def _():
        zeros[...] = jnp.zeros_like(zeros)

        blk = zeros.shape[0]

        def zero_copy(j):
            return pltpu.make_async_copy(zeros, xs_hbm.at[pl.ds(pl.multiple_of(j * blk, blk), blk)], sem_zero)

        def start(j, carry):
            @pl.when(zflag_ref[j] != 0)
            def _():
                zero_copy(j).start()
            return carry

        def wait(j, carry):
            @pl.when(zflag_ref[j] != 0)
            def _():
                zero_copy(j).wait()
            return carry

        lax.fori_loop(0, zflag_ref.shape[0], start, 0)
        lax.fori_loop(0, zflag_ref.shape[0], wait, 0)

    def idx_copy(t, s):
        return pltpu.make_async_copy(dest_hbm.at[t], dest_smem.at[pl.ds(s * TOP_K * tm, TOP_K * tm)], sem_idx.at[s])

    def wait_rows(s):
        for _ in range(TOP_K):
            pltpu.make_async_copy(tile.at[s], xs_hbm.at[pl.ds(0, tm)], sem_rows.at[s]).wait()

    @pl.when(i == 0)
    def _():
        idx_copy(i, 0).start()

    staged = jnp.where(i < n_p_tiles, hp_p_ref[...], hp_s_ref[...])

    for s in range(2):
        @pl.when(i % 2 == s)
        def _(s=s):
            @pl.when(i + 1 < n_steps)
            def _():
                idx_copy(i + 1, 1 - s).start()

            tile[s] = staged
            idx_copy(i, s).wait()

            for n in range(tm):
                for kk in range(TOP_K):
                    pltpu.make_async_copy(tile.at[s, pl.ds(n, 1)], xs_hbm.at[pl.ds(dest_smem[(s * TOP_K + kk) * tm + n], 1)],
                                          sem_rows.at[s]).start(priority=kk % 2)

            @pl.when(i > 0)
            def _():
                wait_rows(1 - s)

            @pl.when(i == n_steps - 1)
            def _():
                wait_rows(s)


def _dispatch(zero_flag, dest_tiles, hp_p, hp_s, n_rows):
    w = hp_p.shape[1]
    tm = ROW_TILE
    n_p_tiles, n_s_tiles = hp_p.shape[0] // tm, hp_s.shape[0] // tm
    grid_spec = pltpu.PrefetchScalarGridSpec(
        num_scalar_prefetch=1,
        grid=(n_p_tiles + n_s_tiles,),
        in_specs=[pl.BlockSpec(memory_space=pl.ANY),
                  pl.BlockSpec((tm, w), lambda i, z: (jnp.minimum(i, n_p_tiles - 1), 0)),
                  pl.BlockSpec((tm, w), lambda i, z: (jnp.maximum(i - n_p_tiles, 0), 0))],
        out_specs=pl.BlockSpec(memory_space=pl.ANY),
        scratch_shapes=[pltpu.VMEM((2, tm, w), jnp.uint32), pltpu.VMEM((MOE_BLOCK, w), jnp.uint32),
                        pltpu.SMEM((2 * TOP_K * tm,), jnp.int32),
                        pltpu.SemaphoreType.DMA((2,)), pltpu.SemaphoreType.DMA((2,)), pltpu.SemaphoreType.DMA(())],
    )
    return pl.pallas_call(
        functools.partial(_dispatch_body, n_p_tiles),
        grid_spec=grid_spec,
        out_shape=jax.ShapeDtypeStruct((n_rows, w), jnp.uint32),
        compiler_params=_params(("arbitrary",)),
        name="moe_dispatch",
    )(zero_flag, dest_tiles, hp_p, hp_s)


def _experts_body(be_ref, na_ref, nv_ref, eseq_ref, enext_ref, xs_ref, wu_hbm, bu_ref, wd_hbm, bd_ref, ys_ref,
                  wu_bf, wd_bf, wu_f32, wd_f32, sem_w):
    j = pl.program_id(0)
    blk = xs_ref.shape[0]
    d_ff = wd_bf.shape[0]
    active = j < na_ref[0]
    changed = jnp.logical_or(j == 0, be_ref[j] != be_ref[jnp.maximum(j - 1, 0)])

    def weight_copies(e, s):
        return (pltpu.make_async_copy(wu_hbm.at[e], wu_f32.at[s], sem_w.at[s]),
                pltpu.make_async_copy(wd_hbm.at[e], wd_f32.at[s], sem_w.at[s]))

    @pl.when(j == 0)
    def _():
        for c in weight_copies(be_ref[0], 0):
            c.start()

    for s in range(2):
        @pl.when(jnp.logical_and(jnp.logical_and(active, changed), eseq_ref[j] % 2 == s))
        def _(s=s):
            for c in weight_copies(be_ref[j], s):
                c.wait()

            @pl.when(enext_ref[j] >= 0)
            def _():
                for c in weight_copies(enext_ref[j], 1 - s):
                    c.start()

            wu_bf[...] = wu_f32[s].astype(_BF16)
            wd_bf[...] = wd_f32[s].astype(_BF16)

    def ffn(rows):
        x = _unpack_bf16_pairs(xs_ref[0:rows, :])
        h = jnp.dot(x, wu_bf[...], preferred_element_type=_F32) + bu_ref[...]
        g = jnp.minimum(h[:, :d_ff], SWIGLU_LIMIT)
        lin = jnp.clip(h[:, d_ff:], -SWIGLU_LIMIT, SWIGLU_LIMIT)
        act = g * jax.nn.sigmoid(SWIGLU_ALPHA * g) * (lin + 1.0)
        y = jnp.dot(act.astype(_BF16), wd_bf[...], preferred_element_type=_F32) + bd_ref[...]
        ys_ref[0:rows, :] = _pack_bf16_pairs(y)
        if rows < blk:
            ys_ref[rows:, :] = jnp.zeros((blk - rows, ys_ref.shape[1]), ys_ref.dtype)

    quarter = blk // EXPERT_ROW_STEPS
    n_quarters = (nv_ref[j] + quarter - 1) // quarter
    for q in range(1, EXPERT_ROW_STEPS + 1):
        @pl.when(jnp.logical_and(active, n_quarters == q))
        def _(q=q):
            ffn(q * quarter)

    @pl.when(jnp.logical_or(jnp.logical_not(active), n_quarters == 0))
    def _():
        ys_ref[...] = jnp.zeros_like(ys_ref)


def _experts(block_e, n_active, n_valid, expert_seq, expert_next, xs, w_up, b_up, w_down, b_down):
    n_rows, w = xs.shape
    blk = MOE_BLOCK
    n_blocks = n_rows // blk
    d_model, d_ff2 = w_up.shape[1:]
    d_ff = w_down.shape[1]
    act_blk = lambda j, be, na, *_: jnp.minimum(j, jnp.maximum(na[0] - 1, 0))
    grid_spec = pltpu.PrefetchScalarGridSpec(
        num_scalar_prefetch=5,
        grid=(n_blocks,),
        in_specs=[pl.BlockSpec((blk, w), lambda j, be, na, *_: (act_blk(j, be, na), 0)),
                  pl.BlockSpec(memory_space=pl.ANY),
                  pl.BlockSpec((None, 1, d_ff2), lambda j, be, *_: (be[j], 0, 0)),
                  pl.BlockSpec(memory_space=pl.ANY),
                  pl.BlockSpec((None, 1, d_model), lambda j, be, *_: (be[j], 0, 0))],
        out_specs=pl.BlockSpec((blk, d_model // 2), lambda j, *_: (j, 0)),
        scratch_shapes=[pltpu.VMEM((d_model, d_ff2), _BF16), pltpu.VMEM((d_ff, d_model), _BF16),
                        pltpu.VMEM((2, d_model, d_ff2), _F32), pltpu.VMEM((2, d_ff, d_model), _F32),
                        pltpu.SemaphoreType.DMA((2,))],
    )
    return pl.pallas_call(
        _experts_body,
        grid_spec=grid_spec,
        out_shape=jax.ShapeDtypeStruct((n_rows, d_model // 2), jnp.uint32),
        compiler_params=_params(("arbitrary",)),
        name="moe_experts",
    )(block_e, n_active, n_valid, expert_seq, expert_next, xs, w_up, b_up, w_down, b_down)


def _combine_body(alpha, tile0, dest_hbm, ys_hbm, h1_ref, gate_ref, g_ref, b_ref, o_ref,
                  dest_smem, buf, sem_idx, sem_rows):
    i = pl.program_id(0)
    n_steps = pl.num_programs(0)
    tm = o_ref.shape[0]
    group = 32

    def idx_copy(t, s):
        return pltpu.make_async_copy(dest_hbm.at[tile0 + t], dest_smem.at[pl.ds(s * TOP_K * tm, TOP_K * tm)],
                                     sem_idx.at[s])

    def gather_tile(s):
        for n in range(tm):
            for kk in range(TOP_K):
                row = dest_smem[(s * TOP_K + kk) * tm + n]
                pltpu.make_async_copy(ys_hbm.at[pl.ds(row, 1)], buf.at[s, kk, pl.ds(n, 1)],
                                      sem_rows.at[s]).start(priority=kk % 2)

    def wait_rows(s):
        for kk in range(TOP_K):
            pltpu.make_async_copy(ys_hbm.at[pl.ds(0, tm)], buf.at[s, kk], sem_rows.at[s]).wait()

    def reduce_rows(s, r0):
        rows = pl.ds(r0, group)
        lo = hi = None
        for kk in range(TOP_K):
            words = buf[s, kk, rows, :]
            gate = gate_ref[rows, kk:kk + 1]
            t_lo = gate * pltpu.bitcast(words << 16, _F32)
            t_hi = gate * pltpu.bitcast(words & jnp.uint32(0xFFFF0000), _F32)
            lo, hi = (t_lo, t_hi) if kk == 0 else (lo + t_lo, hi + t_hi)
        o_ref[rows, :] = _layer_norm(alpha * h1_ref[rows, :] + jnp.concatenate([lo, hi], axis=1), g_ref[...], b_ref[...])

    def row_groups(fn, unroll=1):
        def body(gi, carry):
            fn(pl.multiple_of(gi * group, group))
            return carry
        lax.fori_loop(0, tm // group, body, 0, unroll=unroll)

    @pl.when(i == 0)
    def _():
        idx_copy(i, 0).start()
        idx_copy(i, 0).wait()
        gather_tile(0)

        @pl.when(n_steps > 1)
        def _():
            idx_copy(i + 1, 1).start()

    for s in range(2):
        @pl.when(i % 2 == s)
        def _(s=s):
            @pl.when(i + 1 < n_steps)
            def _():
                idx_copy(i + 1, 1 - s).wait()
                gather_tile(1 - s)

            @pl.when(i + 2 < n_steps)
            def _():
                idx_copy(i + 2, s).start()

            wait_rows(s)
            row_groups(lambda r0: reduce_rows(s, r0), unroll=tm // group)


def _combine(alpha, tile0, dest_tiles, ys, h1, gate_rows, ln_g, ln_b):
    n_tok, d_model = h1.shape
    tm = ROW_TILE
    const2 = lambda i: (0, 0)
    return pl.pallas_call(
        functools.partial(_combine_body, alpha, tile0),
        grid=(n_tok // tm,),
        in_specs=[pl.BlockSpec(memory_space=pl.ANY), pl.BlockSpec(memory_space=pl.ANY),
                  pl.BlockSpec((tm, d_model), lambda i: (i, 0)),
                  pl.BlockSpec((tm, TOP_K), lambda i: (tile0 + i, 0)),
                  pl.BlockSpec(ln_g.shape, const2), pl.BlockSpec(ln_b.shape, const2)],
        out_specs=pl.BlockSpec((tm, d_model), lambda i: (i, 0)),
        out_shape=jax.ShapeDtypeStruct(h1.shape, _F32),
        scratch_shapes=[pltpu.SMEM((2 * TOP_K * tm,), jnp.int32), pltpu.VMEM((2, TOP_K, tm, ys.shape[1]), ys.dtype),
                        pltpu.SemaphoreType.DMA((2,)), pltpu.SemaphoreType.DMA((2,))],
        compiler_params=_params(("arbitrary",)),
        name="moe_combine_tile%d" % tile0,
    )(dest_tiles, ys, h1, gate_rows, ln_g, ln_b)


def _moe_layout(idx_all, rank_all, counts, n_total):
    blk = MOE_BLOCK
    n_blocks = (TOP_K * n_total + N_EXPERTS * (blk - 1)) // blk + 1
    cnt = counts.reshape(N_EXPERTS).astype(jnp.int32)
    padded = (cnt + blk - 1) // blk * blk
    pend = jnp.cumsum(padded)
    pstart = pend - padded
    experts = jnp.arange(N_EXPERTS, dtype=jnp.int32)
    dest = rank_all + jnp.sum(jnp.where(idx_all[None] == experts[:, None, None], pstart[:, None, None], 0), axis=0)
    block_row0 = jnp.arange(n_blocks, dtype=jnp.int32) * blk
    block_e = jnp.minimum(jnp.sum((pend[None, :] <= block_row0[:, None]).astype(jnp.int32), axis=1), N_EXPERTS - 1)
    n_active = (pend[-1] // blk).astype(jnp.int32).reshape(1)
    last_e = block_e[jnp.maximum(n_active[0] - 1, 0)]
    block_e = jnp.where(jnp.arange(n_blocks) < n_active[0], block_e, last_e)
    tm = ROW_TILE
    dest_tiles = dest.reshape(TOP_K, n_total // tm, tm).transpose(1, 0, 2).reshape(n_total // tm, TOP_K * tm)
    blocks = jnp.arange(n_blocks, dtype=jnp.int32)
    is_last = jnp.any((pend[None, :] == block_row0[:, None] + blk) & (cnt[None, :] > 0), axis=1)
    zero_flag = (is_last | (blocks >= n_active[0])).astype(jnp.int32)
    is_e = block_e[:, None] == experts[None, :]
    rows_left = jnp.sum(jnp.where(is_e, (pstart + cnt)[None, :], 0), axis=1) - block_row0
    n_valid = jnp.where(blocks < n_active[0], jnp.clip(rows_left, 0, blk), 0).astype(jnp.int32)
    switch = (blocks < n_active[0]) & jnp.concatenate([jnp.ones((1,), bool), block_e[1:] != block_e[:-1]])
    expert_seq = (jnp.cumsum(switch.astype(jnp.int32)) - 1).astype(jnp.int32)
    switch_at = jnp.where(switch, blocks, n_blocks)
    next_switch = jnp.concatenate([lax.cummin(switch_at[::-1])[::-1][1:], jnp.full((1,), n_blocks, jnp.int32)])
    next_e = jnp.sum(jnp.where(blocks[None, :] == next_switch[:, None], block_e[None, :], 0), axis=1)
    expert_next = jnp.where(next_switch < n_blocks, next_e, -1).astype(jnp.int32)
    return dest_tiles, block_e, n_active, n_valid, expert_seq, expert_next, zero_flag, n_blocks * blk


def kernel(x_prompt, x_sample, cache_attn_k, cache_attn_v, state_pool, w_in, w_out, pool_w, pool_scale, ln1_g, ln1_b,
           w_router, b_router, w_up, b_up, w_down, b_down, ln2_g, ln2_b):
    depth = w_in.shape[0]
    assert depth == 1, "single-layer step"
    B, S, D = x_prompt.shape
    Bd, T, _ = x_sample.shape
    Lb = cache_attn_k.shape[2]
    assert S % ATTN_TILE == 0 and Lb == WINDOW_MAX and (B * S) % ROW_TILE == 0 and (Bd * T) % ROW_TILE == 0
    alpha = (2 * depth) ** 0.25
    n_p, n_s = B * S, Bd * T
    n_total = n_p + n_s

    w_in_b = w_in[0].astype(_BF16)
    w_out_b = w_out[0].astype(_BF16)
    pool_w_b = pool_w[0].astype(_BF16)
    wr_f32 = w_router[0].T
    wr_hi = wr_f32.astype(_BF16)
    wr_t = jnp.concatenate([wr_hi, (wr_f32 - wr_hi.astype(_F32)).astype(_BF16)], axis=0)
    br = b_router[0].reshape(N_EXPERTS, 1)

    q, k, v, u, kt, vt = _inproj_prompt(x_prompt, w_in_b)
    attn = _attn_prompt(q, k, v, _attn_bias_tables())
    h1_p, hp_p, idx_p, gate_p, rank_p, cnt_p = _mix_prompt(
        alpha, x_prompt, attn, u, w_out_b, pool_w_b, pool_scale, ln1_g, ln1_b, wr_t, br)
    n_keep = kt.shape[-1]
    k_prompt = kt.reshape(1, B, N_HEADS, HEAD_DIM, n_keep).transpose(0, 1, 4, 2, 3)
    v_prompt = vt.reshape(1, B, N_HEADS, HEAD_DIM, n_keep).transpose(0, 1, 4, 2, 3)
    pool_prompt = u[:, S - POOL_STATE_LEN:][None]

    hs = _inproj_sample(x_sample.reshape(n_s, D), w_in_b)
    qs = hs[:, :ATTN_WIDTH].reshape(Bd, T, ATTN_WIDTH)
    ks = hs[:, ATTN_WIDTH:2 * ATTN_WIDTH].reshape(Bd, T, ATTN_WIDTH)
    vs = hs[:, 2 * ATTN_WIDTH:3 * ATTN_WIDTH].reshape(Bd, T, ATTN_WIDTH)
    us = hs[:, 3 * ATTN_WIDTH:].reshape(Bd, T, POOL_WIDTH)
    to_t = lambda a: a.reshape(Bd, T, N_HEADS, HEAD_DIM).transpose(0, 2, 3, 1)
    kc = cache_attn_k[0].transpose(0, 2, 3, 1)
    vc = cache_attn_v[0].transpose(0, 2, 3, 1)
    attn_s, k_new, v_new = _attn_sample(qs, ks, vs, to_t(ks), to_t(vs), kc, vc, _decode_tables(T, Lb))
    k_sample = k_new.transpose(0, 3, 1, 2)[None]
    v_sample = v_new.transpose(0, 3, 1, 2)[None]
    uext_tm = jnp.concatenate([state_pool[0].transpose(1, 0, 2), us.transpose(1, 0, 2)], axis=0)
    pool_sample = uext_tm[T:].transpose(1, 0, 2)[None]
    x_tm = x_sample.transpose(1, 0, 2).reshape(n_s, D)
    attn_tm = attn_s.transpose(1, 0, 2).reshape(n_s, ATTN_WIDTH)
    h1_s, hp_s, idx_s, gate_s, rank_s, counts = _mix_sample(
        alpha, T, x_tm, attn_tm, uext_tm, w_out_b, pool_w_b, pool_scale, ln1_g, ln1_b, wr_t, br, cnt_p)

    idx_all = jnp.concatenate([idx_p, idx_s], axis=1)
    rank_all = jnp.concatenate([rank_p, rank_s], axis=1)
    gate_rows = jnp.concatenate([gate_p, gate_s], axis=1).T
    dest_tiles, block_e, n_active, n_valid, expert_seq, expert_next, zero_flag, n_rows = _moe_layout(
        idx_all, rank_all, counts, n_total)
    xs = _dispatch(zero_flag, dest_tiles, hp_p, hp_s, n_rows)
    ys = _experts(block_e, n_active, n_valid, expert_seq, expert_next, xs,
                  w_up[0], b_up[0][:, None, :], w_down[0], b_down[0][:, None, :])
    out_p = _combine(alpha, 0, dest_tiles, ys, h1_p, gate_rows, ln2_g, ln2_b)
    out_s = _combine(alpha, n_p // ROW_TILE, dest_tiles, ys, h1_s, gate_rows, ln2_g, ln2_b)

    y_prompt = out_p.reshape(B, S, D)
    y_sample = out_s.reshape(T, Bd, D).transpose(1, 0, 2)
    return (y_prompt, y_sample, k_prompt, v_prompt, pool_prompt, k_sample, v_sample, pool_sample)
```
